```python
import jax
import jax.numpy as jnp
from jax import lax
import numpy as np

D_MODEL = 2048
BATCH = 4
SEQ = 4096
DEPTH = 2

GRID_W = 64
CTX_LEN = 256
HEAD_DIM = 128
N_MIX_HEADS = D_MODEL // HEAD_DIM
NA_HEADS = N_MIX_HEADS // 4
GQA_Q_HEADS = N_MIX_HEADS // 4
GQA_KV_HEADS = GQA_Q_HEADS // 2
ML_HEADS = N_MIX_HEADS // 2
NA_W = NA_HEADS * HEAD_DIM
GQA_W = GQA_Q_HEADS * HEAD_DIM
KV_W = GQA_KV_HEADS * HEAD_DIM
ML_W = ML_HEADS * HEAD_DIM
NA_WIN_R = 8
NA_WIN_C = 16
ATTN_BLOCK = 128
ML_CHUNK = 128
ML_CONV = 3
ROPE_THETA = 10000.0
D_FF = 256 * ((8 * D_MODEL // 3 + 255) // 256)
N_MOD = 9
EPS = 1e-6
IN_SPLITS = (NA_W, NA_W, NA_W, GQA_W, KV_W, KV_W, 2 * ML_W, ML_W, ML_W, 4 * ML_HEADS)
IN_W = 3 * NA_W + GQA_W + 2 * KV_W + 4 * ML_W + 4 * ML_HEADS

kernel_name = "hybrid_natten_gqa_mlstm_macaron_dit"


def rms_norm(x, gain):
    xf = x.astype(jnp.float32)
    y = xf * lax.rsqrt(jnp.mean(xf * xf, axis=-1, keepdims=True) + EPS)
    return (y * gain.astype(jnp.float32)).astype(x.dtype)


def modulate(h, shift, scale):
    return h * (1.0 + scale) + shift


def swiglu(h, w_gate, w_up, w_down):
    return (jax.nn.silu(h @ w_gate) * (h @ w_up)) @ w_down


def macaron_ffn(x, gain, mod, w_gate, w_up, w_down):
    shift, scale, gate = mod
    h = modulate(rms_norm(x, gain), shift, scale)
    return x + 0.5 * gate * swiglu(h, w_gate, w_up, w_down)


def to_heads(t, n_heads):
    b, s, _ = t.shape
    return t.reshape(b, s, n_heads, -1).transpose(0, 2, 1, 3)


def from_heads(t):
    b, h, s, d = t.shape
    return t.transpose(0, 2, 1, 3).reshape(b, s, h * d)


def split_columns(p):
    parts, start = [], 0
    for width in IN_SPLITS:
        parts.append(p[..., start:start + width])
        start += width
    return parts


def axial_rope_tables(n_tokens, dim):
    t = jnp.arange(n_tokens)
    row = (t // GRID_W).astype(jnp.float32)
    col = (t % GRID_W).astype(jnp.float32)
    half = dim // 2
    inv_freq = 1.0 / (ROPE_THETA ** (jnp.arange(0, half, 2, dtype=jnp.float32) / half))
    ang_r = row[:, None] * inv_freq[None, :]
    ang_c = col[:, None] * inv_freq[None, :]
    ang = jnp.concatenate([ang_r, ang_r, ang_c, ang_c], axis=-1)
    return jnp.cos(ang), jnp.sin(ang)


def apply_axial_rope(x, cos, sin):
    half = x.shape[-1] // 2

    def rotate_half(u):
        u1, u2 = jnp.split(u, 2, axis=-1)
        return jnp.concatenate([-u2, u1], axis=-1)

    rot = jnp.concatenate([rotate_half(x[..., :half]), rotate_half(x[..., half:])], axis=-1)
    return (x * cos + rot * sin).astype(x.dtype)


def grouped_attention(q, k, v):
    s = jnp.einsum('bhgqd,bhkd->bhgqk', q, k).astype(jnp.float32) * (q.shape[-1] ** -0.5)
    p = jax.nn.softmax(s, axis=-1).astype(v.dtype)
    return jnp.einsum('bhgqk,bhkd->bhgqd', p, v)


def blocked_attention(q, k, v):
    b, hkv, g, t, d = q.shape
    nb = t // ATTN_BLOCK
    qb = q.reshape(b, hkv, g, nb, ATTN_BLOCK, d).transpose(3, 0, 1, 2, 4, 5)
    out = lax.map(lambda qi: grouped_attention(qi, k, v), qb)
    return out.transpose(1, 2, 3, 0, 4, 5).reshape(b, hkv, g, t, d)


def neighbourhood_attention(q, k, v, k_ctx, v_ctx, rpb):
    b, h, t, d = q.shape
    rows = t // GRID_W
    win_r = min(NA_WIN_R, rows)
    r = jnp.arange(rows)
    row_start = jnp.clip(r - win_r // 2, 0, rows - win_r)
    key_rows = row_start[:, None] + jnp.arange(win_r)[None, :]
    cq = jnp.arange(GRID_W)
    col_start = jnp.clip(cq - NA_WIN_C // 2, 0, GRID_W - NA_WIN_C)
    col_in = (cq[None, :] >= col_start[:, None]) & (cq[None, :] < col_start[:, None] + NA_WIN_C)
    qg = q.reshape(b, h, rows, GRID_W, d)
    kg = k.reshape(b, h, rows, GRID_W, d)[:, :, key_rows]
    vg = v.reshape(b, h, rows, GRID_W, d)[:, :, key_rows]
    scale = d ** -0.5
    s_loc = jnp.einsum('bhrqd,bhrikd->bhrqik', qg, kg).astype(jnp.float32) * scale
    dr = key_rows - r[:, None] + (NA_WIN_R - 1)
    dc = jnp.clip(cq[None, :] - cq[:, None], -(NA_WIN_C - 1), NA_WIN_C - 1) + (NA_WIN_C - 1)
    bias = rpb[:, dr[:, None, :, None], dc[None, :, None, :]]
    s_loc = jnp.where(col_in[:, None, :], s_loc + bias.astype(jnp.float32), -jnp.inf)
    s_ctx = jnp.einsum('bhrqd,bhcd->bhrqc', qg, k_ctx).astype(jnp.float32) * scale
    n_loc = win_r * GRID_W
    logits = jnp.concatenate([s_loc.reshape(b, h, rows, GRID_W, n_loc), s_ctx], axis=-1)
    p = jax.nn.softmax(logits, axis=-1).astype(v.dtype)
    p_loc = p[..., :n_loc].reshape(b, h, rows, GRID_W, win_r, GRID_W)
    out = (jnp.einsum('bhrqik,bhrikd->bhrqd', p_loc, vg)
           + jnp.einsum('bhrqc,bhcd->bhrqd', p[..., n_loc:], v_ctx))
    return out.reshape(b, h, t, d)


def centred_depthwise_conv(x, w, bias):
    n_ch = x.shape[-1]
    kw = w.shape[0]
    y = lax.conv_general_dilated(x, w[:, None, :].astype(x.dtype), window_strides=(1,),
                                 padding=[(kw // 2, kw // 2)],
                                 dimension_numbers=('NWC', 'WIO', 'NWC'),
                                 feature_group_count=n_ch)
    return y + bias


def mlstm_streams(qk, v, gates, conv_w, conv_b, gate_b):
    f32 = jnp.float32
    qk = jax.nn.silu(centred_depthwise_conv(qk, conv_w, conv_b))
    q, k = jnp.split(qk, 2, axis=-1)
    q = to_heads(q, ML_HEADS).astype(f32)
    k = to_heads(k, ML_HEADS).astype(f32) * (HEAD_DIM ** -0.5)
    v = to_heads(v, ML_HEADS).astype(f32)
    b, t, _ = gates.shape
    g = (gates + gate_b).astype(f32).reshape(b, t, 4, ML_HEADS).transpose(2, 0, 3, 1)
    log_gates = (g[0], jax.nn.log_sigmoid(g[1]), g[2], jax.nn.log_sigmoid(g[3]))
    return q, k, v, log_gates


def mlstm_chunk_states(k, v, b_cum, log_i, state0):
    b_tot = b_cum[..., -1]
    a = b_tot[..., None] - b_cum + log_i
    a_max = jnp.max(a, axis=-1)
    w = jnp.exp(a - a_max[..., None])
    c_loc = jnp.einsum('bhnl,bhnlk,bhnlv->bhnkv', w, k, v)
    n_loc = jnp.einsum('bhnl,bhnlk->bhnk', w, k)

    def step(carry, xs):
        c_prev, n_prev, m_prev = carry
        c_l, n_l, bt, am = xs
        m_new = jnp.maximum(bt + m_prev, am)
        dec = jnp.exp(bt + m_prev - m_new)
        inp = jnp.exp(am - m_new)
        c_new = dec[..., None, None] * c_prev + inp[..., None, None] * c_l
        n_new = dec[..., None] * n_prev + inp[..., None] * n_l
        return (c_new, n_new, m_new), (c_prev, n_prev, m_prev)

    xs = tuple(jnp.moveaxis(t, 2, 0) for t in (c_loc, n_loc, b_tot, a_max))
    final, prev = lax.scan(step, state0, xs)
    prev = tuple(jnp.moveaxis(t, 0, 2) for t in prev)
    return prev, final


def mlstm_chunk_outputs(q, k, v, b_cum, log_i, prev):
    c_prev, n_prev, m_prev = prev
    length = q.shape[-2]
    seen = jnp.tril(jnp.ones((length, length), dtype=bool))
    dmat = b_cum[..., :, None] - b_cum[..., None, :] + log_i[..., None, :]
    dmat = jnp.where(seen, dmat, -jnp.inf)
    inter = b_cum + m_prev[..., None]
    m = jnp.maximum(inter, jnp.max(dmat, axis=-1))
    s = jnp.einsum('bhnqd,bhnsd->bhnqs', q, k) * jnp.exp(dmat - m[..., None])
    g = jnp.exp(inter - m)
    num = g[..., None] * jnp.einsum('bhnqk,bhnkv->bhnqv', q, c_prev) + jnp.einsum('bhnqs,bhnsv->bhnqv', s, v)
    den = g * jnp.einsum('bhnqk,bhnk->bhnq', q, n_prev) + jnp.sum(s, axis=-1)
    return num / jnp.maximum(jnp.abs(den), jnp.exp(-m))[..., None]


def mlstm_scan(q, k, v, log_i, log_f, state0, with_output):
    b, h, t, d = k.shape
    n = t // ML_CHUNK
    chunk = lambda a: a.reshape(b, h, n, ML_CHUNK, *a.shape[3:])
    qc, kc, vc, lic, lfc = chunk(q), chunk(k), chunk(v), chunk(log_i), chunk(log_f)
    b_cum = jnp.cumsum(lfc, axis=-1)
    prev, final = mlstm_chunk_states(kc, vc, b_cum, lic, state0)
    if not with_output:
        return None, final
    hout = mlstm_chunk_outputs(qc, kc, vc, b_cum, lic, prev).reshape(b, h, t, d)
    return hout, final


def mlstm_bidirectional(lat, ctx, need_ctx):
    ql, kl, vl, gl = lat
    qc, kc, vc, gc = ctx
    b, h, _, d = kl.shape
    f32 = jnp.float32
    zero = (jnp.zeros((b, h, d, d), f32), jnp.zeros((b, h, d), f32), jnp.zeros((b, h), f32))
    rev = lambda a: jnp.flip(a, axis=2)
    hc_f, st_f = mlstm_scan(qc, kc, vc, gc[0], gc[1], zero, need_ctx)
    hl_f, _ = mlstm_scan(ql, kl, vl, gl[0], gl[1], st_f, True)
    hc_b, st_b = mlstm_scan(rev(qc), rev(kc), rev(vc), rev(gc[2]), rev(gc[3]), zero, need_ctx)
    hl_b, _ = mlstm_scan(rev(ql), rev(kl), rev(vl), rev(gl[2]), rev(gl[3]), st_b, True)
    h_lat = hl_f + rev(hl_b)
    h_ctx = hc_f + rev(hc_b) if need_ctx else None
    return h_lat, h_ctx


def mlstm_output(hsum, o_pre, out_norm, dtype):
    hn = rms_norm(hsum, out_norm.reshape(ML_HEADS, 1, HEAD_DIM))
    return (from_heads(hn) * jax.nn.sigmoid(o_pre.astype(jnp.float32))).astype(dtype)


def token_mixer(hl, hc, w_in, rpb, q_norm, k_norm, conv_w, conv_b, gate_b, out_norm, need_ctx):
    b, t, _ = hl.shape
    na_q, na_k, na_v, gq_q, gq_k, gq_v, ml_qk, ml_v, ml_o, ml_g = split_columns(hl @ w_in)
    cna_q, cna_k, cna_v, cgq_q, cgq_k, cgq_v, cml_qk, cml_v, cml_o, cml_g = split_columns(hc @ w_in)

    kc_na = to_heads(cna_k, NA_HEADS)
    vc_na = to_heads(cna_v, NA_HEADS)
    a_lat = neighbourhood_attention(to_heads(na_q, NA_HEADS), to_heads(na_k, NA_HEADS),
                                    to_heads(na_v, NA_HEADS), kc_na, vc_na, rpb)

    grp = GQA_Q_HEADS // GQA_KV_HEADS
    cos, sin = axial_rope_tables(t, HEAD_DIM)
    q_b = apply_axial_rope(rms_norm(to_heads(gq_q, GQA_Q_HEADS), q_norm), cos, sin)
    k_b = apply_axial_rope(rms_norm(to_heads(gq_k, GQA_KV_HEADS), k_norm), cos, sin)
    kc_b = rms_norm(to_heads(cgq_k, GQA_KV_HEADS), k_norm)
    vc_b = to_heads(cgq_v, GQA_KV_HEADS)
    k_all = jnp.concatenate([kc_b, k_b], axis=2)
    v_all = jnp.concatenate([vc_b, to_heads(gq_v, GQA_KV_HEADS)], axis=2)
    b_lat = blocked_attention(q_b.reshape(b, GQA_KV_HEADS, grp, t, HEAD_DIM), k_all, v_all)
    b_lat = b_lat.reshape(b, GQA_Q_HEADS, t, HEAD_DIM)

    lat_c = mlstm_streams(ml_qk, ml_v, ml_g, conv_w, conv_b, gate_b)
    ctx_c = mlstm_streams(cml_qk, cml_v, cml_g, conv_w, conv_b, gate_b)
    h_lat, h_ctx = mlstm_bidirectional(lat_c, ctx_c, need_ctx)
    c_lat = mlstm_output(h_lat, ml_o, out_norm, hl.dtype)

    y_lat = jnp.concatenate([from_heads(a_lat), from_heads(b_lat), c_lat], axis=-1)
    if not need_ctx:
        return y_lat, None
    tc = hc.shape[1]
    a_ctx = grouped_attention(to_heads(cna_q, NA_HEADS)[:, :, None], kc_na, vc_na)[:, :, 0]
    q_bc = rms_norm(to_heads(cgq_q, GQA_Q_HEADS), q_norm).reshape(b, GQA_KV_HEADS, grp, tc, HEAD_DIM)
    b_ctx = grouped_attention(q_bc, kc_b, vc_b).reshape(b, GQA_Q_HEADS, tc, HEAD_DIM)
    c_ctx_out = mlstm_output(h_ctx, cml_o, out_norm, hc.dtype)
    y_ctx = jnp.concatenate([from_heads(a_ctx), from_heads(b_ctx), c_ctx_out], axis=-1)
    return y_lat, y_ctx


def setup_inputs(seed: int = 0) -> dict:
    key = jax.random.key(seed)
    ks = iter(jax.random.split(key, 32))
    f32 = jnp.float32
    D = D_MODEL

    def normal(shape, scale):
        return jax.random.normal(next(ks), shape, f32) * scale

    def gain(shape):
        return 1.0 + normal(shape, 0.05)

    lin = jnp.linspace(3.0, 6.0, ML_HEADS, dtype=f32)
    zer = jnp.zeros((ML_HEADS,), f32)
    gate_base = jnp.concatenate([zer, lin, zer, lin])
    return {
        "x": normal((BATCH, SEQ, D), 1.0),
        "c": normal((BATCH, D), 1.0),
        "ctx": normal((BATCH, CTX_LEN, D), 1.0),
        "c_ctx": normal((D,), 1.0),
        "w_ada": normal((DEPTH, D, N_MOD * D), 0.5 * D ** -0.5),
        "b_ada": normal((DEPTH, N_MOD * D), 0.02),
        "norm_ff1": gain((DEPTH, D)),
        "ff1_gate": normal((DEPTH, D, D_FF), D ** -0.5),
        "ff1_up": normal((DEPTH, D, D_FF), D ** -0.5),
        "ff1_down": normal((DEPTH, D_FF, D), D_FF ** -0.5),
        "norm_mix": gain((DEPTH, D)),
        "w_in": normal((DEPTH, D, IN_W), D ** -0.5),
        "na_rpb": normal((DEPTH, NA_HEADS, 2 * NA_WIN_R - 1, 2 * NA_WIN_C - 1), 0.5),
        "gqa_q_norm": gain((DEPTH, HEAD_DIM)),
        "gqa_k_norm": gain((DEPTH, HEAD_DIM)),
        "ml_conv_w": normal((DEPTH, ML_CONV, 2 * ML_W), ML_CONV ** -0.5),
        "ml_conv_b": normal((DEPTH, 2 * ML_W), 0.02),
        "ml_gate_b": gate_base + normal((DEPTH, 4 * ML_HEADS), 0.1),
        "ml_out_norm": gain((DEPTH, ML_W)),
        "w_out": normal((DEPTH, D, D), D ** -0.5),
        "norm_ff2": gain((DEPTH, D)),
        "ff2_gate": normal((DEPTH, D, D_FF), D ** -0.5),
        "ff2_up": normal((DEPTH, D, D_FF), D ** -0.5),
        "ff2_down": normal((DEPTH, D_FF, D), D_FF ** -0.5),
        "final_norm": gain((D,)),
    }


def reference(x, c, ctx, c_ctx, w_ada, b_ada, norm_ff1, ff1_gate, ff1_up, ff1_down, norm_mix,
              w_in, na_rpb, gqa_q_norm, gqa_k_norm, ml_conv_w, ml_conv_b, ml_gate_b, ml_out_norm,
              w_out, norm_ff2, ff2_gate, ff2_up, ff2_down, final_norm):
    xl, xc = x, ctx
    for l in range(DEPTH):
        last = l == DEPTH - 1
        mod_l = jnp.split((jax.nn.silu(c) @ w_ada[l] + b_ada[l])[:, None, :], N_MOD, axis=-1)
        mod_c = jnp.split(jax.nn.silu(c_ctx) @ w_ada[l] + b_ada[l], N_MOD, axis=-1)
        xl = macaron_ffn(xl, norm_ff1[l], mod_l[0:3], ff1_gate[l], ff1_up[l], ff1_down[l])
        xc = macaron_ffn(xc, norm_ff1[l], mod_c[0:3], ff1_gate[l], ff1_up[l], ff1_down[l])
        hl = modulate(rms_norm(xl, norm_mix[l]), mod_l[3], mod_l[4])
        hc = modulate(rms_norm(xc, norm_mix[l]), mod_c[3], mod_c[4])
        yl, yc = token_mixer(hl, hc, w_in[l], na_rpb[l], gqa_q_norm[l], gqa_k_norm[l],
                             ml_conv_w[l], ml_conv_b[l], ml_gate_b[l], ml_out_norm[l], not last)
        xl = xl + mod_l[5] * (yl @ w_out[l])
        xl = macaron_ffn(xl, norm_ff2[l], mod_l[6:9], ff2_gate[l], ff2_up[l], ff2_down[l])
        if not last:
            xc = xc + mod_c[5] * (yc @ w_out[l])
            xc = macaron_ffn(xc, norm_ff2[l], mod_c[6:9], ff2_gate[l], ff2_up[l], ff2_down[l])
    return rms_norm(xl, final_norm)
```

```python
import functools

import numpy as np
import jax
import jax.numpy as jnp
from jax import lax
from jax.experimental import pallas as pl
from jax.experimental.pallas import tpu as pltpu

F32 = jnp.float32
BF16 = jnp.bfloat16

GRID_W = 64
HEAD_DIM = 128
NA_HEADS = 4
GQA_Q_HEADS = 4
GQA_KV_HEADS = 2
ML_HEADS = 8
NA_W = NA_HEADS * HEAD_DIM
GQA_W = GQA_Q_HEADS * HEAD_DIM
KV_W = GQA_KV_HEADS * HEAD_DIM
ML_W = ML_HEADS * HEAD_DIM
NA_WIN_R = 8
NA_WIN_C = 16
ML_CHUNK = 128
ROPE_THETA = 10000.0
N_MOD = 9
EPS = 1e-6
N_GATES = 4 * ML_HEADS
COL_NA_Q = 0
COL_NA_K = NA_W
COL_NA_V = 2 * NA_W
COL_GQ_Q = 3 * NA_W
COL_GQ_K = COL_GQ_Q + GQA_W
COL_GQ_V = COL_GQ_K + KV_W
COL_ML_Q = COL_GQ_V + KV_W
COL_ML_K = COL_ML_Q + ML_W
COL_ML_V = COL_ML_K + ML_W
COL_ML_O = COL_ML_V + ML_W
COL_GATES = COL_ML_O + ML_W

NEG_BIG = -1e30
VMEM_LIMIT_V7X = 56 * 1024 * 1024

NA_QROWS = 8
NA_KROWS = 16


def _cparams(sem):
    return pltpu.CompilerParams(dimension_semantics=sem, vmem_limit_bytes=VMEM_LIMIT_V7X)


def _rms(x, gain):
    return x * lax.rsqrt(jnp.mean(x * x, axis=-1, keepdims=True) + EPS) * gain


def _dot(a, b):
    return jnp.dot(a, b, preferred_element_type=F32)


def _dot_nt(a, b):
    return lax.dot_general(a, b, (((1,), (1,)), ((), ())), preferred_element_type=F32)


def _dot_tn(a, b):
    return lax.dot_general(a, b, (((0,), (0,)), ((), ())), preferred_element_type=F32)


def _ada_kernel(c_ref, w_ref, b_ref, o_ref):
    c = c_ref[...]
    a = (c * jax.nn.sigmoid(c)).astype(BF16)
    o_ref[...] = _dot(a, w_ref[...].astype(BF16)) + b_ref[...]


def _ada_call(cond, w, b, tn=1024):
    m, d = cond.shape
    n = w.shape[1]
    return pl.pallas_call(
        _ada_kernel,
        grid=(n // tn,),
        in_specs=[pl.BlockSpec((m, d), lambda j: (0, 0)),
                  pl.BlockSpec((d, tn), lambda j: (0, j)),
                  pl.BlockSpec((1, tn), lambda j: (0, j))],
        out_specs=pl.BlockSpec((m, tn), lambda j: (0, j)),
        out_shape=jax.ShapeDtypeStruct((m, n), F32),
        compiler_params=_cparams(("arbitrary",)),
        name="ada_mod",
    )(cond, w, b)


def _mod_spec(k, rows_per_group, fixed_group):
    def index_map(i, *_):
        grp = fixed_group if fixed_group is not None else i // rows_per_group
        return (grp * N_MOD + k, 0, 0)
    return index_map


def _ffn_kernel(x_ref, sh_ref, sc_ref, gt_ref, gain_ref, wg_ref, wu_ref, wd_ref, fin_ref, o_ref,
                h_scr, acc_scr, *, n_ff, final_norm):
    j = pl.program_id(1)

    @pl.when(j == 0)
    def _():
        h = _rms(x_ref[...], gain_ref[...]) * (1.0 + sc_ref[...]) + sh_ref[...]
        h_scr[...] = h.astype(BF16)
        acc_scr[...] = jnp.zeros_like(acc_scr)

    h = h_scr[...]
    g = _dot(h, wg_ref[...])
    u = _dot(h, wu_ref[...])
    a = (g * jax.nn.sigmoid(g)) * u
    acc_scr[...] += _dot(a.astype(BF16), wd_ref[...])

    @pl.when(j == n_ff - 1)
    def _():
        out = x_ref[...] + 0.5 * gt_ref[...] * acc_scr[...]
        if final_norm:
            out = _rms(out, fin_ref[...])
        o_ref[...] = out


def _ffn_call(x, mod, k0, gain, wg, wu, wd, fin, *, tiles_per_group, fixed_group, final_norm,
              tm=512, tf=512):
    t, d = x.shape
    dff = wg.shape[1]
    n_ff = dff // tf
    mspec = lambda k: pl.BlockSpec((None, 1, d), _mod_spec(k, tiles_per_group, fixed_group))
    return pl.pallas_call(
        functools.partial(_ffn_kernel, n_ff=n_ff, final_norm=final_norm),
        grid=(t // tm, n_ff),
        in_specs=[pl.BlockSpec((tm, d), lambda i, j: (i, 0)),
                  mspec(k0), mspec(k0 + 1), mspec(k0 + 2),
                  pl.BlockSpec((1, d), lambda i, j: (0, 0)),
                  pl.BlockSpec((d, tf), lambda i, j: (0, j)),
                  pl.BlockSpec((d, tf), lambda i, j: (0, j)),
                  pl.BlockSpec((tf, d), lambda i, j: (j, 0)),
                  pl.BlockSpec((1, d), lambda i, j: (0, 0))],
        out_specs=pl.BlockSpec((tm, d), lambda i, j: (i, 0)),
        out_shape=jax.ShapeDtypeStruct((t, d), F32),
        scratch_shapes=[pltpu.VMEM((tm, d), BF16), pltpu.VMEM((tm, d), F32)],
        compiler_params=_cparams(("parallel", "arbitrary")),
        name="macaron_ffn",
    )(x, mod, mod, mod, gain, wg, wu, wd, fin)


def _inproj_kernel(x_ref, sh_ref, sc_ref, gain_ref, w_ref, wgt_ref, p_ref, gt_ref, h_scr):
    j = pl.program_id(1)

    @pl.when(j == 0)
    def _():
        h = _rms(x_ref[...], gain_ref[...]) * (1.0 + sc_ref[...]) + sh_ref[...]
        hb = h.astype(BF16)
        h_scr[...] = hb
        gt_ref[...] = _dot_nt(wgt_ref[...], hb)

    p_ref[...] = _dot(h_scr[...], w_ref[...])


def _inproj_call(x, mod, gain, w_main, w_gates_t, *, tiles_per_group, fixed_group, tm=512, tn=512):
    t, d = x.shape
    n = w_main.shape[1]
    mspec = lambda k: pl.BlockSpec((None, 1, d), _mod_spec(k, tiles_per_group, fixed_group))
    return pl.pallas_call(
        _inproj_kernel,
        grid=(t // tm, n // tn),
        in_specs=[pl.BlockSpec((tm, d), lambda i, j: (i, 0)),
                  mspec(3), mspec(4),
                  pl.BlockSpec((1, d), lambda i, j: (0, 0)),
                  pl.BlockSpec((d, tn), lambda i, j: (0, j)),
                  pl.BlockSpec((N_GATES, d), lambda i, j: (0, 0))],
        out_specs=[pl.BlockSpec((tm, tn), lambda i, j: (i, j)),
                   pl.BlockSpec((N_GATES, tm), lambda i, j: (0, i))],
        out_shape=[jax.ShapeDtypeStruct((t, n), F32),
                   jax.ShapeDtypeStruct((N_GATES, t), F32)],
        scratch_shapes=[pltpu.VMEM((tm, d), BF16)],
        compiler_params=_cparams(("parallel", "arbitrary")),
        name="mixer_in_proj",
    )(x, mod, mod, gain, w_main, w_gates_t)


def _outproj_kernel(x_ref, gt_ref, ya_ref, yb_ref, yc_ref, w_ref, o_ref):
    acc = _dot(ya_ref[...], w_ref[0:NA_W, :])
    acc += _dot(yb_ref[...], w_ref[NA_W:NA_W + GQA_W, :])
    acc += _dot(yc_ref[...], w_ref[NA_W + GQA_W:, :])
    o_ref[...] = x_ref[...] + gt_ref[...] * acc


def _outproj_call(x, mod, ya, yb, yc, w, *, tiles_per_group, fixed_group, tm=512):
    t, d = x.shape
    return pl.pallas_call(
        _outproj_kernel,
        grid=(t // tm,),
        in_specs=[pl.BlockSpec((tm, d), lambda i: (i, 0)),
                  pl.BlockSpec((None, 1, d), _mod_spec(5, tiles_per_group, fixed_group)),
                  pl.BlockSpec((tm, NA_W), lambda i: (i, 0)),
                  pl.BlockSpec((tm, GQA_W), lambda i: (i, 0)),
                  pl.BlockSpec((tm, ML_W), lambda i: (i, 0)),
                  pl.BlockSpec((d, d), lambda i: (0, 0))],
        out_specs=pl.BlockSpec((tm, d), lambda i: (i, 0)),
        out_shape=jax.ShapeDtypeStruct((t, d), F32),
        compiler_params=_cparams(("parallel",)),
        name="mixer_out_proj",
    )(x, mod, ya, yb, yc, w)


def _na_kernel(q_ref, k_ref, v_ref, kc_ref, vc_ref, mb_ref, o_ref, *, rows):
    rb = pl.program_id(2)
    key_row0 = jnp.clip(rb * NA_QROWS - NA_WIN_R // 2, 0, rows - NA_KROWS)
    start = pl.multiple_of(key_row0 * GRID_W, GRID_W)
    nk = NA_KROWS * GRID_W
    scale = HEAD_DIM ** -0.5
    q = q_ref[...].astype(BF16)
    k = k_ref[pl.ds(start, nk), :].astype(BF16)
    v = v_ref[pl.ds(start, nk), :].astype(BF16)
    s_loc = _dot_nt(q, k) * scale + mb_ref[...]
    s_ctx = _dot_nt(q, kc_ref[...].astype(BF16)) * scale
    m = jnp.maximum(jnp.max(s_loc, axis=-1, keepdims=True), jnp.max(s_ctx, axis=-1, keepdims=True))
    p_loc = jnp.exp(s_loc - m)
    p_ctx = jnp.exp(s_ctx - m)
    den = jnp.sum(p_loc, axis=-1, keepdims=True) + jnp.sum(p_ctx, axis=-1, keepdims=True)
    out = _dot(p_loc.astype(BF16), v) + _dot(p_ctx.astype(BF16), vc_ref[...].astype(BF16))
    o_ref[...] = (out / den).astype(o_ref.dtype)


def _na_mask_bias(rpb, rows):
    nblk = rows // NA_QROWS
    tabs = []
    for blk in (0, 1, nblk - 1):
        r0 = blk * NA_QROWS
        ks = int(np.clip(r0 - NA_WIN_R // 2, 0, rows - NA_KROWS))
        r = r0 + np.arange(NA_QROWS)
        kr = ks + np.arange(NA_KROWS)
        row_start = np.clip(r - NA_WIN_R // 2, 0, rows - NA_WIN_R)
        row_ok = (kr[None, :] >= row_start[:, None]) & (kr[None, :] < row_start[:, None] + NA_WIN_R)
        cq = np.arange(GRID_W)
        col_start = np.clip(cq - NA_WIN_C // 2, 0, GRID_W - NA_WIN_C)
        col_ok = (cq[None, :] >= col_start[:, None]) & (cq[None, :] < col_start[:, None] + NA_WIN_C)
        dr = np.clip(kr[None, :] - r[:, None] + (NA_WIN_R - 1), 0, 2 * NA_WIN_R - 2)
        dc = np.clip(cq[None, :] - cq[:, None], -(NA_WIN_C - 1), NA_WIN_C - 1) + (NA_WIN_C - 1)
        ok = row_ok[:, None, :, None] & col_ok[None, :, None, :]
        dr_i = np.broadcast_to(dr[:, None, :, None], ok.shape)
        dc_i = np.broadcast_to(dc[None, :, None, :], ok.shape)
        n_q, n_k = NA_QROWS * GRID_W, NA_KROWS * GRID_W
        bias = rpb[:, dr_i.reshape(n_q, n_k), dc_i.reshape(n_q, n_k)]
        tabs.append(jnp.where(ok.reshape(n_q, n_k)[None], bias, NEG_BIG))
    return jnp.stack(tabs, axis=1).astype(F32)


def _na_call(p_lat, p_ctx, mask_bias, batch, seq, ctx_len):
    rows = seq // GRID_W
    nblk = rows // NA_QROWS
    tq = NA_QROWS * GRID_W
    nk = NA_KROWS * GRID_W
    cq, ck, cv = COL_NA_Q // HEAD_DIM, COL_NA_K // HEAD_DIM, COL_NA_V // HEAD_DIM

    def mb_map(b, h, r):
        return (h, jnp.where(r == 0, 0, jnp.where(r == nblk - 1, 2, 1)), 0, 0)

    return pl.pallas_call(
        functools.partial(_na_kernel, rows=rows),
        grid=(batch, NA_HEADS, nblk),
        in_specs=[pl.BlockSpec((tq, HEAD_DIM), lambda b, h, r: (b * nblk + r, cq + h)),
                  pl.BlockSpec((seq, HEAD_DIM), lambda b, h, r: (b, ck + h)),
                  pl.BlockSpec((seq, HEAD_DIM), lambda b, h, r: (b, cv + h)),
                  pl.BlockSpec((ctx_len, HEAD_DIM), lambda b, h, r: (b, ck + h)),
                  pl.BlockSpec((ctx_len, HEAD_DIM), lambda b, h, r: (b, cv + h)),
                  pl.BlockSpec((None, None, tq, nk), mb_map)],
        out_specs=pl.BlockSpec((tq, HEAD_DIM), lambda b, h, r: (b * nblk + r, h)),
        out_shape=jax.ShapeDtypeStruct((batch * seq, NA_W), BF16),
        compiler_params=_cparams(("parallel", "parallel", "arbitrary")),
        name="neighbourhood_attn",
    )(p_lat, p_lat, p_lat, p_ctx, p_ctx, mask_bias)


def _ctx_attn_kernel(q_ref, k_ref, v_ref, o_ref):
    scale = HEAD_DIM ** -0.5
    s = _dot_nt(q_ref[...].astype(BF16), k_ref[...].astype(BF16)) * scale
    m = jnp.max(s, axis=-1, keepdims=True)
    p = jnp.exp(s - m)
    den = jnp.sum(p, axis=-1, keepdims=True)
    o_ref[...] = (_dot(p.astype(BF16), v_ref[...].astype(BF16)) / den).astype(o_ref.dtype)


def _ctx_attn_call(q_arr, k_arr, v_arr, batch, ctx_len, n_heads, q_col, k_col, v_col, kv_group):
    return pl.pallas_call(
        _ctx_attn_kernel,
        grid=(batch, n_heads),
        in_specs=[pl.BlockSpec((ctx_len, HEAD_DIM), lambda b, h: (b, q_col + h)),
                  pl.BlockSpec((ctx_len, HEAD_DIM), lambda b, h: (b, k_col + h // kv_group)),
                  pl.BlockSpec((ctx_len, HEAD_DIM), lambda b, h: (b, v_col + h // kv_group))],
        out_specs=pl.BlockSpec((ctx_len, HEAD_DIM), lambda b, h: (b, h)),
        out_shape=jax.ShapeDtypeStruct((batch * ctx_len, n_heads * HEAD_DIM), BF16),
        compiler_params=_cparams(("parallel", "parallel")),
        name="ctx_attn",
    )(q_arr, k_arr, v_arr)


def _rope(x, cos, sin_lo, sin_hi):
    return (x * cos + pltpu.roll(x, HEAD_DIM - HEAD_DIM // 4, 1) * sin_lo
            + pltpu.roll(x, HEAD_DIM // 4, 1) * sin_hi)


def _gqa_prep_kernel(q_ref, k_ref, v_ref, qg_ref, kg_ref, cos_ref, slo_ref, shi_ref,
                     qo_ref, ko_ref, vo_ref, *, rope):
    def prep(x, gain):
        y = _rms(x, gain)
        if rope:
            y = _rope(y, cos_ref[...], slo_ref[...], shi_ref[...])
        return y.astype(BF16)

    for h in range(GQA_Q_HEADS):
        sl = slice(h * HEAD_DIM, (h + 1) * HEAD_DIM)
        qo_ref[:, sl] = prep(q_ref[:, sl], qg_ref[...])
    for h in range(GQA_KV_HEADS):
        sl = slice(h * HEAD_DIM, (h + 1) * HEAD_DIM)
        ko_ref[:, sl] = prep(k_ref[:, sl], kg_ref[...])
    vo_ref[...] = v_ref[...].astype(BF16)


def _rope_tables(n_tokens):
    t = jnp.arange(n_tokens)
    row = (t // GRID_W).astype(F32)
    col = (t % GRID_W).astype(F32)
    half = HEAD_DIM // 2
    inv_freq = 1.0 / (ROPE_THETA ** (jnp.arange(0, half, 2, dtype=F32) / half))
    ang_r = row[:, None] * inv_freq[None, :]
    ang_c = col[:, None] * inv_freq[None, :]
    ang = jnp.concatenate([ang_r, ang_r, ang_c, ang_c], axis=-1)
    cos, sin = jnp.cos(ang), jnp.sin(ang)
    lo = (jnp.arange(HEAD_DIM) % half) < (half // 2)
    return cos, jnp.where(lo, -sin, 0.0), jnp.where(lo, 0.0, sin)


def _gqa_prep_call(p, q_gain, k_gain, tables, seq, *, rope, tm=512):
    t = p.shape[0]
    per_seq = seq // tm if rope else 1
    tab_spec = pl.BlockSpec((tm, HEAD_DIM), lambda i: (i % per_seq, 0))
    return pl.pallas_call(
        functools.partial(_gqa_prep_kernel, rope=rope),
        grid=(t // tm,),
        in_specs=[pl.BlockSpec((tm, GQA_W), lambda i: (i, COL_GQ_Q // GQA_W)),
                  pl.BlockSpec((tm, KV_W), lambda i: (i, COL_GQ_K // KV_W)),
                  pl.BlockSpec((tm, KV_W), lambda i: (i, COL_GQ_V // KV_W)),
                  pl.BlockSpec((1, HEAD_DIM), lambda i: (0, 0)),
                  pl.BlockSpec((1, HEAD_DIM), lambda i: (0, 0)),
                  tab_spec, tab_spec, tab_spec],
        out_specs=[pl.BlockSpec((tm, GQA_W), lambda i: (i, 0)),
                   pl.BlockSpec((tm, KV_W), lambda i: (i, 0)),
                   pl.BlockSpec((tm, KV_W), lambda i: (i, 0))],
        out_shape=[jax.ShapeDtypeStruct((t, GQA_W), BF16),
                   jax.ShapeDtypeStruct((t, KV_W), BF16),
                   jax.ShapeDtypeStruct((t, KV_W), BF16)],
        compiler_params=_cparams(("parallel",)),
        name="gqa_prep",
    )(p, p, p, q_gain, k_gain, *tables)


def _gqa_kernel(q_ref, kl_ref, vl_ref, kc_ref, vc_ref, o_ref):
    scale = HEAD_DIM ** -0.5
    q = q_ref[...]
    s_lat = _dot_nt(q, kl_ref[...]) * scale
    s_ctx = _dot_nt(q, kc_ref[...]) * scale
    m = jnp.maximum(jnp.max(s_lat, axis=-1, keepdims=True), jnp.max(s_ctx, axis=-1, keepdims=True))
    p_lat = jnp.exp(s_lat - m)
    p_ctx = jnp.exp(s_ctx - m)
    den = jnp.sum(p_lat, axis=-1, keepdims=True) + jnp.sum(p_ctx, axis=-1, keepdims=True)
    out = _dot(p_lat.astype(BF16), vl_ref[...]) + _dot(p_ctx.astype(BF16), vc_ref[...])
    o_ref[...] = (out / den).astype(o_ref.dtype)


def _gqa_call(q_lat, k_lat, v_lat, k_ctx, v_ctx, batch, seq, ctx_len, tq=256):
    grp = GQA_Q_HEADS // GQA_KV_HEADS
    nq = seq // tq
    return pl.pallas_call(
        _gqa_kernel,
        grid=(batch, GQA_Q_HEADS, nq),
        in_specs=[pl.BlockSpec((tq, HEAD_DIM), lambda b, h, i: (b * nq + i, h)),
                  pl.BlockSpec((seq, HEAD_DIM), lambda b, h, i: (b, h // grp)),
                  pl.BlockSpec((seq, HEAD_DIM), lambda b, h, i: (b, h // grp)),
                  pl.BlockSpec((ctx_len, HEAD_DIM), lambda b, h, i: (b, h // grp)),
                  pl.BlockSpec((ctx_len, HEAD_DIM), lambda b, h, i: (b, h // grp))],
        out_specs=pl.BlockSpec((tq, HEAD_DIM), lambda b, h, i: (b * nq + i, h)),
        out_shape=jax.ShapeDtypeStruct((batch * seq, GQA_W), BF16),
        compiler_params=_cparams(("parallel", "parallel", "arbitrary")),
        name="gqa_attn",
    )(q_lat, k_lat, v_lat, k_ctx, v_ctx)


ST_R, ST_W, ST_B, ST_BTOT, ST_AMAX = range(5)
N_STATS = 5


def _mlstm_kernel(ql_ref, kl_ref, vl_ref, ol_ref, qc_ref, kc_ref, vc_ref, oc_ref, g_ref, gb_ref,
                  cwq_ref, cwk_ref, cbq_ref, cbk_ref, on_ref, yl_ref, *rest,
                  n_lat, n_ctx, need_ctx):
    if need_ctx:
        yc_ref = rest[0]
        rest = rest[1:]
    qa_l, ka_l, qa_c, ka_c, h_l, h_c, st = rest
    L = ML_CHUNK
    h = pl.program_id(1)
    row = lax.broadcasted_iota(jnp.int32, (L, L), 0)
    col = lax.broadcasted_iota(jnp.int32, (L, L), 1)
    eye = row == col

    tri_fw = (row <= col).astype(F32)
    tri_bw = (row >= col).astype(F32)
    for d, tri in enumerate((tri_fw, tri_bw)):
        log_i = g_ref[2 * d * ML_HEADS + h] + gb_ref[2 * d * ML_HEADS + h]
        log_f = jax.nn.log_sigmoid(g_ref[(2 * d + 1) * ML_HEADS + h] + gb_ref[(2 * d + 1) * ML_HEADS + h])
        b = jnp.dot(log_f, tri, preferred_element_type=F32, precision=lax.Precision.HIGHEST)
        b_tot = b[:, L - 1:L] if d == 0 else b[:, 0:1]
        a = b_tot - b + log_i
        a_max = jnp.max(a, axis=-1, keepdims=True)
        base = d * N_STATS
        st[base + ST_R] = log_i - b
        st[base + ST_W] = jnp.exp(a - a_max)
        st[base + ST_B] = b
        st[base + ST_BTOT] = jnp.broadcast_to(b_tot, b.shape)
        st[base + ST_AMAX] = jnp.broadcast_to(a_max, b.shape)

    def conv_stream(src_ref, w_ref, b_ref, dst_ref, n_chunks, post_scale):
        n_tok = n_chunks * L
        sub = lax.broadcasted_iota(jnp.int32, (L, HEAD_DIM), 0)
        w0, w1, w2 = w_ref[0:1, :], w_ref[1:2, :], w_ref[2:3, :]

        def body(n, carry):
            t0 = pl.multiple_of(n * L, L)
            x = src_ref[pl.ds(t0, L), :]
            prev_row = src_ref[pl.ds(jnp.maximum(t0 - 1, 0), 1), :] * (n > 0).astype(F32)
            next_row = src_ref[pl.ds(jnp.minimum(t0 + L, n_tok - 1), 1), :] * (n < n_chunks - 1).astype(F32)
            x_prev = jnp.where(sub == 0, prev_row, pltpu.roll(x, 1, 0))
            x_next = jnp.where(sub == L - 1, next_row, pltpu.roll(x, L - 1, 0))
            y = w0 * x_prev + w1 * x + w2 * x_next + b_ref[...]
            y = y * jax.nn.sigmoid(y)
            if post_scale is not None:
                y = y * post_scale
            dst_ref[pl.ds(t0, L), :] = y.astype(dst_ref.dtype)
            return carry

        lax.fori_loop(0, n_chunks, body, 0)

    k_scale = HEAD_DIM ** -0.5
    conv_stream(ql_ref, cwq_ref, cbq_ref, qa_l, n_lat, None)
    conv_stream(kl_ref, cwk_ref, cbk_ref, ka_l, n_lat, k_scale)
    conv_stream(qc_ref, cwq_ref, cbq_ref, qa_c, n_ctx, None)
    conv_stream(kc_ref, cwk_ref, cbk_ref, ka_c, n_ctx, k_scale)

    def chunk(d, n_stat, t0, q_src, k_src, v_src, carry, want_out):
        c_prev, n_prev, m_prev = carry
        base = d * N_STATS
        stat = lambda k: st[base + k, pl.ds(n_stat, 1), :]
        kf = k_src[pl.ds(t0, L), :]
        v = v_src[pl.ds(t0, L), :].astype(BF16)
        out = None
        if want_out:
            q = q_src[pl.ds(t0, L), :]
            seen = (col <= row) if d == 0 else (col >= row)
            r_b = jnp.broadcast_to(stat(ST_R), (L, L))
            m1 = m_prev[:, 0:1]
            big_r = jnp.maximum(jnp.max(jnp.where(seen, r_b, NEG_BIG), axis=-1, keepdims=True), m1)
            dec_mat = jnp.exp(jnp.where(seen, r_b - big_r, NEG_BIG))
            s = _dot_nt(q, kf.astype(BF16)) * dec_mat
            g = jnp.exp(m1 - big_r)
            num = g * _dot(q, c_prev.astype(BF16)) + _dot(s.astype(BF16), v)
            den = (g * jnp.sum(q.astype(F32) * n_prev, axis=-1, keepdims=True)
                   + jnp.sum(s, axis=-1, keepdims=True))
            b_col = jnp.sum(jnp.where(eye, jnp.broadcast_to(stat(ST_B), (L, L)), 0.0), axis=-1, keepdims=True)
            out = num / jnp.maximum(jnp.abs(den), jnp.exp(-(b_col + big_r)))
        w_col = jnp.sum(jnp.where(eye, jnp.broadcast_to(stat(ST_W), (L, L)), 0.0), axis=-1, keepdims=True)
        kw = kf * w_col
        c_loc = _dot_tn(kw.astype(BF16), v)
        n_loc = jnp.sum(kw, axis=0, keepdims=True)
        b_tot, a_max = stat(ST_BTOT), stat(ST_AMAX)
        m_new = jnp.maximum(b_tot + m_prev, a_max)
        dec = jnp.exp(b_tot + m_prev - m_new)
        inp = jnp.exp(a_max - m_new)
        return out, (dec * c_prev + inp * c_loc, dec * n_prev + inp * n_loc, m_new)

    def finish(hsum, o_src, t0):
        y = _rms(hsum, on_ref[...]) * jax.nn.sigmoid(o_src[pl.ds(t0, L), :])
        return y.astype(BF16)

    zero_state = (jnp.zeros((L, L), F32), jnp.zeros((1, L), F32), jnp.zeros((1, L), F32))

    def fw_ctx(n, carry):
        t0 = pl.multiple_of(n * L, L)
        out, carry = chunk(0, n_lat + n, t0, qa_c, ka_c, vc_ref, carry, need_ctx)
        if need_ctx:
            h_c[pl.ds(t0, L), :] = out
        return carry

    def fw_lat(n, carry):
        t0 = pl.multiple_of(n * L, L)
        out, carry = chunk(0, n, t0, qa_l, ka_l, vl_ref, carry, True)
        h_l[pl.ds(t0, L), :] = out
        return carry

    carry = lax.fori_loop(0, n_ctx, fw_ctx, zero_state)
    lax.fori_loop(0, n_lat, fw_lat, carry)

    def bw_ctx(i, carry):
        n = n_ctx - 1 - i
        t0 = pl.multiple_of(n * L, L)
        out, carry = chunk(1, n_lat + n, t0, qa_c, ka_c, vc_ref, carry, need_ctx)
        if need_ctx:
            yc_ref[pl.ds(t0, L), :] = finish(h_c[pl.ds(t0, L), :] + out, oc_ref, t0)
        return carry

    def bw_lat(i, carry):
        n = n_lat - 1 - i
        t0 = pl.multiple_of(n * L, L)
        out, carry = chunk(1, n, t0, qa_l, ka_l, vl_ref, carry, True)
        yl_ref[pl.ds(t0, L), :] = finish(h_l[pl.ds(t0, L), :] + out, ol_ref, t0)
        return carry

    carry = lax.fori_loop(0, n_ctx, bw_ctx, zero_state)
    lax.fori_loop(0, n_lat, bw_lat, carry)


def _mlstm_call(p_lat, p_ctx, gates, gate_b, conv_w, conv_b, out_norm, batch, seq, ctx_len, need_ctx):
    n_lat, n_ctx = seq // ML_CHUNK, ctx_len // ML_CHUNK
    n_pad = gates.shape[2]
    cq, ck = COL_ML_Q // HEAD_DIM, COL_ML_K // HEAD_DIM
    cv, co = COL_ML_V // HEAD_DIM, COL_ML_O // HEAD_DIM
    col = lambda rows, c0: pl.BlockSpec((rows, HEAD_DIM), lambda b, h: (b, c0 + h))
    out_specs = [pl.BlockSpec((seq, HEAD_DIM), lambda b, h: (b, h))]
    out_shape = [jax.ShapeDtypeStruct((batch * seq, ML_W), BF16)]
    if need_ctx:
        out_specs.append(pl.BlockSpec((ctx_len, HEAD_DIM), lambda b, h: (b, h)))
        out_shape.append(jax.ShapeDtypeStruct((batch * ctx_len, ML_W), BF16))
    res = pl.pallas_call(
        functools.partial(_mlstm_kernel, n_lat=n_lat, n_ctx=n_ctx, need_ctx=need_ctx),
        grid=(batch, ML_HEADS),
        in_specs=[col(seq, cq), col(seq, ck), col(seq, cv), col(seq, co),
                  col(ctx_len, cq), col(ctx_len, ck), col(ctx_len, cv), col(ctx_len, co),
                  pl.BlockSpec((N_GATES, None, n_pad, ML_CHUNK), lambda b, h: (0, b, 0, 0)),
                  pl.BlockSpec((N_GATES, 1, ML_CHUNK), lambda b, h: (0, 0, 0)),
                  pl.BlockSpec((3, HEAD_DIM), lambda b, h: (0, h)),
                  pl.BlockSpec((3, HEAD_DIM), lambda b, h: (0, ML_HEADS + h)),
                  pl.BlockSpec((1, HEAD_DIM), lambda b, h: (0, h)),
                  pl.BlockSpec((1, HEAD_DIM), lambda b, h: (0, ML_HEADS + h)),
                  pl.BlockSpec((1, HEAD_DIM), lambda b, h: (0, h))],
        out_specs=out_specs,
        out_shape=out_shape,
        scratch_shapes=[pltpu.VMEM((seq, HEAD_DIM), BF16), pltpu.VMEM((seq, HEAD_DIM), F32),
                        pltpu.VMEM((ctx_len, HEAD_DIM), BF16), pltpu.VMEM((ctx_len, HEAD_DIM), F32),
                        pltpu.VMEM((seq, HEAD_DIM), F32), pltpu.VMEM((ctx_len, HEAD_DIM), F32),
                        pltpu.VMEM((2 * N_STATS, n_pad, ML_CHUNK), F32)],
        compiler_params=_cparams(("parallel", "parallel")),
        name="mlstm",
    )(p_lat, p_lat, p_lat, p_lat, p_ctx, p_ctx, p_ctx, p_ctx, gates, gate_b,
      conv_w, conv_w, conv_b, conv_b, out_norm)
    return (res[0], res[1]) if need_ctx else (res[0], None)


def kernel(x, c, ctx, c_ctx, w_ada, b_ada, norm_ff1, ff1_gate, ff1_up, ff1_down, norm_mix, w_in, na_rpb,
           gqa_q_norm, gqa_k_norm, ml_conv_w, ml_conv_b, ml_gate_b, ml_out_norm, w_out, norm_ff2,
           ff2_gate, ff2_up, ff2_down, final_norm):
    batch, seq, d = x.shape
    ctx_len = ctx.shape[1]
    depth = w_ada.shape[0]
    tm = 512
    lat_tiles = seq // tm
    ctx_group = batch
    n_lat_chunks, n_ctx_chunks = seq // ML_CHUNK, ctx_len // ML_CHUNK
    n_chunk_pad = -(-(n_lat_chunks + n_ctx_chunks) // 8) * 8

    xl = x.reshape(batch * seq, d)
    xc = ctx.reshape(batch * ctx_len, d)
    cond = jnp.zeros((8, d), F32).at[:batch].set(c).at[batch].set(c_ctx)
    rope_tabs = _rope_tables(seq)
    row2 = lambda v: v.reshape(1, -1)
    fin = row2(final_norm)

    for l in range(depth):
        last = l == depth - 1
        mod = _ada_call(cond, w_ada[l], row2(b_ada[l])).reshape(8 * N_MOD, 1, d)
        lat = dict(tiles_per_group=lat_tiles, fixed_group=None)
        cx = dict(tiles_per_group=None, fixed_group=ctx_group)

        wg, wu, wd = ff1_gate[l].astype(BF16), ff1_up[l].astype(BF16), ff1_down[l].astype(BF16)
        g1 = row2(norm_ff1[l])
        xl = _ffn_call(xl, mod, 0, g1, wg, wu, wd, fin, final_norm=False, **lat)
        xc = _ffn_call(xc, mod, 0, g1, wg, wu, wd, fin, final_norm=False, **cx)

        w_main = w_in[l][:, :COL_GATES].astype(BF16)
        w_gates_t = w_in[l][:, COL_GATES:].T.astype(BF16)
        gm = row2(norm_mix[l])
        p_lat, gt_lat = _inproj_call(xl, mod, gm, w_main, w_gates_t, **lat)
        p_ctx, gt_ctx = _inproj_call(xc, mod, gm, w_main, w_gates_t, **cx)

        ya_lat = _na_call(p_lat, p_ctx, _na_mask_bias(na_rpb[l], seq // GRID_W), batch, seq, ctx_len)
        qg, kg = row2(gqa_q_norm[l]), row2(gqa_k_norm[l])
        qn_lat, kn_lat, vb_lat = _gqa_prep_call(p_lat, qg, kg, rope_tabs, seq, rope=True)
        qn_ctx, kn_ctx, vb_ctx = _gqa_prep_call(p_ctx, qg, kg, rope_tabs, seq, rope=False, tm=ctx_len)
        yb_lat = _gqa_call(qn_lat, kn_lat, vb_lat, kn_ctx, vb_ctx, batch, seq, ctx_len)
        gates = jnp.concatenate(
            [gt_lat.reshape(N_GATES, batch, n_lat_chunks, ML_CHUNK),
             gt_ctx.reshape(N_GATES, batch, n_ctx_chunks, ML_CHUNK),
             jnp.zeros((N_GATES, batch, n_chunk_pad - n_lat_chunks - n_ctx_chunks, ML_CHUNK), F32)], axis=2)
        gate_b = jnp.broadcast_to(ml_gate_b[l][:, None, None], (N_GATES, 1, ML_CHUNK))
        yc_lat, yc_ctx = _mlstm_call(p_lat, p_ctx, gates, gate_b, ml_conv_w[l], row2(ml_conv_b[l]),
                                     row2(ml_out_norm[l]), batch, seq, ctx_len, not last)

        wo = w_out[l].astype(BF16)
        xl = _outproj_call(xl, mod, ya_lat, yb_lat, yc_lat, wo, **lat)
        wg, wu, wd = ff2_gate[l].astype(BF16), ff2_up[l].astype(BF16), ff2_down[l].astype(BF16)
        g2 = row2(norm_ff2[l])
        xl = _ffn_call(xl, mod, 6, g2, wg, wu, wd, fin, final_norm=last, **lat)
        if not last:
            ya_ctx = _ctx_attn_call(p_ctx, p_ctx, p_ctx, batch, ctx_len, NA_HEADS,
                                    COL_NA_Q // HEAD_DIM, COL_NA_K // HEAD_DIM, COL_NA_V // HEAD_DIM, 1)
            yb_ctx = _ctx_attn_call(qn_ctx, kn_ctx, vb_ctx, batch, ctx_len, GQA_Q_HEADS, 0, 0, 0,
                                    GQA_Q_HEADS // GQA_KV_HEADS)
            xc = _outproj_call(xc, mod, ya_ctx, yb_ctx, yc_ctx, wo, **cx)
            xc = _ffn_call(xc, mod, 6, g2, wg, wu, wd, fin, final_norm=False, **cx)
    return xl.reshape(batch, seq, d)
```

```python
import functools

import numpy as np
import jax
import jax.numpy as jnp
from jax import lax
from jax.experimental import pallas as pl
from jax.experimental.pallas import tpu as pltpu

F32 = jnp.float32
BF16 = jnp.bfloat16

GRID_W = 64
HEAD_DIM = 128
NA_HEADS = 4
GQA_Q_HEADS = 4
GQA_KV_HEADS = 2
ML_HEADS = 8
NA_W = NA_HEADS * HEAD_DIM
GQA_W = GQA_Q_HEADS * HEAD_DIM
KV_W = GQA_KV_HEADS * HEAD_DIM
ML_W = ML_HEADS * HEAD_DIM
NA_WIN_R = 8
NA_WIN_C = 16
ML_CHUNK = 128
ROPE_THETA = 10000.0
N_MOD = 9
EPS = 1e-6
N_GATES = 4 * ML_HEADS
COL_NA_Q = 0
COL_NA_K = NA_W
COL_NA_V = 2 * NA_W
COL_GQ_Q = 3 * NA_W
COL_GQ_K = COL_GQ_Q + GQA_W
COL_GQ_V = COL_GQ_K + KV_W
COL_ML_Q = COL_GQ_V + KV_W
COL_ML_K = COL_ML_Q + ML_W
COL_ML_V = COL_ML_K + ML_W
COL_ML_O = COL_ML_V + ML_W
COL_GATES = COL_ML_O + ML_W

NEG_BIG = -1e30
VMEM_LIMIT_V7X = 56 * 1024 * 1024
BF16_ROWS = 16

NA_QROWS = 8
NA_KROWS = 16
NA_MASKED_SLAB = 2 * NA_WIN_R - 1


def _cparams(sem):
    return pltpu.CompilerParams(dimension_semantics=sem, vmem_limit_bytes=VMEM_LIMIT_V7X)


def _rms(x, gain):
    return x * lax.rsqrt(jnp.mean(x * x, axis=-1, keepdims=True) + EPS) * gain


def _dot(a, b):
    return jnp.dot(a, b, preferred_element_type=F32)


def _dot_nt(a, b):
    return lax.dot_general(a, b, (((1,), (1,)), ((), ())), preferred_element_type=F32)


def _dot_tn(a, b):
    return lax.dot_general(a, b, (((0,), (0,)), ((), ())), preferred_element_type=F32)


def _ada_kernel(c_ref, w_ref, b_ref, o_ref):
    c = c_ref[...]
    a = (c * jax.nn.sigmoid(c)).astype(BF16)
    o_ref[...] = _dot(a, w_ref[...].astype(BF16)) + b_ref[...]


def _ada_call(cond, w, b, tn=1024):
    m, d = cond.shape
    n = w.shape[1]
    return pl.pallas_call(
        _ada_kernel,
        grid=(n // tn,),
        in_specs=[pl.BlockSpec((m, d), lambda j: (0, 0)),
                  pl.BlockSpec((d, tn), lambda j: (0, j)),
                  pl.BlockSpec((1, tn), lambda j: (0, j))],
        out_specs=pl.BlockSpec((m, tn), lambda j: (0, j)),
        out_shape=jax.ShapeDtypeStruct((m, n), F32),
        compiler_params=_cparams(("arbitrary",)),
        name="ada_mod",
    )(cond, w, b)


def _mod_spec(k, rows_per_group, fixed_group):
    def index_map(i, *_):
        grp = fixed_group if fixed_group is not None else i // rows_per_group
        return (grp * N_MOD + k, 0, 0)
    return index_map


def _ffn_kernel(x_ref, sh_ref, sc_ref, gt_ref, gain_ref, wg_ref, wu_ref, wd_ref, fin_ref, o_ref,
                h_scr, acc_scr, *, n_ff, final_norm):
    j = pl.program_id(1)

    @pl.when(j == 0)
    def _():
        h = _rms(x_ref[...], gain_ref[...]) * (1.0 + sc_ref[...]) + sh_ref[...]
        h_scr[...] = h.astype(BF16)
        acc_scr[...] = jnp.zeros_like(acc_scr)

    h = h_scr[...]
    g = _dot(h, wg_ref[...])
    u = _dot(h, wu_ref[...])
    a = (g * jax.nn.sigmoid(g)) * u
    acc_scr[...] += _dot(a.astype(BF16), wd_ref[...])

    @pl.when(j == n_ff - 1)
    def _():
        out = x_ref[...] + 0.5 * gt_ref[...] * acc_scr[...]
        if final_norm:
            out = _rms(out, fin_ref[...])
        o_ref[...] = out


def _ffn_call(x, mod, k0, gain, wg, wu, wd, fin, *, tiles_per_group, fixed_group, final_norm,
              tm=512, tf=512):
    t, d = x.shape
    dff = wg.shape[1]
    n_ff = dff // tf
    mspec = lambda k: pl.BlockSpec((None, 1, d), _mod_spec(k, tiles_per_group, fixed_group))
    return pl.pallas_call(
        functools.partial(_ffn_kernel, n_ff=n_ff, final_norm=final_norm),
        grid=(t // tm, n_ff),
        in_specs=[pl.BlockSpec((tm, d), lambda i, j: (i, 0)),
                  mspec(k0), mspec(k0 + 1), mspec(k0 + 2),
                  pl.BlockSpec((1, d), lambda i, j: (0, 0)),
                  pl.BlockSpec((d, tf), lambda i, j: (0, j)),
                  pl.BlockSpec((d, tf), lambda i, j: (0, j)),
                  pl.BlockSpec((tf, d), lambda i, j: (j, 0)),
                  pl.BlockSpec((1, d), lambda i, j: (0, 0))],
        out_specs=pl.BlockSpec((tm, d), lambda i, j: (i, 0)),
        out_shape=jax.ShapeDtypeStruct((t, d), F32),
        scratch_shapes=[pltpu.VMEM((tm, d), BF16), pltpu.VMEM((tm, d), F32)],
        compiler_params=_cparams(("parallel", "arbitrary")),
        name="macaron_ffn",
    )(x, mod, mod, mod, gain, wg, wu, wd, fin)


def _inproj_kernel(x_ref, sh_ref, sc_ref, gain_ref, w_ref, wgt_ref, p_ref, gt_ref, h_scr):
    j = pl.program_id(1)

    @pl.when(j == 0)
    def _():
        h = _rms(x_ref[...], gain_ref[...]) * (1.0 + sc_ref[...]) + sh_ref[...]
        hb = h.astype(BF16)
        h_scr[...] = hb
        gt_ref[...] = _dot_nt(wgt_ref[...], hb)

    p_ref[...] = _dot(h_scr[...], w_ref[...]).astype(p_ref.dtype)


def _inproj_call(x, mod, gain, w_main, w_gates_t, *, tiles_per_group, fixed_group, tm=512, tn=1664):
    t, d = x.shape
    n = w_main.shape[1]
    mspec = lambda k: pl.BlockSpec((None, 1, d), _mod_spec(k, tiles_per_group, fixed_group))
    return pl.pallas_call(
        _inproj_kernel,
        grid=(t // tm, n // tn),
        in_specs=[pl.BlockSpec((tm, d), lambda i, j: (i, 0)),
                  mspec(3), mspec(4),
                  pl.BlockSpec((1, d), lambda i, j: (0, 0)),
                  pl.BlockSpec((d, tn), lambda i, j: (0, j)),
                  pl.BlockSpec((N_GATES, d), lambda i, j: (0, 0))],
        out_specs=[pl.BlockSpec((tm, tn), lambda i, j: (i, j)),
                   pl.BlockSpec((N_GATES, tm), lambda i, j: (0, i))],
        out_shape=[jax.ShapeDtypeStruct((t, n), BF16),
                   jax.ShapeDtypeStruct((N_GATES, t), F32)],
        scratch_shapes=[pltpu.VMEM((tm, d), BF16)],
        compiler_params=_cparams(("parallel", "arbitrary")),
        name="mixer_in_proj",
    )(x, mod, mod, gain, w_main, w_gates_t)


def _outproj_kernel(x_ref, gt_ref, ya_ref, yb_ref, yc_ref, w_ref, o_ref):
    acc = _dot(ya_ref[...], w_ref[0:NA_W, :])
    acc += _dot(yb_ref[...], w_ref[NA_W:NA_W + GQA_W, :])
    acc += _dot(yc_ref[...], w_ref[NA_W + GQA_W:, :])
    o_ref[...] = x_ref[...] + gt_ref[...] * acc


def _outproj_call(x, mod, ya, yb, yc, w, *, tiles_per_group, fixed_group, tm=512):
    t, d = x.shape
    return pl.pallas_call(
        _outproj_kernel,
        grid=(t // tm,),
        in_specs=[pl.BlockSpec((tm, d), lambda i: (i, 0)),
                  pl.BlockSpec((None, 1, d), _mod_spec(5, tiles_per_group, fixed_group)),
                  pl.BlockSpec((tm, NA_W), lambda i: (i, 0)),
                  pl.BlockSpec((tm, GQA_W), lambda i: (i, 0)),
                  pl.BlockSpec((tm, ML_W), lambda i: (i, 0)),
                  pl.BlockSpec((d, d), lambda i: (0, 0))],
        out_specs=pl.BlockSpec((tm, d), lambda i: (i, 0)),
        out_shape=jax.ShapeDtypeStruct((t, d), F32),
        compiler_params=_cparams(("parallel",)),
        name="mixer_out_proj",
    )(x, mod, ya, yb, yc, w)


def _na_kernel(q_ref, k_ref, v_ref, kc_ref, vc_ref, tl_ref, tr_ref, o_ref, mb_scr, *, rows):
    rb = pl.program_id(2)
    nblk = rows // NA_QROWS
    key_row0 = jnp.clip(rb * NA_QROWS - NA_WIN_R // 2, 0, rows - NA_KROWS)

    @pl.when((rb <= 1) | (rb == nblk - 1))
    def _():
        for a in range(NA_QROWS):
            r = rb * NA_QROWS + a
            row_start = jnp.clip(r - NA_WIN_R // 2, 0, rows - NA_WIN_R)
            for j in range(NA_KROWS // 2):
                sel = []
                for kr in (key_row0 + 2 * j, key_row0 + 2 * j + 1):
                    in_win = (kr >= row_start) & (kr < row_start + NA_WIN_R)
                    sel.append(jnp.where(in_win, kr - r + (NA_WIN_R - 1), NA_MASKED_SLAB))
                mb_scr[a * GRID_W:(a + 1) * GRID_W, 2 * j * GRID_W:(2 * j + 2) * GRID_W] = (
                    tl_ref[sel[0]] + tr_ref[sel[1]])

    start = pl.multiple_of(key_row0 * GRID_W, GRID_W)
    nk = NA_KROWS * GRID_W
    scale = HEAD_DIM ** -0.5
    q = q_ref[...]
    k = k_ref[pl.ds(start, nk), :]
    v = v_ref[pl.ds(start, nk), :]
    s_loc = _dot_nt(q, k) * scale + mb_scr[...]
    s_ctx = _dot_nt(q, kc_ref[...]) * scale
    m = jnp.maximum(jnp.max(s_loc, axis=-1, keepdims=True), jnp.max(s_ctx, axis=-1, keepdims=True))
    p_loc = jnp.exp(s_loc - m)
    p_ctx = jnp.exp(s_ctx - m)
    den = jnp.sum(p_loc, axis=-1, keepdims=True) + jnp.sum(p_ctx, axis=-1, keepdims=True)
    out = _dot(p_loc.astype(BF16), v) + _dot(p_ctx.astype(BF16), vc_ref[...])
    o_ref[...] = (out / den).astype(o_ref.dtype)


def _na_toeplitz_slabs(rpb):
    n_h, n_dr, n_dc = rpb.shape
    edge = GRID_W - NA_WIN_C
    ext = jnp.concatenate(
        [rpb[..., NA_WIN_C - 1:], jnp.broadcast_to(rpb[..., n_dc - 1:], (n_h, n_dr, edge)),
         jnp.zeros((n_h, n_dr, 1), F32),
         jnp.broadcast_to(rpb[..., :1], (n_h, n_dr, edge)), rpb[..., :NA_WIN_C - 1]], axis=-1)
    width = 2 * GRID_W
    tiled = jnp.broadcast_to(ext[:, :, None, :], (n_h, n_dr, GRID_W, width)).reshape(n_h, n_dr, GRID_W * width)
    toep = tiled[..., :GRID_W * (width - 1)].reshape(n_h, n_dr, GRID_W, width - 1)[..., :GRID_W]
    cq = np.arange(GRID_W)
    col_start = np.clip(cq - NA_WIN_C // 2, 0, GRID_W - NA_WIN_C)
    col_ok = (cq[None, :] >= col_start[:, None]) & (cq[None, :] < col_start[:, None] + NA_WIN_C)
    slabs = jnp.where(col_ok, toep, NEG_BIG)
    slabs = jnp.concatenate([slabs, jnp.full((n_h, 1, GRID_W, GRID_W), NEG_BIG, F32)], axis=1)
    zeros = jnp.zeros_like(slabs)
    return jnp.concatenate([slabs, zeros], axis=-1), jnp.concatenate([zeros, slabs], axis=-1)


def _na_call(p_lat, p_ctx, slabs_left, slabs_right, batch, seq, ctx_len):
    rows = seq // GRID_W
    nblk = rows // NA_QROWS
    tq = NA_QROWS * GRID_W
    nk = NA_KROWS * GRID_W
    cq, ck, cv = COL_NA_Q // HEAD_DIM, COL_NA_K // HEAD_DIM, COL_NA_V // HEAD_DIM
    slab_spec = pl.BlockSpec((None, NA_MASKED_SLAB + 1, GRID_W, 2 * GRID_W), lambda b, h, r: (h, 0, 0, 0))

    return pl.pallas_call(
        functools.partial(_na_kernel, rows=rows),
        grid=(batch, NA_HEADS, nblk),
        in_specs=[pl.BlockSpec((tq, HEAD_DIM), lambda b, h, r: (b * nblk + r, cq + h)),
                  pl.BlockSpec((seq, HEAD_DIM), lambda b, h, r: (b, ck + h)),
                  pl.BlockSpec((seq, HEAD_DIM), lambda b, h, r: (b, cv + h)),
                  pl.BlockSpec((ctx_len, HEAD_DIM), lambda b, h, r: (b, ck + h)),
                  pl.BlockSpec((ctx_len, HEAD_DIM), lambda b, h, r: (b, cv + h)),
                  slab_spec, slab_spec],
        out_specs=pl.BlockSpec((tq, HEAD_DIM), lambda b, h, r: (b * nblk + r, h)),
        out_shape=jax.ShapeDtypeStruct((batch * seq, NA_W), BF16),
        scratch_shapes=[pltpu.VMEM((tq, nk), F32)],
        compiler_params=_cparams(("parallel", "parallel", "arbitrary")),
        name="neighbourhood_attn",
    )(p_lat, p_lat, p_lat, p_ctx, p_ctx, slabs_left, slabs_right)


def _ctx_attn_kernel(q_ref, k_ref, v_ref, o_ref):
    scale = HEAD_DIM ** -0.5
    s = _dot_nt(q_ref[...], k_ref[...]) * scale
    m = jnp.max(s, axis=-1, keepdims=True)
    p = jnp.exp(s - m)
    den = jnp.sum(p, axis=-1, keepdims=True)
    o_ref[...] = (_dot(p.astype(BF16), v_ref[...]) / den).astype(o_ref.dtype)


def _ctx_attn_call(q_arr, k_arr, v_arr, batch, ctx_len, n_heads, q_col, k_col, v_col, kv_group):
    return pl.pallas_call(
        _ctx_attn_kernel,
        grid=(batch, n_heads),
        in_specs=[pl.BlockSpec((ctx_len, HEAD_DIM), lambda b, h: (b, q_col + h)),
                  pl.BlockSpec((ctx_len, HEAD_DIM), lambda b, h: (b, k_col + h // kv_group)),
                  pl.BlockSpec((ctx_len, HEAD_DIM), lambda b, h: (b, v_col + h // kv_group))],
        out_specs=pl.BlockSpec((ctx_len, HEAD_DIM), lambda b, h: (b, h)),
        out_shape=jax.ShapeDtypeStruct((batch * ctx_len, n_heads * HEAD_DIM), BF16),
        compiler_params=_cparams(("parallel", "parallel")),
        name="ctx_attn",
    )(q_arr, k_arr, v_arr)


def _rope(x, cos, sin_lo, sin_hi):
    return (x * cos + pltpu.roll(x, HEAD_DIM - HEAD_DIM // 4, 1) * sin_lo
            + pltpu.roll(x, HEAD_DIM // 4, 1) * sin_hi)


def _gqa_prep_kernel(q_ref, k_ref, qg_ref, kg_ref, cos_ref, slo_ref, shi_ref, qo_ref, ko_ref, *, rope):
    def prep(x, gain):
        y = _rms(x.astype(F32), gain)
        if rope:
            y = _rope(y, cos_ref[...], slo_ref[...], shi_ref[...])
        return y.astype(BF16)

    for h in range(GQA_Q_HEADS):
        sl = slice(h * HEAD_DIM, (h + 1) * HEAD_DIM)
        qo_ref[:, sl] = prep(q_ref[:, sl], qg_ref[...])
    for h in range(GQA_KV_HEADS):
        sl = slice(h * HEAD_DIM, (h + 1) * HEAD_DIM)
        ko_ref[:, sl] = prep(k_ref[:, sl], kg_ref[...])


def _rope_tables(n_tokens):
    t = jnp.arange(n_tokens)
    row = (t // GRID_W).astype(F32)
    col = (t % GRID_W).astype(F32)
    half = HEAD_DIM // 2
    inv_freq = 1.0 / (ROPE_THETA ** (jnp.arange(0, half, 2, dtype=F32) / half))
    ang_r = row[:, None] * inv_freq[None, :]
    ang_c = col[:, None] * inv_freq[None, :]
    ang = jnp.concatenate([ang_r, ang_r, ang_c, ang_c], axis=-1)
    cos, sin = jnp.cos(ang), jnp.sin(ang)
    lo = (jnp.arange(HEAD_DIM) % half) < (half // 2)
    return cos, jnp.where(lo, -sin, 0.0), jnp.where(lo, 0.0, sin)


def _gqa_prep_call(p, q_gain, k_gain, tables, seq, *, rope, tm=512):
    t = p.shape[0]
    per_seq = seq // tm if rope else 1
    tab_spec = pl.BlockSpec((tm, HEAD_DIM), lambda i: (i % per_seq, 0))
    return pl.pallas_call(
        functools.partial(_gqa_prep_kernel, rope=rope),
        grid=(t // tm,),
        in_specs=[pl.BlockSpec((tm, GQA_W), lambda i: (i, COL_GQ_Q // GQA_W)),
                  pl.BlockSpec((tm, KV_W), lambda i: (i, COL_GQ_K // KV_W)),
                  pl.BlockSpec((1, HEAD_DIM), lambda i: (0, 0)),
                  pl.BlockSpec((1, HEAD_DIM), lambda i: (0, 0)),
                  tab_spec, tab_spec, tab_spec],
        out_specs=[pl.BlockSpec((tm, GQA_W), lambda i: (i, 0)),
                   pl.BlockSpec((tm, KV_W), lambda i: (i, 0))],
        out_shape=[jax.ShapeDtypeStruct((t, GQA_W), BF16),
                   jax.ShapeDtypeStruct((t, KV_W), BF16)],
        compiler_params=_cparams(("parallel",)),
        name="gqa_prep",
    )(p, p, q_gain, k_gain, *tables)


def _gqa_kernel(q_ref, kl_ref, vl_ref, kc_ref, vc_ref, o_ref):
    scale = HEAD_DIM ** -0.5
    q = q_ref[...]
    s_lat = _dot_nt(q, kl_ref[...]) * scale
    s_ctx = _dot_nt(q, kc_ref[...]) * scale
    m = jnp.maximum(jnp.max(s_lat, axis=-1, keepdims=True), jnp.max(s_ctx, axis=-1, keepdims=True))
    p_lat = jnp.exp(s_lat - m)
    p_ctx = jnp.exp(s_ctx - m)
    den = jnp.sum(p_lat, axis=-1, keepdims=True) + jnp.sum(p_ctx, axis=-1, keepdims=True)
    out = _dot(p_lat.astype(BF16), vl_ref[...]) + _dot(p_ctx.astype(BF16), vc_ref[...])
    o_ref[...] = (out / den).astype(o_ref.dtype)


def _gqa_call(q_lat, k_lat, p_lat, k_ctx, p_ctx, batch, seq, ctx_len, tq=512):
    grp = GQA_Q_HEADS // GQA_KV_HEADS
    nq = seq // tq
    cv = COL_GQ_V // HEAD_DIM
    return pl.pallas_call(
        _gqa_kernel,
        grid=(batch, GQA_Q_HEADS, nq),
        in_specs=[pl.BlockSpec((tq, HEAD_DIM), lambda b, h, i: (b * nq + i, h)),
                  pl.BlockSpec((seq, HEAD_DIM), lambda b, h, i: (b, h // grp)),
                  pl.BlockSpec((seq, HEAD_DIM), lambda b, h, i: (b, cv + h // grp)),
                  pl.BlockSpec((ctx_len, HEAD_DIM), lambda b, h, i: (b, h // grp)),
                  pl.BlockSpec((ctx_len, HEAD_DIM), lambda b, h, i: (b, cv + h // grp))],
        out_specs=pl.BlockSpec((tq, HEAD_DIM), lambda b, h, i: (b * nq + i, h)),
        out_shape=jax.ShapeDtypeStruct((batch * seq, GQA_W), BF16),
        compiler_params=_cparams(("parallel", "parallel", "arbitrary")),
        name="gqa_attn",
    )(q_lat, k_lat, p_lat, k_ctx, p_ctx)


ST_R, ST_W, ST_B, ST_BTOT, ST_AMAX = range(5)
N_STATS = 5
ML_AUG = 2 * HEAD_DIM


def _mlstm_kernel(ql_ref, kl_ref, vl_ref, ol_ref, qc_ref, kc_ref, vc_ref, oc_ref, g_ref, gb_ref,
                  cwq_ref, cwk_ref, cbq_ref, cbk_ref, on_ref, yl_ref, *rest,
                  n_lat, n_ctx, need_ctx):
    if need_ctx:
        yc_ref = rest[0]
        rest = rest[1:]
    qa_l, kt_l, va_l, qa_c, kt_c, va_c, hf_l, hb_l, hf_c, hb_c, st = rest
    L = ML_CHUNK
    h = pl.program_id(1)
    row = lax.broadcasted_iota(jnp.int32, (L, L), 0)
    col = lax.broadcasted_iota(jnp.int32, (L, L), 1)
    eye = row == col

    tri_fw = (row <= col).astype(F32)
    tri_bw = (row >= col).astype(F32)
    for d, tri in enumerate((tri_fw, tri_bw)):
        log_i = g_ref[2 * d * ML_HEADS + h] + gb_ref[2 * d * ML_HEADS + h]
        log_f = jax.nn.log_sigmoid(g_ref[(2 * d + 1) * ML_HEADS + h] + gb_ref[(2 * d + 1) * ML_HEADS + h])
        b = jnp.dot(log_f, tri, preferred_element_type=F32, precision=lax.Precision.HIGHEST)
        b_tot = b[:, L - 1:L] if d == 0 else b[:, 0:1]
        a = b_tot - b + log_i
        a_max = jnp.max(a, axis=-1, keepdims=True)
        base = d * N_STATS
        st[base + ST_R] = log_i - b
        st[base + ST_W] = jnp.exp(a - a_max)
        st[base + ST_B] = b
        st[base + ST_BTOT] = jnp.broadcast_to(b_tot, b.shape)
        st[base + ST_AMAX] = jnp.broadcast_to(a_max, b.shape)

    def prepare_stream(q_src, k_src, v_src, q_dst, kt_dst, va_dst, n_chunks):
        n_tok = n_chunks * L
        sub = lax.broadcasted_iota(jnp.int32, (L, HEAD_DIM), 0)

        def conv_silu(src_ref, w_ref, b_ref, n, t0):
            x = src_ref[pl.ds(t0, L), :].astype(F32)
            prev_t0 = pl.multiple_of(jnp.maximum(t0 - BF16_ROWS, 0), BF16_ROWS)
            next_t0 = pl.multiple_of(jnp.minimum(t0 + L, n_tok - BF16_ROWS), BF16_ROWS)
            prev_row = (src_ref[pl.ds(prev_t0, BF16_ROWS), :].astype(F32)[BF16_ROWS - 1:, :]
                        * (n > 0).astype(F32))
            next_row = (src_ref[pl.ds(next_t0, BF16_ROWS), :].astype(F32)[:1, :]
                        * (n < n_chunks - 1).astype(F32))
            x_prev = jnp.where(sub == 0, prev_row, pltpu.roll(x, 1, 0))
            x_next = jnp.where(sub == L - 1, next_row, pltpu.roll(x, L - 1, 0))
            y = w_ref[0:1, :] * x_prev + w_ref[1:2, :] * x + w_ref[2:3, :] * x_next + b_ref[...]
            return y * jax.nn.sigmoid(y)

        def body(n, carry):
            t0 = pl.multiple_of(n * L, L)
            q_dst[pl.ds(t0, L), :] = conv_silu(q_src, cwq_ref, cbq_ref, n, t0).astype(BF16)
            k_act = conv_silu(k_src, cwk_ref, cbk_ref, n, t0) * (HEAD_DIM ** -0.5)
            kt_dst[:, pl.ds(t0, L)] = k_act.T
            va_dst[pl.ds(t0, L), 0:HEAD_DIM] = v_src[pl.ds(t0, L), :]
            va_dst[pl.ds(t0, L), HEAD_DIM:ML_AUG] = jnp.ones((L, HEAD_DIM), BF16)
            return carry

        lax.fori_loop(0, n_chunks, body, 0)

    prepare_stream(ql_ref, kl_ref, vl_ref, qa_l, kt_l, va_l, n_lat)
    prepare_stream(qc_ref, kc_ref, vc_ref, qa_c, kt_c, va_c, n_ctx)

    def chunk(d, n_stat, t0, q_src, kt_src, va_src, carry, want_out):
        c_prev, m_prev = carry
        base = d * N_STATS
        stat = lambda k: st[base + k, pl.ds(n_stat, 1), :]
        kt = kt_src[:, pl.ds(t0, L)]
        va = va_src[pl.ds(t0, L), :]
        out = None
        if want_out:
            q = q_src[pl.ds(t0, L), :]
            seen = (col <= row) if d == 0 else (col >= row)
            r_b = jnp.broadcast_to(stat(ST_R), (L, L))
            m1 = m_prev[:, 0:1]
            big_r = jnp.maximum(jnp.max(jnp.where(seen, r_b, NEG_BIG), axis=-1, keepdims=True), m1)
            dec_mat = jnp.exp(jnp.where(seen, r_b - big_r, NEG_BIG))
            s = _dot(q, kt.astype(BF16)) * dec_mat
            g = jnp.exp(m1 - big_r)
            acc = g * _dot(q, c_prev.astype(BF16)) + _dot(s.astype(BF16), va)
            b_col = jnp.sum(jnp.where(eye, jnp.broadcast_to(stat(ST_B), (L, L)), 0.0), axis=-1, keepdims=True)
            out = acc[:, :HEAD_DIM] / jnp.maximum(jnp.abs(acc[:, HEAD_DIM:]), jnp.exp(-(b_col + big_r)))
        c_loc = _dot((kt * stat(ST_W)).astype(BF16), va)
        b_tot, a_max = stat(ST_BTOT), stat(ST_AMAX)
        m_new = jnp.maximum(b_tot + m_prev, a_max)
        dec = jnp.exp(b_tot + m_prev - m_new)[:, 0:1]
        inp = jnp.exp(a_max - m_new)[:, 0:1]
        return out, (dec * c_prev + inp * c_loc, m_new)

    def finish(hsum, o_src, t0):
        y = _rms(hsum, on_ref[...]) * jax.nn.sigmoid(o_src[pl.ds(t0, L), :].astype(F32))
        return y.astype(BF16)

    zero_state = (jnp.zeros((HEAD_DIM, ML_AUG), F32), jnp.zeros((1, L), F32))

    def make_body(n_chunks, stat0, q_src, kt_src, va_src, hf_dst, hb_dst, want_out):
        def body(i, carry):
            fw, bw = carry
            nf, nb = i, n_chunks - 1 - i
            tf, tb = pl.multiple_of(nf * L, L), pl.multiple_of(nb * L, L)
            out_f, fw = chunk(0, stat0 + nf, tf, q_src, kt_src, va_src, fw, want_out)
            out_b, bw = chunk(1, stat0 + nb, tb, q_src, kt_src, va_src, bw, want_out)
            if want_out:
                hf_dst[pl.ds(tf, L), :] = out_f
                hb_dst[pl.ds(tb, L), :] = out_b
            return fw, bw
        return body

    carry = lax.fori_loop(0, n_ctx, make_body(n_ctx, n_lat, qa_c, kt_c, va_c, hf_c, hb_c, need_ctx),
                          (zero_state, zero_state))
    lax.fori_loop(0, n_lat, make_body(n_lat, 0, qa_l, kt_l, va_l, hf_l, hb_l, True), carry, unroll=2)

    def finish_lat(n, carry):
        t0 = pl.multiple_of(n * L, L)
        yl_ref[pl.ds(t0, L), :] = finish(hf_l[pl.ds(t0, L), :] + hb_l[pl.ds(t0, L), :], ol_ref, t0)
        return carry

    lax.fori_loop(0, n_lat, finish_lat, 0)
    if need_ctx:
        def finish_ctx(n, carry):
            t0 = pl.multiple_of(n * L, L)
            yc_ref[pl.ds(t0, L), :] = finish(hf_c[pl.ds(t0, L), :] + hb_c[pl.ds(t0, L), :], oc_ref, t0)
            return carry

        lax.fori_loop(0, n_ctx, finish_ctx, 0)


def _mlstm_call(p_lat, p_ctx, gates, gate_b, conv_w, conv_b, out_norm, batch, seq, ctx_len, need_ctx):
    n_lat, n_ctx = seq // ML_CHUNK, ctx_len // ML_CHUNK
    n_pad = gates.shape[2]
    cq, ck = COL_ML_Q // HEAD_DIM, COL_ML_K // HEAD_DIM
    cv, co = COL_ML_V // HEAD_DIM, COL_ML_O // HEAD_DIM
    col = lambda rows, c0: pl.BlockSpec((rows, HEAD_DIM), lambda b, h: (b, c0 + h))
    out_specs = [pl.BlockSpec((seq, HEAD_DIM), lambda b, h: (b, h))]
    out_shape = [jax.ShapeDtypeStruct((batch * seq, ML_W), BF16)]
    if need_ctx:
        out_specs.append(pl.BlockSpec((ctx_len, HEAD_DIM), lambda b, h: (b, h)))
        out_shape.append(jax.ShapeDtypeStruct((batch * ctx_len, ML_W), BF16))

    def stream_scratch(n_tok):
        return [pltpu.VMEM((n_tok, HEAD_DIM), BF16), pltpu.VMEM((HEAD_DIM, n_tok), F32),
                pltpu.VMEM((n_tok, ML_AUG), BF16)]

    res = pl.pallas_call(
        functools.partial(_mlstm_kernel, n_lat=n_lat, n_ctx=n_ctx, need_ctx=need_ctx),
        grid=(batch, ML_HEADS),
        in_specs=[col(seq, cq), col(seq, ck), col(seq, cv), col(seq, co),
                  col(ctx_len, cq), col(ctx_len, ck), col(ctx_len, cv), col(ctx_len, co),
                  pl.BlockSpec((N_GATES, None, n_pad, ML_CHUNK), lambda b, h: (0, b, 0, 0)),
                  pl.BlockSpec((N_GATES, 1, ML_CHUNK), lambda b, h: (0, 0, 0)),
                  pl.BlockSpec((3, HEAD_DIM), lambda b, h: (0, h)),
                  pl.BlockSpec((3, HEAD_DIM), lambda b, h: (0, ML_HEADS + h)),
                  pl.BlockSpec((1, HEAD_DIM), lambda b, h: (0, h)),
                  pl.BlockSpec((1, HEAD_DIM), lambda b, h: (0, ML_HEADS + h)),
                  pl.BlockSpec((1, HEAD_DIM), lambda b, h: (0, h))],
        out_specs=out_specs,
        out_shape=out_shape,
        scratch_shapes=(stream_scratch(seq) + stream_scratch(ctx_len)
                        + [pltpu.VMEM((seq, HEAD_DIM), F32), pltpu.VMEM((seq, HEAD_DIM), F32),
                           pltpu.VMEM((ctx_len, HEAD_DIM), F32), pltpu.VMEM((ctx_len, HEAD_DIM), F32),
                           pltpu.VMEM((2 * N_STATS, n_pad, ML_CHUNK), F32)]),
        compiler_params=_cparams(("parallel", "parallel")),
        name="mlstm",
    )(p_lat, p_lat, p_lat, p_lat, p_ctx, p_ctx, p_ctx, p_ctx, gates, gate_b,
      conv_w, conv_w, conv_b, conv_b, out_norm)
    return (res[0], res[1]) if need_ctx else (res[0], None)


def kernel(x, c, ctx, c_ctx, w_ada, b_ada, norm_ff1, ff1_gate, ff1_up, ff1_down, norm_mix, w_in, na_rpb,
           gqa_q_norm, gqa_k_norm, ml_conv_w, ml_conv_b, ml_gate_b, ml_out_norm, w_out, norm_ff2,
           ff2_gate, ff2_up, ff2_down, final_norm):
    batch, seq, d = x.shape
    ctx_len = ctx.shape[1]
    depth = w_ada.shape[0]
    tm = 512
    lat_tiles = seq // tm
    ctx_group = batch
    n_lat_chunks, n_ctx_chunks = seq // ML_CHUNK, ctx_len // ML_CHUNK
    n_chunk_pad = -(-(n_lat_chunks + n_ctx_chunks) // 8) * 8

    xl = x.reshape(batch * seq, d)
    xc = ctx.reshape(batch * ctx_len, d)
    cond = jnp.zeros((8, d), F32).at[:batch].set(c).at[batch].set(c_ctx)
    rope_tabs = _rope_tables(seq)
    row2 = lambda v: v.reshape(1, -1)
    fin = row2(final_norm)

    for l in range(depth):
        last = l == depth - 1
        mod = _ada_call(cond, w_ada[l], row2(b_ada[l])).reshape(8 * N_MOD, 1, d)
        lat = dict(tiles_per_group=lat_tiles, fixed_group=None)
        cx = dict(tiles_per_group=None, fixed_group=ctx_group)

        wg, wu, wd = ff1_gate[l].astype(BF16), ff1_up[l].astype(BF16), ff1_down[l].astype(BF16)
        g1 = row2(norm_ff1[l])
        xl = _ffn_call(xl, mod, 0, g1, wg, wu, wd, fin, final_norm=False, **lat)
        xc = _ffn_call(xc, mod, 0, g1, wg, wu, wd, fin, final_norm=False, **cx)

        w_main = w_in[l][:, :COL_GATES].astype(BF16)
        w_gates_t = w_in[l][:, COL_GATES:].T.astype(BF16)
        gm = row2(norm_mix[l])
        p_lat, gt_lat = _inproj_call(xl, mod, gm, w_main, w_gates_t, **lat)
        p_ctx, gt_ctx = _inproj_call(xc, mod, gm, w_main, w_gates_t, **cx)

        ya_lat = _na_call(p_lat, p_ctx, *_na_toeplitz_slabs(na_rpb[l]), batch, seq, ctx_len)
        qg, kg = row2(gqa_q_norm[l]), row2(gqa_k_norm[l])
        qn_lat, kn_lat = _gqa_prep_call(p_lat, qg, kg, rope_tabs, seq, rope=True)
        qn_ctx, kn_ctx = _gqa_prep_call(p_ctx, qg, kg, rope_tabs, seq, rope=False, tm=ctx_len)
        yb_lat = _gqa_call(qn_lat, kn_lat, p_lat, kn_ctx, p_ctx, batch, seq, ctx_len)
        gates = jnp.concatenate(
            [gt_lat.reshape(N_GATES, batch, n_lat_chunks, ML_CHUNK),
             gt_ctx.reshape(N_GATES, batch, n_ctx_chunks, ML_CHUNK),
             jnp.zeros((N_GATES, batch, n_chunk_pad - n_lat_chunks - n_ctx_chunks, ML_CHUNK), F32)], axis=2)
        gate_b = jnp.broadcast_to(ml_gate_b[l][:, None, None], (N_GATES, 1, ML_CHUNK))
        yc_lat, yc_ctx = _mlstm_call(p_lat, p_ctx, gates, gate_b, ml_conv_w[l], row2(ml_conv_b[l]),
                                     row2(ml_out_norm[l]), batch, seq, ctx_len, not last)

        wo = w_out[l].astype(BF16)
        xl = _outproj_call(xl, mod, ya_lat, yb_lat, yc_lat, wo, **lat)
        wg, wu, wd = ff2_gate[l].astype(BF16), ff2_up[l].astype(BF16), ff2_down[l].astype(BF16)
        g2 = row2(norm_ff2[l])
        xl = _ffn_call(xl, mod, 6, g2, wg, wu, wd, fin, final_norm=last, **lat)
        if not last:
            ya_ctx = _ctx_attn_call(p_ctx, p_ctx, p_ctx, batch, ctx_len, NA_HEADS,
                                    COL_NA_Q // HEAD_DIM, COL_NA_K // HEAD_DIM, COL_NA_V // HEAD_DIM, 1)
            yb_ctx = _ctx_attn_call(qn_ctx, kn_ctx, p_ctx, batch, ctx_len, GQA_Q_HEADS, 0, 0,
                                    COL_GQ_V // HEAD_DIM, GQA_Q_HEADS // GQA_KV_HEADS)
            xc = _outproj_call(xc, mod, ya_ctx, yb_ctx, yc_ctx, wo, **cx)
            xc = _ffn_call(xc, mod, 6, g2, wg, wu, wd, fin, final_norm=False, **cx)
    return xl.reshape(batch, seq, d)
```

```python
import functools

import numpy as np
import jax
import jax.numpy as jnp
from jax import lax
from jax.experimental import pallas as pl
from jax.experimental.pallas import tpu as pltpu

F32 = jnp.float32
BF16 = jnp.bfloat16

GRID_W = 64
HEAD_DIM = 128
NA_HEADS = 4
GQA_Q_HEADS = 4
GQA_KV_HEADS = 2
ML_HEADS = 8
NA_W = NA_HEADS * HEAD_DIM
GQA_W = GQA_Q_HEADS * HEAD_DIM
KV_W = GQA_KV_HEADS * HEAD_DIM
ML_W = ML_HEADS * HEAD_DIM
NA_WIN_R = 8
NA_WIN_C = 16
ML_CHUNK = 128
ROPE_THETA = 10000.0
N_MOD = 9
EPS = 1e-6
N_GATES = 4 * ML_HEADS
COL_NA_Q = 0
COL_NA_K = NA_W
COL_NA_V = 2 * NA_W
COL_GQ_Q = 3 * NA_W
COL_GQ_K = COL_GQ_Q + GQA_W
COL_GQ_V = COL_GQ_K + KV_W
COL_ML_Q = COL_GQ_V + KV_W
COL_ML_K = COL_ML_Q + ML_W
COL_ML_V = COL_ML_K + ML_W
COL_ML_O = COL_ML_V + ML_W
COL_GATES = COL_ML_O + ML_W

NEG_BIG = -1e30
VMEM_LIMIT_V7X = 56 * 1024 * 1024
BF16_ROWS = 16

NA_QROWS = 8
NA_KROWS = 16
NA_MASKED_SLAB = 2 * NA_WIN_R - 1


def _cparams(sem):
    return pltpu.CompilerParams(dimension_semantics=sem, vmem_limit_bytes=VMEM_LIMIT_V7X)


def _rms(x, gain):
    return x * lax.rsqrt(jnp.mean(x * x, axis=-1, keepdims=True) + EPS) * gain


def _dot(a, b):
    return jnp.dot(a, b, preferred_element_type=F32)


def _dot_nt(a, b):
    return lax.dot_general(a, b, (((1,), (1,)), ((), ())), preferred_element_type=F32)


def _dot_tn(a, b):
    return lax.dot_general(a, b, (((0,), (0,)), ((), ())), preferred_element_type=F32)


def _ada_kernel(c_ref, w_ref, b_ref, o_ref):
    c = c_ref[...]
    a = (c * jax.nn.sigmoid(c)).astype(BF16)
    o_ref[...] = _dot(a, w_ref[...].astype(BF16)) + b_ref[...]


def _ada_call(cond, w, b, tn=1024):
    m, d = cond.shape
    n = w.shape[1]
    return pl.pallas_call(
        _ada_kernel,
        grid=(n // tn,),
        in_specs=[pl.BlockSpec((m, d), lambda j: (0, 0)),
                  pl.BlockSpec((d, tn), lambda j: (0, j)),
                  pl.BlockSpec((1, tn), lambda j: (0, j))],
        out_specs=pl.BlockSpec((m, tn), lambda j: (0, j)),
        out_shape=jax.ShapeDtypeStruct((m, n), F32),
        compiler_params=_cparams(("arbitrary",)),
        name="ada_mod",
    )(cond, w, b)


def _mod_spec(k, tm, group_rows, fixed_group):
    def index_map(i, *_):
        grp = fixed_group if fixed_group is not None else i // (group_rows // tm)
        return (grp * N_MOD + k, 0, 0)
    return index_map


def _ffn_kernel(x_ref, sh_ref, sc_ref, gt_ref, gain_ref, wg_ref, wu_ref, wd_ref, fin_ref, o_ref,
                h_scr, acc_scr, *, n_ff, final_norm):
    j = pl.program_id(1)

    @pl.when(j == 0)
    def _():
        h = _rms(x_ref[...], gain_ref[...]) * (1.0 + sc_ref[...]) + sh_ref[...]
        h_scr[...] = h.astype(BF16)
        acc_scr[...] = jnp.zeros_like(acc_scr)

    h = h_scr[...]
    g = _dot(h, wg_ref[...])
    u = _dot(h, wu_ref[...])
    a = (g * jax.nn.sigmoid(g)) * u
    acc_scr[...] += _dot(a.astype(BF16), wd_ref[...])

    @pl.when(j == n_ff - 1)
    def _():
        out = x_ref[...] + 0.5 * gt_ref[...] * acc_scr[...]
        if final_norm:
            out = _rms(out, fin_ref[...])
        o_ref[...] = out


def _ffn_call(x, mod, k0, gain, wg, wu, wd, fin, *, group_rows, fixed_group, final_norm,
              tm=512, tf=512):
    t, d = x.shape
    dff = wg.shape[1]
    n_ff = dff // tf
    mspec = lambda k: pl.BlockSpec((None, 1, d), _mod_spec(k, tm, group_rows, fixed_group))
    return pl.pallas_call(
        functools.partial(_ffn_kernel, n_ff=n_ff, final_norm=final_norm),
        grid=(t // tm, n_ff),
        in_specs=[pl.BlockSpec((tm, d), lambda i, j: (i, 0)),
                  mspec(k0), mspec(k0 + 1), mspec(k0 + 2),
                  pl.BlockSpec((1, d), lambda i, j: (0, 0)),
                  pl.BlockSpec((d, tf), lambda i, j: (0, j)),
                  pl.BlockSpec((d, tf), lambda i, j: (0, j)),
                  pl.BlockSpec((tf, d), lambda i, j: (j, 0)),
                  pl.BlockSpec((1, d), lambda i, j: (0, 0))],
        out_specs=pl.BlockSpec((tm, d), lambda i, j: (i, 0)),
        out_shape=jax.ShapeDtypeStruct((t, d), F32),
        scratch_shapes=[pltpu.VMEM((tm, d), BF16), pltpu.VMEM((tm, d), F32)],
        compiler_params=_cparams(("parallel", "arbitrary")),
        name="macaron_ffn",
    )(x, mod, mod, mod, gain, wg, wu, wd, fin)


def _inproj_kernel(x_ref, sh_ref, sc_ref, gain_ref, w_ref, wgt_ref, p_ref, gt_ref, h_scr):
    j = pl.program_id(1)

    @pl.when(j == 0)
    def _():
        h = _rms(x_ref[...], gain_ref[...]) * (1.0 + sc_ref[...]) + sh_ref[...]
        hb = h.astype(BF16)
        h_scr[...] = hb
        gt_ref[...] = _dot_nt(wgt_ref[...], hb)

    p_ref[...] = _dot(h_scr[...], w_ref[...]).astype(p_ref.dtype)


def _inproj_call(x, mod, gain, w_main, w_gates_t, *, group_rows, fixed_group, tm=512, tn=1664):
    t, d = x.shape
    n = w_main.shape[1]
    mspec = lambda k: pl.BlockSpec((None, 1, d), _mod_spec(k, tm, group_rows, fixed_group))
    return pl.pallas_call(
        _inproj_kernel,
        grid=(t // tm, n // tn),
        in_specs=[pl.BlockSpec((tm, d), lambda i, j: (i, 0)),
                  mspec(3), mspec(4),
                  pl.BlockSpec((1, d), lambda i, j: (0, 0)),
                  pl.BlockSpec((d, tn), lambda i, j: (0, j)),
                  pl.BlockSpec((N_GATES, d), lambda i, j: (0, 0))],
        out_specs=[pl.BlockSpec((tm, tn), lambda i, j: (i, j)),
                   pl.BlockSpec((N_GATES, tm), lambda i, j: (0, i))],
        out_shape=[jax.ShapeDtypeStruct((t, n), BF16),
                   jax.ShapeDtypeStruct((N_GATES, t), F32)],
        scratch_shapes=[pltpu.VMEM((tm, d), BF16)],
        compiler_params=_cparams(("parallel", "arbitrary")),
        name="mixer_in_proj",
    )(x, mod, mod, gain, w_main, w_gates_t)


def _outproj_kernel(x_ref, gt_ref, ya_ref, yb_ref, yc_ref, w_ref, o_ref):
    acc = _dot(ya_ref[...], w_ref[0:NA_W, :])
    acc += _dot(yb_ref[...], w_ref[NA_W:NA_W + GQA_W, :])
    acc += _dot(yc_ref[...], w_ref[NA_W + GQA_W:, :])
    o_ref[...] = x_ref[...] + gt_ref[...] * acc


def _outproj_call(x, mod, ya, yb, yc, w, *, group_rows, fixed_group, tm=512):
    t, d = x.shape
    return pl.pallas_call(
        _outproj_kernel,
        grid=(t // tm,),
        in_specs=[pl.BlockSpec((tm, d), lambda i: (i, 0)),
                  pl.BlockSpec((None, 1, d), _mod_spec(5, tm, group_rows, fixed_group)),
                  pl.BlockSpec((tm, NA_W), lambda i: (i, 0)),
                  pl.BlockSpec((tm, GQA_W), lambda i: (i, 0)),
                  pl.BlockSpec((tm, ML_W), lambda i: (i, 0)),
                  pl.BlockSpec((d, d), lambda i: (0, 0))],
        out_specs=pl.BlockSpec((tm, d), lambda i: (i, 0)),
        out_shape=jax.ShapeDtypeStruct((t, d), F32),
        compiler_params=_cparams(("parallel",)),
        name="mixer_out_proj",
    )(x, mod, ya, yb, yc, w)


def _na_kernel(q_ref, k_ref, v_ref, kc_ref, vc_ref, tl_ref, tr_ref, o_ref, mb_scr, *, rows):
    rb = pl.program_id(2)
    nblk = rows // NA_QROWS
    key_row0 = jnp.clip(rb * NA_QROWS - NA_WIN_R // 2, 0, rows - NA_KROWS)

    @pl.when((rb <= 1) | (rb == nblk - 1))
    def _():
        for a in range(NA_QROWS):
            r = rb * NA_QROWS + a
            row_start = jnp.clip(r - NA_WIN_R // 2, 0, rows - NA_WIN_R)
            for j in range(NA_KROWS // 2):
                sel = []
                for kr in (key_row0 + 2 * j, key_row0 + 2 * j + 1):
                    in_win = (kr >= row_start) & (kr < row_start + NA_WIN_R)
                    sel.append(jnp.where(in_win, kr - r + (NA_WIN_R - 1), NA_MASKED_SLAB))
                mb_scr[a * GRID_W:(a + 1) * GRID_W, 2 * j * GRID_W:(2 * j + 2) * GRID_W] = (
                    tl_ref[sel[0]] + tr_ref[sel[1]])

    start = pl.multiple_of(key_row0 * GRID_W, GRID_W)
    nk = NA_KROWS * GRID_W
    scale = HEAD_DIM ** -0.5
    q = q_ref[...]
    k = k_ref[pl.ds(start, nk), :]
    v = v_ref[pl.ds(start, nk), :]
    s_loc = _dot_nt(q, k) * scale + mb_scr[...]
    s_ctx = _dot_nt(q, kc_ref[...]) * scale
    m = jnp.maximum(jnp.max(s_loc, axis=-1, keepdims=True), jnp.max(s_ctx, axis=-1, keepdims=True))
    p_loc = jnp.exp(s_loc - m)
    p_ctx = jnp.exp(s_ctx - m)
    den = jnp.sum(p_loc, axis=-1, keepdims=True) + jnp.sum(p_ctx, axis=-1, keepdims=True)
    out = _dot(p_loc.astype(BF16), v) + _dot(p_ctx.astype(BF16), vc_ref[...])
    o_ref[...] = (out / den).astype(o_ref.dtype)


def _na_toeplitz_slabs(rpb):
    n_h, n_dr, n_dc = rpb.shape
    edge = GRID_W - NA_WIN_C
    ext = jnp.concatenate(
        [rpb[..., NA_WIN_C - 1:], jnp.broadcast_to(rpb[..., n_dc - 1:], (n_h, n_dr, edge)),
         jnp.zeros((n_h, n_dr, 1), F32),
         jnp.broadcast_to(rpb[..., :1], (n_h, n_dr, edge)), rpb[..., :NA_WIN_C - 1]], axis=-1)
    width = 2 * GRID_W
    tiled = jnp.broadcast_to(ext[:, :, None, :], (n_h, n_dr, GRID_W, width)).reshape(n_h, n_dr, GRID_W * width)
    toep = tiled[..., :GRID_W * (width - 1)].reshape(n_h, n_dr, GRID_W, width - 1)[..., :GRID_W]
    cq = np.arange(GRID_W)
    col_start = np.clip(cq - NA_WIN_C // 2, 0, GRID_W - NA_WIN_C)
    col_ok = (cq[None, :] >= col_start[:, None]) & (cq[None, :] < col_start[:, None] + NA_WIN_C)
    slabs = jnp.where(col_ok, toep, NEG_BIG)
    slabs = jnp.concatenate([slabs, jnp.full((n_h, 1, GRID_W, GRID_W), NEG_BIG, F32)], axis=1)
    zeros = jnp.zeros_like(slabs)
    return jnp.concatenate([slabs, zeros], axis=-1), jnp.concatenate([zeros, slabs], axis=-1)


def _na_call(p_lat, p_ctx, slabs_left, slabs_right, batch, seq, ctx_len):
    rows = seq // GRID_W
    nblk = rows // NA_QROWS
    tq = NA_QROWS * GRID_W
    nk = NA_KROWS * GRID_W
    cq, ck, cv = COL_NA_Q // HEAD_DIM, COL_NA_K // HEAD_DIM, COL_NA_V // HEAD_DIM
    slab_spec = pl.BlockSpec((None, NA_MASKED_SLAB + 1, GRID_W, 2 * GRID_W), lambda b, h, r: (h, 0, 0, 0))

    return pl.pallas_call(
        functools.partial(_na_kernel, rows=rows),
        grid=(batch, NA_HEADS, nblk),
        in_specs=[pl.BlockSpec((tq, HEAD_DIM), lambda b, h, r: (b * nblk + r, cq + h)),
                  pl.BlockSpec((seq, HEAD_DIM), lambda b, h, r: (b, ck + h)),
                  pl.BlockSpec((seq, HEAD_DIM), lambda b, h, r: (b, cv + h)),
                  pl.BlockSpec((ctx_len, HEAD_DIM), lambda b, h, r: (b, ck + h)),
                  pl.BlockSpec((ctx_len, HEAD_DIM), lambda b, h, r: (b, cv + h)),
                  slab_spec, slab_spec],
        out_specs=pl.BlockSpec((tq, HEAD_DIM), lambda b, h, r: (b * nblk + r, h)),
        out_shape=jax.ShapeDtypeStruct((batch * seq, NA_W), BF16),
        scratch_shapes=[pltpu.VMEM((tq, nk), F32)],
        compiler_params=_cparams(("parallel", "parallel", "arbitrary")),
        name="neighbourhood_attn",
    )(p_lat, p_lat, p_lat, p_ctx, p_ctx, slabs_left, slabs_right)


def _ctx_attn_kernel(q_ref, k_ref, v_ref, o_ref, *, scale):
    s = _dot_nt(q_ref[...], k_ref[...])
    if scale is not None:
        s = s * scale
    m = jnp.max(s, axis=-1, keepdims=True)
    p = jnp.exp(s - m)
    den = jnp.sum(p, axis=-1, keepdims=True)
    o_ref[...] = (_dot(p.astype(BF16), v_ref[...]) / den).astype(o_ref.dtype)


def _ctx_attn_call(q_arr, k_arr, v_arr, batch, ctx_len, n_heads, q_col, k_col, v_col, kv_group, scale):
    return pl.pallas_call(
        functools.partial(_ctx_attn_kernel, scale=scale),
        grid=(batch, n_heads),
        in_specs=[pl.BlockSpec((ctx_len, HEAD_DIM), lambda b, h: (b, q_col + h)),
                  pl.BlockSpec((ctx_len, HEAD_DIM), lambda b, h: (b, k_col + h // kv_group)),
                  pl.BlockSpec((ctx_len, HEAD_DIM), lambda b, h: (b, v_col + h // kv_group))],
        out_specs=pl.BlockSpec((ctx_len, HEAD_DIM), lambda b, h: (b, h)),
        out_shape=jax.ShapeDtypeStruct((batch * ctx_len, n_heads * HEAD_DIM), BF16),
        compiler_params=_cparams(("parallel", "parallel")),
        name="ctx_attn",
    )(q_arr, k_arr, v_arr)


def _rope(x, cos, sin_lo, sin_hi):
    return (x * cos + pltpu.roll(x, HEAD_DIM - HEAD_DIM // 4, 1) * sin_lo
            + pltpu.roll(x, HEAD_DIM // 4, 1) * sin_hi)


def _gqa_prep_kernel(q_ref, k_ref, qg_ref, kg_ref, cos_ref, slo_ref, shi_ref, qo_ref, ko_ref, *, rope):
    def prep(x, gain, scale):
        y = _rms(x.astype(F32), gain)
        if rope:
            y = _rope(y, cos_ref[...], slo_ref[...], shi_ref[...])
        if scale is not None:
            y = y * scale
        return y.astype(BF16)

    for h in range(GQA_Q_HEADS):
        sl = slice(h * HEAD_DIM, (h + 1) * HEAD_DIM)
        qo_ref[:, sl] = prep(q_ref[:, sl], qg_ref[...], HEAD_DIM ** -0.5)
    for h in range(GQA_KV_HEADS):
        sl = slice(h * HEAD_DIM, (h + 1) * HEAD_DIM)
        ko_ref[:, sl] = prep(k_ref[:, sl], kg_ref[...], None)


def _rope_tables(n_tokens):
    t = jnp.arange(n_tokens)
    row = (t // GRID_W).astype(F32)
    col = (t % GRID_W).astype(F32)
    half = HEAD_DIM // 2
    inv_freq = 1.0 / (ROPE_THETA ** (jnp.arange(0, half, 2, dtype=F32) / half))
    ang_r = row[:, None] * inv_freq[None, :]
    ang_c = col[:, None] * inv_freq[None, :]
    ang = jnp.concatenate([ang_r, ang_r, ang_c, ang_c], axis=-1)
    cos, sin = jnp.cos(ang), jnp.sin(ang)
    lo = (jnp.arange(HEAD_DIM) % half) < (half // 2)
    return cos, jnp.where(lo, -sin, 0.0), jnp.where(lo, 0.0, sin)


def _gqa_prep_call(p, q_gain, k_gain, tables, seq, *, rope, tm=512):
    t = p.shape[0]
    per_seq = seq // tm if rope else 1
    tab_spec = pl.BlockSpec((tm, HEAD_DIM), lambda i: (i % per_seq, 0))
    return pl.pallas_call(
        functools.partial(_gqa_prep_kernel, rope=rope),
        grid=(t // tm,),
        in_specs=[pl.BlockSpec((tm, GQA_W), lambda i: (i, COL_GQ_Q // GQA_W)),
                  pl.BlockSpec((tm, KV_W), lambda i: (i, COL_GQ_K // KV_W)),
                  pl.BlockSpec((1, HEAD_DIM), lambda i: (0, 0)),
                  pl.BlockSpec((1, HEAD_DIM), lambda i: (0, 0)),
                  tab_spec, tab_spec, tab_spec],
        out_specs=[pl.BlockSpec((tm, GQA_W), lambda i: (i, 0)),
                   pl.BlockSpec((tm, KV_W), lambda i: (i, 0))],
        out_shape=[jax.ShapeDtypeStruct((t, GQA_W), BF16),
                   jax.ShapeDtypeStruct((t, KV_W), BF16)],
        compiler_params=_cparams(("parallel",)),
        name="gqa_prep",
    )(p, p, q_gain, k_gain, *tables)


def _gqa_kernel(q_ref, kl_ref, vl_ref, kc_ref, vc_ref, o_ref, *, tk):
    tq = q_ref.shape[0]
    grp = q_ref.shape[1] // HEAD_DIM
    q = jnp.concatenate([q_ref[:, g * HEAD_DIM:(g + 1) * HEAD_DIM] for g in range(grp)], axis=0)

    s = _dot_nt(q, kc_ref[...])
    m = jnp.max(s, axis=-1, keepdims=True)
    p = jnp.exp(s - m)
    den = jnp.sum(p, axis=-1, keepdims=True)
    acc = _dot(p.astype(BF16), vc_ref[...])

    def tile(i, carry):
        m, den, acc = carry
        t0 = pl.multiple_of(i * tk, tk)
        s = _dot_nt(q, kl_ref[pl.ds(t0, tk), :])
        m_new = jnp.maximum(m, jnp.max(s, axis=-1, keepdims=True))
        p = jnp.exp(s - m_new)
        alpha = jnp.exp(m - m_new)
        den = alpha * den + jnp.sum(p, axis=-1, keepdims=True)
        acc = alpha * acc + _dot(p.astype(BF16), vl_ref[pl.ds(t0, tk), :])
        return m_new, den, acc

    m, den, acc = lax.fori_loop(0, kl_ref.shape[0] // tk, tile, (m, den, acc), unroll=True)
    out = (acc / den).astype(o_ref.dtype)
    for g in range(grp):
        o_ref[:, g * HEAD_DIM:(g + 1) * HEAD_DIM] = out[g * tq:(g + 1) * tq]


def _gqa_call(q_lat, k_lat, p_lat, k_ctx, p_ctx, batch, seq, ctx_len, tq=256, tk=512):
    grp = GQA_Q_HEADS // GQA_KV_HEADS
    nq = seq // tq
    cv = COL_GQ_V // HEAD_DIM
    return pl.pallas_call(
        functools.partial(_gqa_kernel, tk=tk),
        grid=(batch, GQA_KV_HEADS, nq),
        in_specs=[pl.BlockSpec((tq, grp * HEAD_DIM), lambda b, h, i: (b * nq + i, h)),
                  pl.BlockSpec((seq, HEAD_DIM), lambda b, h, i: (b, h)),
                  pl.BlockSpec((seq, HEAD_DIM), lambda b, h, i: (b, cv + h)),
                  pl.BlockSpec((ctx_len, HEAD_DIM), lambda b, h, i: (b, h)),
                  pl.BlockSpec((ctx_len, HEAD_DIM), lambda b, h, i: (b, cv + h))],
        out_specs=pl.BlockSpec((tq, grp * HEAD_DIM), lambda b, h, i: (b * nq + i, h)),
        out_shape=jax.ShapeDtypeStruct((batch * seq, GQA_W), BF16),
        compiler_params=_cparams(("parallel", "parallel", "arbitrary")),
        name="gqa_attn",
    )(q_lat, k_lat, p_lat, k_ctx, p_ctx)


ST_R, ST_W, ST_BTOT, ST_AMAX, ST_MPREV, ST_DEC, ST_INP, ST_BIGR, ST_M = range(9)
N_STATS = 9
REP_R, REP_M = range(2)
ML_AUG = 2 * HEAD_DIM


def _mlstm_kernel(ql_ref, kl_ref, vl_ref, ol_ref, qc_ref, kc_ref, vc_ref, oc_ref, g_ref, gb_ref,
                  cwq_ref, cwk_ref, cbq_ref, cbk_ref, on_ref, yl_ref, *rest,
                  n_lat, n_ctx, need_ctx):
    if need_ctx:
        yc_ref = rest[0]
        rest = rest[1:]
    qa_l, kt_l, va_l, qa_c, kt_c, va_c, hf_l, hb_l, hf_c, hb_c, st, rep = rest
    L = ML_CHUNK
    n_all = n_lat + n_ctx
    h = pl.program_id(1)
    row = lax.broadcasted_iota(jnp.int32, (L, L), 0)
    col = lax.broadcasted_iota(jnp.int32, (L, L), 1)

    tri_fw = (row <= col).astype(F32)
    tri_bw = (row >= col).astype(F32)
    b_cum = []
    for d, tri in enumerate((tri_fw, tri_bw)):
        log_i = g_ref[2 * d * ML_HEADS + h] + gb_ref[2 * d * ML_HEADS + h]
        log_f = jax.nn.log_sigmoid(g_ref[(2 * d + 1) * ML_HEADS + h] + gb_ref[(2 * d + 1) * ML_HEADS + h])
        b = jnp.dot(log_f, tri, preferred_element_type=F32, precision=lax.Precision.HIGHEST)
        b_tot = b[:, L - 1:L] if d == 0 else b[:, 0:1]
        a = b_tot - b + log_i
        a_max = jnp.max(a, axis=-1, keepdims=True)
        r = log_i - b
        lane = lax.broadcasted_iota(jnp.int32, r.shape, 1)
        run = r
        shift = 1
        while shift < L:
            if d == 0:
                run = jnp.maximum(run, jnp.where(lane >= shift, pltpu.roll(run, shift, 1), NEG_BIG))
            else:
                run = jnp.maximum(run, jnp.where(lane < L - shift, pltpu.roll(run, L - shift, 1), NEG_BIG))
            shift *= 2
        base = d * N_STATS
        st[base + ST_R] = r
        st[base + ST_W] = jnp.exp(a - a_max)
        st[base + ST_BTOT] = jnp.broadcast_to(b_tot, b.shape)
        st[base + ST_AMAX] = jnp.broadcast_to(a_max, b.shape)
        st[base + ST_BIGR] = run
        st[base + ST_MPREV] = jnp.zeros_like(b)
        b_cum.append(b)

    def stabiliser_step(i, carry):
        new = []
        for d, m_prev in enumerate(carry):
            in_ctx = i < n_ctx
            j = jnp.where(in_ctx, i, i - n_ctx)
            if d == 0:
                n = jnp.where(in_ctx, n_lat + j, j)
            else:
                n = jnp.where(in_ctx, n_lat + n_ctx - 1 - j, n_lat - 1 - j)
            base = d * N_STATS
            b_tot = st[base + ST_BTOT, pl.ds(n, 1), :]
            a_max = st[base + ST_AMAX, pl.ds(n, 1), :]
            m_new = jnp.maximum(b_tot + m_prev, a_max)
            st[base + ST_MPREV, pl.ds(n, 1), :] = m_prev
            st[base + ST_DEC, pl.ds(n, 1), :] = jnp.exp(b_tot + m_prev - m_new)
            st[base + ST_INP, pl.ds(n, 1), :] = jnp.exp(a_max - m_new)
            new.append(m_new)
        return tuple(new)

    lax.fori_loop(0, n_all, stabiliser_step, (jnp.zeros((1, L), F32), jnp.zeros((1, L), F32)))

    for d in range(2):
        base = d * N_STATS
        big_r = jnp.maximum(st[base + ST_BIGR], st[base + ST_MPREV])
        st[base + ST_BIGR] = big_r
        st[base + ST_M] = b_cum[d] + big_r

    def column_tables(n, carry):
        for d in range(2):
            for k_rep, k_st in ((REP_R, ST_BIGR), (REP_M, ST_M)):
                rows = jnp.broadcast_to(st[d * N_STATS + k_st, pl.ds(n, 1), :], (L, L))
                rep[2 * d + k_rep, n] = rows.T
        return carry

    lax.fori_loop(0, n_all, column_tables, 0, unroll=2)

    def prepare_stream(q_src, k_src, v_src, q_dst, kt_dst, va_dst, n_chunks):
        n_tok = n_chunks * L
        sub = lax.broadcasted_iota(jnp.int32, (L, HEAD_DIM), 0)

        def conv_silu(src_ref, w_ref, b_ref, n, t0):
            x = src_ref[pl.ds(t0, L), :].astype(F32)
            prev_t0 = pl.multiple_of(jnp.maximum(t0 - BF16_ROWS, 0), BF16_ROWS)
            next_t0 = pl.multiple_of(jnp.minimum(t0 + L, n_tok - BF16_ROWS), BF16_ROWS)
            prev_row = (src_ref[pl.ds(prev_t0, BF16_ROWS), :].astype(F32)[BF16_ROWS - 1:, :]
                        * jnp.where(n > 0, 1.0, 0.0))
            next_row = (src_ref[pl.ds(next_t0, BF16_ROWS), :].astype(F32)[:1, :]
                        * jnp.where(n < n_chunks - 1, 1.0, 0.0))
            x_prev = jnp.where(sub == 0, prev_row, pltpu.roll(x, 1, 0))
            x_next = jnp.where(sub == L - 1, next_row, pltpu.roll(x, L - 1, 0))
            y = w_ref[0:1, :] * x_prev + w_ref[1:2, :] * x + w_ref[2:3, :] * x_next + b_ref[...]
            return y * jax.nn.sigmoid(y)

        def body(n, carry):
            t0 = pl.multiple_of(n * L, L)
            q_dst[pl.ds(t0, L), :] = conv_silu(q_src, cwq_ref, cbq_ref, n, t0).astype(BF16)
            k_act = conv_silu(k_src, cwk_ref, cbk_ref, n, t0) * (HEAD_DIM ** -0.5)
            kt_dst[:, pl.ds(t0, L)] = k_act.T
            va_dst[pl.ds(t0, L), 0:HEAD_DIM] = v_src[pl.ds(t0, L), :]
            va_dst[pl.ds(t0, L), HEAD_DIM:ML_AUG] = jnp.ones((L, HEAD_DIM), BF16)
            return carry

        lax.fori_loop(0, n_chunks, body, 0, unroll=2)

    prepare_stream(ql_ref, kl_ref, vl_ref, qa_l, kt_l, va_l, n_lat)
    prepare_stream(qc_ref, kc_ref, vc_ref, qa_c, kt_c, va_c, n_ctx)

    def chunk(d, n_stat, t0, q_src, kt_src, va_src, c_prev, want_out):
        base = d * N_STATS
        stat = lambda k: st[base + k, pl.ds(n_stat, 1), :]
        both = lambda x: jnp.concatenate([x, x], axis=-1)
        kt = kt_src[:, pl.ds(t0, L)]
        va = va_src[pl.ds(t0, L), :]
        out = None
        if want_out:
            q = q_src[pl.ds(t0, L), :]
            seen = (col <= row) if d == 0 else (col >= row)
            big_r = rep[2 * d + REP_R, n_stat]
            dec_mat = jnp.exp(jnp.where(seen, stat(ST_R) - big_r, NEG_BIG))
            s = _dot(q, kt.astype(BF16)) * dec_mat
            g = jnp.exp(stat(ST_MPREV) - big_r)
            acc = both(g) * _dot(q, c_prev.astype(BF16)) + _dot(s.astype(BF16), va)
            floor = jnp.exp(-rep[2 * d + REP_M, n_stat])
            out = acc[:, :HEAD_DIM] / jnp.maximum(jnp.abs(acc[:, HEAD_DIM:]), floor)
        c_loc = _dot((kt * stat(ST_W)).astype(BF16), va)
        return out, both(stat(ST_DEC)) * c_prev + both(stat(ST_INP)) * c_loc

    def finish(hsum, o_src, t0):
        y = _rms(hsum, on_ref[...]) * jax.nn.sigmoid(o_src[pl.ds(t0, L), :].astype(F32))
        return y.astype(BF16)

    zero_state = jnp.zeros((HEAD_DIM, ML_AUG), F32)

    def make_body(n_chunks, stat0, q_src, kt_src, va_src, hf_dst, hb_dst, want_out):
        def body(i, carry):
            fw, bw = carry
            nf, nb = i, n_chunks - 1 - i
            tf, tb = pl.multiple_of(nf * L, L), pl.multiple_of(nb * L, L)
            out_f, fw = chunk(0, stat0 + nf, tf, q_src, kt_src, va_src, fw, want_out)
            out_b, bw = chunk(1, stat0 + nb, tb, q_src, kt_src, va_src, bw, want_out)
            if want_out:
                hf_dst[pl.ds(tf, L), :] = out_f
                hb_dst[pl.ds(tb, L), :] = out_b
            return fw, bw
        return body

    carry = lax.fori_loop(0, n_ctx, make_body(n_ctx, n_lat, qa_c, kt_c, va_c, hf_c, hb_c, need_ctx),
                          (zero_state, zero_state))
    lax.fori_loop(0, n_lat, make_body(n_lat, 0, qa_l, kt_l, va_l, hf_l, hb_l, True), carry, unroll=4)

    def finish_lat(n, carry):
        t0 = pl.multiple_of(n * L, L)
        yl_ref[pl.ds(t0, L), :] = finish(hf_l[pl.ds(t0, L), :] + hb_l[pl.ds(t0, L), :], ol_ref, t0)
        return carry

    lax.fori_loop(0, n_lat, finish_lat, 0, unroll=4)
    if need_ctx:
        def finish_ctx(n, carry):
            t0 = pl.multiple_of(n * L, L)
            yc_ref[pl.ds(t0, L), :] = finish(hf_c[pl.ds(t0, L), :] + hb_c[pl.ds(t0, L), :], oc_ref, t0)
            return carry

        lax.fori_loop(0, n_ctx, finish_ctx, 0, unroll=True)


def _mlstm_call(p_lat, p_ctx, gates, gate_b, conv_w, conv_b, out_norm, batch, seq, ctx_len, need_ctx):
    n_lat, n_ctx = seq // ML_CHUNK, ctx_len // ML_CHUNK
    n_pad = gates.shape[2]
    cq, ck = COL_ML_Q // HEAD_DIM, COL_ML_K // HEAD_DIM
    cv, co = COL_ML_V // HEAD_DIM, COL_ML_O // HEAD_DIM
    col = lambda rows, c0: pl.BlockSpec((rows, HEAD_DIM), lambda b, h: (b, c0 + h))
    out_specs = [pl.BlockSpec((seq, HEAD_DIM), lambda b, h: (b, h))]
    out_shape = [jax.ShapeDtypeStruct((batch * seq, ML_W), BF16)]
    if need_ctx:
        out_specs.append(pl.BlockSpec((ctx_len, HEAD_DIM), lambda b, h: (b, h)))
        out_shape.append(jax.ShapeDtypeStruct((batch * ctx_len, ML_W), BF16))

    def stream_scratch(n_tok):
        return [pltpu.VMEM((n_tok, HEAD_DIM), BF16), pltpu.VMEM((HEAD_DIM, n_tok), F32),
                pltpu.VMEM((n_tok, ML_AUG), BF16)]

    res = pl.pallas_call(
        functools.partial(_mlstm_kernel, n_lat=n_lat, n_ctx=n_ctx, need_ctx=need_ctx),
        grid=(batch, ML_HEADS),
        in_specs=[col(seq, cq), col(seq, ck), col(seq, cv), col(seq, co),
                  col(ctx_len, cq), col(ctx_len, ck), col(ctx_len, cv), col(ctx_len, co),
                  pl.BlockSpec((N_GATES, None, n_pad, ML_CHUNK), lambda b, h: (0, b, 0, 0)),
                  pl.BlockSpec((N_GATES, 1, ML_CHUNK), lambda b, h: (0, 0, 0)),
                  pl.BlockSpec((3, HEAD_DIM), lambda b, h: (0, h)),
                  pl.BlockSpec((3, HEAD_DIM), lambda b, h: (0, ML_HEADS + h)),
                  pl.BlockSpec((1, HEAD_DIM), lambda b, h: (0, h)),
                  pl.BlockSpec((1, HEAD_DIM), lambda b, h: (0, ML_HEADS + h)),
                  pl.BlockSpec((1, HEAD_DIM), lambda b, h: (0, h))],
        out_specs=out_specs,
        out_shape=out_shape,
        scratch_shapes=(stream_scratch(seq) + stream_scratch(ctx_len)
                        + [pltpu.VMEM((seq, HEAD_DIM), F32), pltpu.VMEM((seq, HEAD_DIM), F32),
                           pltpu.VMEM((ctx_len, HEAD_DIM), F32), pltpu.VMEM((ctx_len, HEAD_DIM), F32),
                           pltpu.VMEM((2 * N_STATS, n_pad, ML_CHUNK), F32),
                           pltpu.VMEM((4, n_lat + n_ctx, ML_CHUNK, ML_CHUNK), F32)]),
        compiler_params=_cparams(("parallel", "parallel")),
        name="mlstm",
    )(p_lat, p_lat, p_lat, p_lat, p_ctx, p_ctx, p_ctx, p_ctx, gates, gate_b,
      conv_w, conv_w, conv_b, conv_b, out_norm)
    return (res[0], res[1]) if need_ctx else (res[0], None)


def kernel(x, c, ctx, c_ctx, w_ada, b_ada, norm_ff1, ff1_gate, ff1_up, ff1_down, norm_mix, w_in, na_rpb,
           gqa_q_norm, gqa_k_norm, ml_conv_w, ml_conv_b, ml_gate_b, ml_out_norm, w_out, norm_ff2,
           ff2_gate, ff2_up, ff2_down, final_norm):
    batch, seq, d = x.shape
    ctx_len = ctx.shape[1]
    depth = w_ada.shape[0]
    ctx_group = batch
    n_lat_chunks, n_ctx_chunks = seq // ML_CHUNK, ctx_len // ML_CHUNK
    n_chunk_pad = -(-(n_lat_chunks + n_ctx_chunks) // 8) * 8

    xl = x.reshape(batch * seq, d)
    xc = ctx.reshape(batch * ctx_len, d)
    cond = jnp.zeros((8, d), F32).at[:batch].set(c).at[batch].set(c_ctx)
    rope_tabs = _rope_tables(seq)
    row2 = lambda v: v.reshape(1, -1)
    fin = row2(final_norm)

    for l in range(depth):
        last = l == depth - 1
        mod = _ada_call(cond, w_ada[l], row2(b_ada[l])).reshape(8 * N_MOD, 1, d)
        lat = dict(group_rows=seq, fixed_group=None)
        cx = dict(group_rows=None, fixed_group=ctx_group)

        wg, wu, wd = ff1_gate[l].astype(BF16), ff1_up[l].astype(BF16), ff1_down[l].astype(BF16)
        g1 = row2(norm_ff1[l])
        xl = _ffn_call(xl, mod, 0, g1, wg, wu, wd, fin, final_norm=False, **lat)
        xc = _ffn_call(xc, mod, 0, g1, wg, wu, wd, fin, final_norm=False, **cx)

        w_main = w_in[l][:, :COL_GATES].astype(BF16)
        w_gates_t = w_in[l][:, COL_GATES:].T.astype(BF16)
        gm = row2(norm_mix[l])
        p_lat, gt_lat = _inproj_call(xl, mod, gm, w_main, w_gates_t, **lat)
        p_ctx, gt_ctx = _inproj_call(xc, mod, gm, w_main, w_gates_t, **cx)

        ya_lat = _na_call(p_lat, p_ctx, *_na_toeplitz_slabs(na_rpb[l]), batch, seq, ctx_len)
        qg, kg = row2(gqa_q_norm[l]), row2(gqa_k_norm[l])
        qn_lat, kn_lat = _gqa_prep_call(p_lat, qg, kg, rope_tabs, seq, rope=True)
        qn_ctx, kn_ctx = _gqa_prep_call(p_ctx, qg, kg, rope_tabs, seq, rope=False, tm=ctx_len)
        yb_lat = _gqa_call(qn_lat, kn_lat, p_lat, kn_ctx, p_ctx, batch, seq, ctx_len)
        gates = jnp.concatenate(
            [gt_lat.reshape(N_GATES, batch, n_lat_chunks, ML_CHUNK),
             gt_ctx.reshape(N_GATES, batch, n_ctx_chunks, ML_CHUNK),
             jnp.zeros((N_GATES, batch, n_chunk_pad - n_lat_chunks - n_ctx_chunks, ML_CHUNK), F32)], axis=2)
        gate_b = jnp.broadcast_to(ml_gate_b[l][:, None, None], (N_GATES, 1, ML_CHUNK))
        yc_lat, yc_ctx = _mlstm_call(p_lat, p_ctx, gates, gate_b, ml_conv_w[l], row2(ml_conv_b[l]),
                                     row2(ml_out_norm[l]), batch, seq, ctx_len, not last)

        wo = w_out[l].astype(BF16)
        xl = _outproj_call(xl, mod, ya_lat, yb_lat, yc_lat, wo, **lat)
        wg, wu, wd = ff2_gate[l].astype(BF16), ff2_up[l].astype(BF16), ff2_down[l].astype(BF16)
        g2 = row2(norm_ff2[l])
        xl = _ffn_call(xl, mod, 6, g2, wg, wu, wd, fin, final_norm=last, **lat)
        if not last:
            ya_ctx = _ctx_attn_call(p_ctx, p_ctx, p_ctx, batch, ctx_len, NA_HEADS,
                                    COL_NA_Q // HEAD_DIM, COL_NA_K // HEAD_DIM, COL_NA_V // HEAD_DIM, 1,
                                    HEAD_DIM ** -0.5)
            yb_ctx = _ctx_attn_call(qn_ctx, kn_ctx, p_ctx, batch, ctx_len, GQA_Q_HEADS, 0, 0,
                                    COL_GQ_V // HEAD_DIM, GQA_Q_HEADS // GQA_KV_HEADS, None)
            xc = _outproj_call(xc, mod, ya_ctx, yb_ctx, yc_ctx, wo, **cx)
            xc = _ffn_call(xc, mod, 6, g2, wg, wu, wd, fin, final_norm=False, **cx)
    return xl.reshape(batch, seq, d)
```

```python
import functools

import numpy as np
import jax
import jax.numpy as jnp
from jax import lax
from jax.experimental import pallas as pl
from jax.experimental.pallas import tpu as pltpu

F32 = jnp.float32
BF16 = jnp.bfloat16

GRID_W = 64
HEAD_DIM = 128
NA_HEADS = 4
GQA_Q_HEADS = 4
GQA_KV_HEADS = 2
ML_HEADS = 8
NA_W = NA_HEADS * HEAD_DIM
GQA_W = GQA_Q_HEADS * HEAD_DIM
KV_W = GQA_KV_HEADS * HEAD_DIM
ML_W = ML_HEADS * HEAD_DIM
NA_WIN_R = 8
NA_WIN_C = 16
ML_CHUNK = 128
ROPE_THETA = 10000.0
N_MOD = 9
EPS = 1e-6
N_GATES = 4 * ML_HEADS
COL_NA_Q = 0
COL_NA_K = NA_W
COL_NA_V = 2 * NA_W
COL_GQ_Q = 3 * NA_W
COL_GQ_K = COL_GQ_Q + GQA_W
COL_GQ_V = COL_GQ_K + KV_W
COL_ML_Q = COL_GQ_V + KV_W
COL_ML_K = COL_ML_Q + ML_W
COL_ML_V = COL_ML_K + ML_W
COL_ML_O = COL_ML_V + ML_W
COL_GATES = COL_ML_O + ML_W

NEG_BIG = -1e30
VMEM_LIMIT_V7X = 56 * 1024 * 1024
BF16_ROWS = 16
NORM_ROWS = 32
FFN_SUBTILE = 256

NA_QROWS = 8
NA_KROWS = 16
NA_MASKED_SLAB = 2 * NA_WIN_R - 1


def _cparams(sem):
    return pltpu.CompilerParams(dimension_semantics=sem, vmem_limit_bytes=VMEM_LIMIT_V7X)


def _rms(x, gain):
    return x * lax.rsqrt(jnp.mean(x * x, axis=-1, keepdims=True) + EPS) * gain


def _norm_modulate_store(x_ref, gain_ref, scale_ref, shift_ref, h_ref):
    gain_scale = gain_ref[...] * (1.0 + scale_ref[...])
    shift = shift_ref[...]

    def body(i, carry):
        r0 = pl.multiple_of(i * NORM_ROWS, NORM_ROWS)
        x = x_ref[pl.ds(r0, NORM_ROWS), :]
        inv = lax.rsqrt(jnp.mean(x * x, axis=-1, keepdims=True) + EPS)
        h_ref[pl.ds(r0, NORM_ROWS), :] = (x * inv * gain_scale + shift).astype(h_ref.dtype)
        return carry

    lax.fori_loop(0, x_ref.shape[0] // NORM_ROWS, body, 0, unroll=2)


def _dot(a, b):
    return jnp.dot(a, b, preferred_element_type=F32)


def _dot_nt(a, b):
    return lax.dot_general(a, b, (((1,), (1,)), ((), ())), preferred_element_type=F32)


def _dot_tn(a, b):
    return lax.dot_general(a, b, (((0,), (0,)), ((), ())), preferred_element_type=F32)


def _cast_kernel(w_ref, o_ref):
    o_ref[...] = w_ref[...].astype(o_ref.dtype)


def _cast_call(w_stack, layer, n_cols, row_blocks=8):
    _, rows, _ = w_stack.shape
    tr = rows // row_blocks
    return pl.pallas_call(
        _cast_kernel,
        grid=(row_blocks,),
        in_specs=[pl.BlockSpec((None, tr, n_cols), lambda i: (layer, i, 0))],
        out_specs=pl.BlockSpec((tr, n_cols), lambda i: (i, 0)),
        out_shape=jax.ShapeDtypeStruct((rows, n_cols), BF16),
        compiler_params=_cparams(("parallel",)),
        name="weight_cast",
    )(w_stack)


def _ada_kernel(c_ref, w_ref, b_ref, o_ref):
    c = c_ref[...]
    a = (c * jax.nn.sigmoid(c)).astype(BF16)
    o_ref[...] = _dot(a, w_ref[...].astype(BF16)) + b_ref[...]


def _ada_call(cond, w, b, tn=1024):
    m, d = cond.shape
    n = w.shape[1]
    return pl.pallas_call(
        _ada_kernel,
        grid=(n // tn,),
        in_specs=[pl.BlockSpec((m, d), lambda j: (0, 0)),
                  pl.BlockSpec((d, tn), lambda j: (0, j)),
                  pl.BlockSpec((1, tn), lambda j: (0, j))],
        out_specs=pl.BlockSpec((m, tn), lambda j: (0, j)),
        out_shape=jax.ShapeDtypeStruct((m, n), F32),
        compiler_params=_cparams(("arbitrary",)),
        name="ada_mod",
    )(cond, w, b)


def _mod_spec(k, tm, group_rows, fixed_group):
    def index_map(i, *_):
        grp = fixed_group if fixed_group is not None else i // (group_rows // tm)
        return (grp * N_MOD + k, 0, 0)
    return index_map


def _ffn_kernel(x_ref, sh_ref, sc_ref, gt_ref, gain_ref, wg_ref, wu_ref, wd_ref, fin_ref, o_ref,
                h_scr, acc_scr, *, n_ff, final_norm):
    j = pl.program_id(1)

    @pl.when(j == 0)
    def _():
        _norm_modulate_store(x_ref, gain_ref, sc_ref, sh_ref, h_scr)
        acc_scr[...] = jnp.zeros_like(acc_scr)

    h = h_scr[...]
    tf = wg_ref.shape[1]
    part = None
    for c0 in range(0, tf, FFN_SUBTILE):
        g = _dot(h, wg_ref[:, c0:c0 + FFN_SUBTILE])
        u = _dot(h, wu_ref[:, c0:c0 + FFN_SUBTILE])
        a = (g * jax.nn.sigmoid(g)) * u
        d = _dot(a.astype(BF16), wd_ref[c0:c0 + FFN_SUBTILE, :])
        part = d if part is None else part + d
    acc_scr[...] += part

    @pl.when(j == n_ff - 1)
    def _():
        out = x_ref[...] + 0.5 * gt_ref[...] * acc_scr[...]
        if final_norm:
            out = _rms(out, fin_ref[...])
        o_ref[...] = out


def _ffn_call(x, mod, k0, gain, wg, wu, wd, fin, *, group_rows, fixed_group, final_norm,
              tm=512, tf=512):
    t, d = x.shape
    dff = wg.shape[1]
    n_ff = dff // tf
    mspec = lambda k: pl.BlockSpec((None, 1, d), _mod_spec(k, tm, group_rows, fixed_group))
    return pl.pallas_call(
        functools.partial(_ffn_kernel, n_ff=n_ff, final_norm=final_norm),
        grid=(t // tm, n_ff),
        in_specs=[pl.BlockSpec((tm, d), lambda i, j: (i, 0)),
                  mspec(k0), mspec(k0 + 1), mspec(k0 + 2),
                  pl.BlockSpec((1, d), lambda i, j: (0, 0)),
                  pl.BlockSpec((d, tf), lambda i, j: (0, j)),
                  pl.BlockSpec((d, tf), lambda i, j: (0, j)),
                  pl.BlockSpec((tf, d), lambda i, j: (j, 0)),
                  pl.BlockSpec((1, d), lambda i, j: (0, 0))],
        out_specs=pl.BlockSpec((tm, d), lambda i, j: (i, 0)),
        out_shape=jax.ShapeDtypeStruct((t, d), F32),
        scratch_shapes=[pltpu.VMEM((tm, d), BF16), pltpu.VMEM((tm, d), F32)],
        compiler_params=_cparams(("parallel", "arbitrary")),
        name="macaron_ffn",
    )(x, mod, mod, mod, gain, wg, wu, wd, fin)


def _inproj_kernel(x_ref, sh_ref, sc_ref, gain_ref, w_ref, wgt_ref, p_ref, gt_ref, h_scr):
    j = pl.program_id(1)

    @pl.when(j == 0)
    def _():
        _norm_modulate_store(x_ref, gain_ref, sc_ref, sh_ref, h_scr)
        gt_ref[...] = _dot_nt(wgt_ref[...], h_scr[...])

    p_ref[...] = _dot(h_scr[...], w_ref[...]).astype(p_ref.dtype)


def _inproj_call(x, mod, gain, w_main, w_gates_t, *, group_rows, fixed_group, tm=512, tn=1664):
    t, d = x.shape
    n = w_main.shape[1]
    mspec = lambda k: pl.BlockSpec((None, 1, d), _mod_spec(k, tm, group_rows, fixed_group))
    return pl.pallas_call(
        _inproj_kernel,
        grid=(t // tm, n // tn),
        in_specs=[pl.BlockSpec((tm, d), lambda i, j: (i, 0)),
                  mspec(3), mspec(4),
                  pl.BlockSpec((1, d), lambda i, j: (0, 0)),
                  pl.BlockSpec((d, tn), lambda i, j: (0, j)),
                  pl.BlockSpec((N_GATES, d), lambda i, j: (0, 0))],
        out_specs=[pl.BlockSpec((tm, tn), lambda i, j: (i, j)),
                   pl.BlockSpec((N_GATES, tm), lambda i, j: (0, i))],
        out_shape=[jax.ShapeDtypeStruct((t, n), BF16),
                   jax.ShapeDtypeStruct((N_GATES, t), F32)],
        scratch_shapes=[pltpu.VMEM((tm, d), BF16)],
        compiler_params=_cparams(("parallel", "arbitrary")),
        name="mixer_in_proj",
    )(x, mod, mod, gain, w_main, w_gates_t)


def _outproj_kernel(x_ref, gt_ref, ya_ref, yb_ref, yc_ref, w_ref, o_ref):
    acc = _dot(ya_ref[...], w_ref[0:NA_W, :])
    acc += _dot(yb_ref[...], w_ref[NA_W:NA_W + GQA_W, :])
    acc += _dot(yc_ref[...], w_ref[NA_W + GQA_W:, :])
    o_ref[...] = x_ref[...] + gt_ref[...] * acc


def _outproj_call(x, mod, ya, yb, yc, w, *, group_rows, fixed_group, tm=512):
    t, d = x.shape
    return pl.pallas_call(
        _outproj_kernel,
        grid=(t // tm,),
        in_specs=[pl.BlockSpec((tm, d), lambda i: (i, 0)),
                  pl.BlockSpec((None, 1, d), _mod_spec(5, tm, group_rows, fixed_group)),
                  pl.BlockSpec((tm, NA_W), lambda i: (i, 0)),
                  pl.BlockSpec((tm, GQA_W), lambda i: (i, 0)),
                  pl.BlockSpec((tm, ML_W), lambda i: (i, 0)),
                  pl.BlockSpec((d, d), lambda i: (0, 0))],
        out_specs=pl.BlockSpec((tm, d), lambda i: (i, 0)),
        out_shape=jax.ShapeDtypeStruct((t, d), F32),
        compiler_params=_cparams(("parallel",)),
        name="mixer_out_proj",
    )(x, mod, ya, yb, yc, w)


def _na_kernel(q_ref, k_ref, v_ref, kc_ref, vc_ref, tl_ref, tr_ref, o_ref, mb_scr, *, rows):
    rb = pl.program_id(2)
    nblk = rows // NA_QROWS
    key_row0 = jnp.clip(rb * NA_QROWS - NA_WIN_R // 2, 0, rows - NA_KROWS)

    @pl.when((rb <= 1) | (rb == nblk - 1))
    def _():
        for a in range(NA_QROWS):
            r = rb * NA_QROWS + a
            row_start = jnp.clip(r - NA_WIN_R // 2, 0, rows - NA_WIN_R)
            for j in range(NA_KROWS // 2):
                sel = []
                for kr in (key_row0 + 2 * j, key_row0 + 2 * j + 1):
                    in_win = (kr >= row_start) & (kr < row_start + NA_WIN_R)
                    sel.append(jnp.where(in_win, kr - r + (NA_WIN_R - 1), NA_MASKED_SLAB))
                mb_scr[a * GRID_W:(a + 1) * GRID_W, 2 * j * GRID_W:(2 * j + 2) * GRID_W] = (
                    tl_ref[sel[0]] + tr_ref[sel[1]])

    start = pl.multiple_of(key_row0 * GRID_W, GRID_W)
    nk = NA_KROWS * GRID_W
    scale = HEAD_DIM ** -0.5
    q = q_ref[...]
    k = k_ref[pl.ds(start, nk), :]
    v = v_ref[pl.ds(start, nk), :]
    s_loc = _dot_nt(q, k) * scale + mb_scr[...]
    s_ctx = _dot_nt(q, kc_ref[...]) * scale
    m = jnp.maximum(jnp.max(s_loc, axis=-1, keepdims=True), jnp.max(s_ctx, axis=-1, keepdims=True))
    p_loc = jnp.exp(s_loc - m)
    p_ctx = jnp.exp(s_ctx - m)
    den = jnp.sum(p_loc, axis=-1, keepdims=True) + jnp.sum(p_ctx, axis=-1, keepdims=True)
    out = _dot(p_loc.astype(BF16), v) + _dot(p_ctx.astype(BF16), vc_ref[...])
    o_ref[...] = (out / den).astype(o_ref.dtype)


def _na_toeplitz_slabs(rpb):
    n_h, n_dr, n_dc = rpb.shape
    edge = GRID_W - NA_WIN_C
    ext = jnp.concatenate(
        [rpb[..., NA_WIN_C - 1:], jnp.broadcast_to(rpb[..., n_dc - 1:], (n_h, n_dr, edge)),
         jnp.zeros((n_h, n_dr, 1), F32),
         jnp.broadcast_to(rpb[..., :1], (n_h, n_dr, edge)), rpb[..., :NA_WIN_C - 1]], axis=-1)
    width = 2 * GRID_W
    tiled = jnp.broadcast_to(ext[:, :, None, :], (n_h, n_dr, GRID_W, width)).reshape(n_h, n_dr, GRID_W * width)
    toep = tiled[..., :GRID_W * (width - 1)].reshape(n_h, n_dr, GRID_W, width - 1)[..., :GRID_W]
    cq = np.arange(GRID_W)
    col_start = np.clip(cq - NA_WIN_C // 2, 0, GRID_W - NA_WIN_C)
    col_ok = (cq[None, :] >= col_start[:, None]) & (cq[None, :] < col_start[:, None] + NA_WIN_C)
    slabs = jnp.where(col_ok, toep, NEG_BIG)
    slabs = jnp.concatenate([slabs, jnp.full((n_h, 1, GRID_W, GRID_W), NEG_BIG, F32)], axis=1)
    zeros = jnp.zeros_like(slabs)
    return jnp.concatenate([slabs, zeros], axis=-1), jnp.concatenate([zeros, slabs], axis=-1)


def _na_call(p_lat, p_ctx, slabs_left, slabs_right, batch, seq, ctx_len):
    rows = seq // GRID_W
    nblk = rows // NA_QROWS
    tq = NA_QROWS * GRID_W
    nk = NA_KROWS * GRID_W
    cq, ck, cv = COL_NA_Q // HEAD_DIM, COL_NA_K // HEAD_DIM, COL_NA_V // HEAD_DIM
    slab_spec = pl.BlockSpec((None, NA_MASKED_SLAB + 1, GRID_W, 2 * GRID_W), lambda b, h, r: (h, 0, 0, 0))

    return pl.pallas_call(
        functools.partial(_na_kernel, rows=rows),
        grid=(batch, NA_HEADS, nblk),
        in_specs=[pl.BlockSpec((tq, HEAD_DIM), lambda b, h, r: (b * nblk + r, cq + h)),
                  pl.BlockSpec((seq, HEAD_DIM), lambda b, h, r: (b, ck + h)),
                  pl.BlockSpec((seq, HEAD_DIM), lambda b, h, r: (b, cv + h)),
                  pl.BlockSpec((ctx_len, HEAD_DIM), lambda b, h, r: (b, ck + h)),
                  pl.BlockSpec((ctx_len, HEAD_DIM), lambda b, h, r: (b, cv + h)),
                  slab_spec, slab_spec],
        out_specs=pl.BlockSpec((tq, HEAD_DIM), lambda b, h, r: (b * nblk + r, h)),
        out_shape=jax.ShapeDtypeStruct((batch * seq, NA_W), BF16),
        scratch_shapes=[pltpu.VMEM((tq, nk), F32)],
        compiler_params=_cparams(("parallel", "parallel", "arbitrary")),
        name="neighbourhood_attn",
    )(p_lat, p_lat, p_lat, p_ctx, p_ctx, slabs_left, slabs_right)


def _ctx_attn_kernel(q_ref, k_ref, v_ref, o_ref, *, scale):
    s = _dot_nt(q_ref[...], k_ref[...])
    if scale is not None:
        s = s * scale
    m = jnp.max(s, axis=-1, keepdims=True)
    p = jnp.exp(s - m)
    den = jnp.sum(p, axis=-1, keepdims=True)
    o_ref[...] = (_dot(p.astype(BF16), v_ref[...]) / den).astype(o_ref.dtype)


def _ctx_attn_call(q_arr, k_arr, v_arr, batch, ctx_len, n_heads, q_col, k_col, v_col, kv_group, scale):
    return pl.pallas_call(
        functools.partial(_ctx_attn_kernel, scale=scale),
        grid=(batch, n_heads),
        in_specs=[pl.BlockSpec((ctx_len, HEAD_DIM), lambda b, h: (b, q_col + h)),
                  pl.BlockSpec((ctx_len, HEAD_DIM), lambda b, h: (b, k_col + h // kv_group)),
                  pl.BlockSpec((ctx_len, HEAD_DIM), lambda b, h: (b, v_col + h // kv_group))],
        out_specs=pl.BlockSpec((ctx_len, HEAD_DIM), lambda b, h: (b, h)),
        out_shape=jax.ShapeDtypeStruct((batch * ctx_len, n_heads * HEAD_DIM), BF16),
        compiler_params=_cparams(("parallel", "parallel")),
        name="ctx_attn",
    )(q_arr, k_arr, v_arr)


def _rope(x, cos, sin_lo, sin_hi):
    return (x * cos + pltpu.roll(x, HEAD_DIM - HEAD_DIM // 4, 1) * sin_lo
            + pltpu.roll(x, HEAD_DIM // 4, 1) * sin_hi)


def _gqa_prep_kernel(q_ref, k_ref, qg_ref, kg_ref, cos_ref, slo_ref, shi_ref, qo_ref, ko_ref, *, rope):
    def prep(x, gain, scale):
        y = _rms(x.astype(F32), gain)
        if rope:
            y = _rope(y, cos_ref[...], slo_ref[...], shi_ref[...])
        if scale is not None:
            y = y * scale
        return y.astype(BF16)

    for h in range(GQA_Q_HEADS):
        sl = slice(h * HEAD_DIM, (h + 1) * HEAD_DIM)
        qo_ref[:, sl] = prep(q_ref[:, sl], qg_ref[...], HEAD_DIM ** -0.5)
    for h in range(GQA_KV_HEADS):
        sl = slice(h * HEAD_DIM, (h + 1) * HEAD_DIM)
        ko_ref[:, sl] = prep(k_ref[:, sl], kg_ref[...], None)


def _rope_tables(n_tokens):
    t = jnp.arange(n_tokens)
    row = (t // GRID_W).astype(F32)
    col = (t % GRID_W).astype(F32)
    half = HEAD_DIM // 2
    inv_freq = 1.0 / (ROPE_THETA ** (jnp.arange(0, half, 2, dtype=F32) / half))
    ang_r = row[:, None] * inv_freq[None, :]
    ang_c = col[:, None] * inv_freq[None, :]
    ang = jnp.concatenate([ang_r, ang_r, ang_c, ang_c], axis=-1)
    cos, sin = jnp.cos(ang), jnp.sin(ang)
    lo = (jnp.arange(HEAD_DIM) % half) < (half // 2)
    return cos, jnp.where(lo, -sin, 0.0), jnp.where(lo, 0.0, sin)


def _gqa_prep_call(p, q_gain, k_gain, tables, seq, *, rope, tm=512):
    t = p.shape[0]
    per_seq = seq // tm if rope else 1
    tab_spec = pl.BlockSpec((tm, HEAD_DIM), lambda i: (i % per_seq, 0))
    return pl.pallas_call(
        functools.partial(_gqa_prep_kernel, rope=rope),
        grid=(t // tm,),
        in_specs=[pl.BlockSpec((tm, GQA_W), lambda i: (i, COL_GQ_Q // GQA_W)),
                  pl.BlockSpec((tm, KV_W), lambda i: (i, COL_GQ_K // KV_W)),
                  pl.BlockSpec((1, HEAD_DIM), lambda i: (0, 0)),
                  pl.BlockSpec((1, HEAD_DIM), lambda i: (0, 0)),
                  tab_spec, tab_spec, tab_spec],
        out_specs=[pl.BlockSpec((tm, GQA_W), lambda i: (i, 0)),
                   pl.BlockSpec((tm, KV_W), lambda i: (i, 0))],
        out_shape=[jax.ShapeDtypeStruct((t, GQA_W), BF16),
                   jax.ShapeDtypeStruct((t, KV_W), BF16)],
        compiler_params=_cparams(("parallel",)),
        name="gqa_prep",
    )(p, p, q_gain, k_gain, *tables)


def _gqa_kernel(q_ref, kl_ref, vl_ref, kc_ref, vc_ref, o_ref, *, tk):
    tq = q_ref.shape[0]
    grp = q_ref.shape[1] // HEAD_DIM
    q = jnp.concatenate([q_ref[:, g * HEAD_DIM:(g + 1) * HEAD_DIM] for g in range(grp)], axis=0)

    s = _dot_nt(q, kc_ref[...])
    m = jnp.max(s, axis=-1, keepdims=True)
    p = jnp.exp(s - m)
    den = jnp.sum(p, axis=-1, keepdims=True)
    acc = _dot(p.astype(BF16), vc_ref[...])

    def tile(i, carry):
        m, den, acc = carry
        t0 = pl.multiple_of(i * tk, tk)
        s = _dot_nt(q, kl_ref[pl.ds(t0, tk), :])
        m_new = jnp.maximum(m, jnp.max(s, axis=-1, keepdims=True))
        p = jnp.exp(s - m_new)
        alpha = jnp.exp(m - m_new)
        den = alpha * den + jnp.sum(p, axis=-1, keepdims=True)
        acc = alpha * acc + _dot(p.astype(BF16), vl_ref[pl.ds(t0, tk), :])
        return m_new, den, acc

    m, den, acc = lax.fori_loop(0, kl_ref.shape[0] // tk, tile, (m, den, acc), unroll=True)
    out = (acc / den).astype(o_ref.dtype)
    for g in range(grp):
        o_ref[:, g * HEAD_DIM:(g + 1) * HEAD_DIM] = out[g * tq:(g + 1) * tq]


def _gqa_call(q_lat, k_lat, p_lat, k_ctx, p_ctx, batch, seq, ctx_len, tq=256, tk=512):
    grp = GQA_Q_HEADS // GQA_KV_HEADS
    nq = seq // tq
    cv = COL_GQ_V // HEAD_DIM
    return pl.pallas_call(
        functools.partial(_gqa_kernel, tk=tk),
        grid=(batch, GQA_KV_HEADS, nq),
        in_specs=[pl.BlockSpec((tq, grp * HEAD_DIM), lambda b, h, i: (b * nq + i, h)),
                  pl.BlockSpec((seq, HEAD_DIM), lambda b, h, i: (b, h)),
                  pl.BlockSpec((seq, HEAD_DIM), lambda b, h, i: (b, cv + h)),
                  pl.BlockSpec((ctx_len, HEAD_DIM), lambda b, h, i: (b, h)),
                  pl.BlockSpec((ctx_len, HEAD_DIM), lambda b, h, i: (b, cv + h))],
        out_specs=pl.BlockSpec((tq, grp * HEAD_DIM), lambda b, h, i: (b * nq + i, h)),
        out_shape=jax.ShapeDtypeStruct((batch * seq, GQA_W), BF16),
        compiler_params=_cparams(("parallel", "parallel", "arbitrary")),
        name="gqa_attn",
    )(q_lat, k_lat, p_lat, k_ctx, p_ctx)


ST_R, ST_W, ST_BTOT, ST_AMAX, ST_MPREV, ST_DEC, ST_INP, ST_BIGR, ST_M = range(9)
N_STATS = 9
REP_R, REP_M = range(2)
ML_AUG = 2 * HEAD_DIM


def _mlstm_kernel(ql_ref, kl_ref, vl_ref, ol_ref, qc_ref, kc_ref, vc_ref, oc_ref, g_ref, gb_ref,
                  cwq_ref, cwk_ref, cbq_ref, cbk_ref, on_ref, yl_ref, *rest,
                  n_lat, n_ctx, need_ctx):
    if need_ctx:
        yc_ref = rest[0]
        rest = rest[1:]
    qa_l, kt_l, va_l, qa_c, kt_c, va_c, hf_l, hb_l, hf_c, hb_c, st, rep = rest
    L = ML_CHUNK
    n_all = n_lat + n_ctx
    h = pl.program_id(1)
    row = lax.broadcasted_iota(jnp.int32, (L, L), 0)
    col = lax.broadcasted_iota(jnp.int32, (L, L), 1)

    tri_fw = (row <= col).astype(F32)
    tri_bw = (row >= col).astype(F32)
    b_cum = []
    for d, tri in enumerate((tri_fw, tri_bw)):
        log_i = g_ref[2 * d * ML_HEADS + h] + gb_ref[2 * d * ML_HEADS + h]
        log_f = jax.nn.log_sigmoid(g_ref[(2 * d + 1) * ML_HEADS + h] + gb_ref[(2 * d + 1) * ML_HEADS + h])
        b = jnp.dot(log_f, tri, preferred_element_type=F32, precision=lax.Precision.HIGHEST)
        b_tot = b[:, L - 1:L] if d == 0 else b[:, 0:1]
        a = b_tot - b + log_i
        a_max = jnp.max(a, axis=-1, keepdims=True)
        r = log_i - b
        lane = lax.broadcasted_iota(jnp.int32, r.shape, 1)
        run = r
        shift = 1
        while shift < L:
            if d == 0:
                run = jnp.maximum(run, jnp.where(lane >= shift, pltpu.roll(run, shift, 1), NEG_BIG))
            else:
                run = jnp.maximum(run, jnp.where(lane < L - shift, pltpu.roll(run, L - shift, 1), NEG_BIG))
            shift *= 2
        base = d * N_STATS
        st[base + ST_R] = r
        st[base + ST_W] = jnp.exp(a - a_max)
        st[base + ST_BTOT] = jnp.broadcast_to(b_tot, b.shape)
        st[base + ST_AMAX] = jnp.broadcast_to(a_max, b.shape)
        st[base + ST_BIGR] = run
        st[base + ST_MPREV] = jnp.zeros_like(b)
        b_cum.append(b)

    def stabiliser_step(i, carry):
        new = []
        for d, m_prev in enumerate(carry):
            in_ctx = i < n_ctx
            j = jnp.where(in_ctx, i, i - n_ctx)
            if d == 0:
                n = jnp.where(in_ctx, n_lat + j, j)
            else:
                n = jnp.where(in_ctx, n_lat + n_ctx - 1 - j, n_lat - 1 - j)
            base = d * N_STATS
            b_tot = st[base + ST_BTOT, pl.ds(n, 1), :]
            a_max = st[base + ST_AMAX, pl.ds(n, 1), :]
            m_new = jnp.maximum(b_tot + m_prev, a_max)
            st[base + ST_MPREV, pl.ds(n, 1), :] = m_prev
            st[base + ST_DEC, pl.ds(n, 1), :] = jnp.exp(b_tot + m_prev - m_new)
            st[base + ST_INP, pl.ds(n, 1), :] = jnp.exp(a_max - m_new)
            new.append(m_new)
        return tuple(new)

    lax.fori_loop(0, n_all, stabiliser_step, (jnp.zeros((1, L), F32), jnp.zeros((1, L), F32)))

    for d in range(2):
        base = d * N_STATS
        big_r = jnp.maximum(st[base + ST_BIGR], st[base + ST_MPREV])
        st[base + ST_BIGR] = big_r
        st[base + ST_M] = b_cum[d] + big_r

    def column_tables(n, carry):
        for d in range(2):
            for k_rep, k_st in ((REP_R, ST_BIGR), (REP_M, ST_M)):
                rows = jnp.broadcast_to(st[d * N_STATS + k_st, pl.ds(n, 1), :], (L, L))
                rep[2 * d + k_rep, n] = rows.T
        return carry

    lax.fori_loop(0, n_all, column_tables, 0, unroll=2)

    def prepare_stream(q_src, k_src, v_src, q_dst, kt_dst, va_dst, n_chunks):
        n_tok = n_chunks * L
        sub = lax.broadcasted_iota(jnp.int32, (L, HEAD_DIM), 0)

        def conv_silu(src_ref, w_ref, b_ref, n, t0):
            x = src_ref[pl.ds(t0, L), :].astype(F32)
            prev_t0 = pl.multiple_of(jnp.maximum(t0 - BF16_ROWS, 0), BF16_ROWS)
            next_t0 = pl.multiple_of(jnp.minimum(t0 + L, n_tok - BF16_ROWS), BF16_ROWS)
            prev_row = (src_ref[pl.ds(prev_t0, BF16_ROWS), :].astype(F32)[BF16_ROWS - 1:, :]
                        * jnp.where(n > 0, 1.0, 0.0))
            next_row = (src_ref[pl.ds(next_t0, BF16_ROWS), :].astype(F32)[:1, :]
                        * jnp.where(n < n_chunks - 1, 1.0, 0.0))
            x_prev = jnp.where(sub == 0, prev_row, pltpu.roll(x, 1, 0))
            x_next = jnp.where(sub == L - 1, next_row, pltpu.roll(x, L - 1, 0))
            y = w_ref[0:1, :] * x_prev + w_ref[1:2, :] * x + w_ref[2:3, :] * x_next + b_ref[...]
            return y * jax.nn.sigmoid(y)

        def body(n, carry):
            t0 = pl.multiple_of(n * L, L)
            q_dst[pl.ds(t0, L), :] = conv_silu(q_src, cwq_ref, cbq_ref, n, t0).astype(BF16)
            k_act = conv_silu(k_src, cwk_ref, cbk_ref, n, t0) * (HEAD_DIM ** -0.5)
            kt_dst[:, pl.ds(t0, L)] = k_act.T
            va_dst[pl.ds(t0, L), 0:HEAD_DIM] = v_src[pl.ds(t0, L), :]
            va_dst[pl.ds(t0, L), HEAD_DIM:ML_AUG] = jnp.ones((L, HEAD_DIM), BF16)
            return carry

        lax.fori_loop(0, n_chunks, body, 0, unroll=2)

    prepare_stream(ql_ref, kl_ref, vl_ref, qa_l, kt_l, va_l, n_lat)
    prepare_stream(qc_ref, kc_ref, vc_ref, qa_c, kt_c, va_c, n_ctx)

    def chunk(d, n_stat, t0, q_src, kt_src, va_src, c_prev, want_out):
        base = d * N_STATS
        stat = lambda k: st[base + k, pl.ds(n_stat, 1), :]
        both = lambda x: jnp.concatenate([x, x], axis=-1)
        kt = kt_src[:, pl.ds(t0, L)]
        va = va_src[pl.ds(t0, L), :]
        out = None
        if want_out:
            q = q_src[pl.ds(t0, L), :]
            seen = (col <= row) if d == 0 else (col >= row)
            big_r = rep[2 * d + REP_R, n_stat]
            dec_mat = jnp.exp(jnp.where(seen, stat(ST_R) - big_r, NEG_BIG))
            s = _dot(q, kt.astype(BF16)) * dec_mat
            g = jnp.exp(stat(ST_MPREV) - big_r)
            acc = both(g) * _dot(q, c_prev.astype(BF16)) + _dot(s.astype(BF16), va)
            floor = jnp.exp(-rep[2 * d + REP_M, n_stat])
            out = acc[:, :HEAD_DIM] / jnp.maximum(jnp.abs(acc[:, HEAD_DIM:]), floor)
        c_loc = _dot((kt * stat(ST_W)).astype(BF16), va)
        return out, both(stat(ST_DEC)) * c_prev + both(stat(ST_INP)) * c_loc

    def finish(hsum, o_src, t0):
        y = _rms(hsum, on_ref[...]) * jax.nn.sigmoid(o_src[pl.ds(t0, L), :].astype(F32))
        return y.astype(BF16)

    zero_state = jnp.zeros((HEAD_DIM, ML_AUG), F32)

    def make_body(n_chunks, stat0, q_src, kt_src, va_src, hf_dst, hb_dst, want_out):
        def body(i, carry):
            fw, bw = carry
            nf, nb = i, n_chunks - 1 - i
            tf, tb = pl.multiple_of(nf * L, L), pl.multiple_of(nb * L, L)
            out_f, fw = chunk(0, stat0 + nf, tf, q_src, kt_src, va_src, fw, want_out)
            out_b, bw = chunk(1, stat0 + nb, tb, q_src, kt_src, va_src, bw, want_out)
            if want_out:
                hf_dst[pl.ds(tf, L), :] = out_f
                hb_dst[pl.ds(tb, L), :] = out_b
            return fw, bw
        return body

    carry = lax.fori_loop(0, n_ctx, make_body(n_ctx, n_lat, qa_c, kt_c, va_c, hf_c, hb_c, need_ctx),
                          (zero_state, zero_state))
    lax.fori_loop(0, n_lat, make_body(n_lat, 0, qa_l, kt_l, va_l, hf_l, hb_l, True), carry, unroll=4)

    def finish_lat(n, carry):
        t0 = pl.multiple_of(n * L, L)
        yl_ref[pl.ds(t0, L), :] = finish(hf_l[pl.ds(t0, L), :] + hb_l[pl.ds(t0, L), :], ol_ref, t0)
        return carry

    lax.fori_loop(0, n_lat, finish_lat, 0, unroll=4)
    if need_ctx:
        def finish_ctx(n, carry):
            t0 = pl.multiple_of(n * L, L)
            yc_ref[pl.ds(t0, L), :] = finish(hf_c[pl.ds(t0, L), :] + hb_c[pl.ds(t0, L), :], oc_ref, t0)
            return carry

        lax.fori_loop(0, n_ctx, finish_ctx, 0, unroll=True)


def _mlstm_call(p_lat, p_ctx, gates, gate_b, conv_w, conv_b, out_norm, batch, seq, ctx_len, need_ctx):
    n_lat, n_ctx = seq // ML_CHUNK, ctx_len // ML_CHUNK
    n_pad = gates.shape[2]
    cq, ck = COL_ML_Q // HEAD_DIM, COL_ML_K // HEAD_DIM
    cv, co = COL_ML_V // HEAD_DIM, COL_ML_O // HEAD_DIM
    col = lambda rows, c0: pl.BlockSpec((rows, HEAD_DIM), lambda b, h: (b, c0 + h))
    out_specs = [pl.BlockSpec((seq, HEAD_DIM), lambda b, h: (b, h))]
    out_shape = [jax.ShapeDtypeStruct((batch * seq, ML_W), BF16)]
    if need_ctx:
        out_specs.append(pl.BlockSpec((ctx_len, HEAD_DIM), lambda b, h: (b, h)))
        out_shape.append(jax.ShapeDtypeStruct((batch * ctx_len, ML_W), BF16))

    def stream_scratch(n_tok):
        return [pltpu.VMEM((n_tok, HEAD_DIM), BF16), pltpu.VMEM((HEAD_DIM, n_tok), F32),
                pltpu.VMEM((n_tok, ML_AUG), BF16)]

    res = pl.pallas_call(
        functools.partial(_mlstm_kernel, n_lat=n_lat, n_ctx=n_ctx, need_ctx=need_ctx),
        grid=(batch, ML_HEADS),
        in_specs=[col(seq, cq), col(seq, ck), col(seq, cv), col(seq, co),
                  col(ctx_len, cq), col(ctx_len, ck), col(ctx_len, cv), col(ctx_len, co),
                  pl.BlockSpec((N_GATES, None, n_pad, ML_CHUNK), lambda b, h: (0, b, 0, 0)),
                  pl.BlockSpec((N_GATES, 1, ML_CHUNK), lambda b, h: (0, 0, 0)),
                  pl.BlockSpec((3, HEAD_DIM), lambda b, h: (0, h)),
                  pl.BlockSpec((3, HEAD_DIM), lambda b, h: (0, ML_HEADS + h)),
                  pl.BlockSpec((1, HEAD_DIM), lambda b, h: (0, h)),
                  pl.BlockSpec((1, HEAD_DIM), lambda b, h: (0, ML_HEADS + h)),
                  pl.BlockSpec((1, HEAD_DIM), lambda b, h: (0, h))],
        out_specs=out_specs,
        out_shape=out_shape,
        scratch_shapes=(stream_scratch(seq) + stream_scratch(ctx_len)
                        + [pltpu.VMEM((seq, HEAD_DIM), F32), pltpu.VMEM((seq, HEAD_DIM), F32),
                           pltpu.VMEM((ctx_len, HEAD_DIM), F32), pltpu.VMEM((ctx_len, HEAD_DIM), F32),
                           pltpu.VMEM((2 * N_STATS, n_pad, ML_CHUNK), F32),
                           pltpu.VMEM((4, n_lat + n_ctx, ML_CHUNK, ML_CHUNK), F32)]),
        compiler_params=_cparams(("parallel", "parallel")),
        name="mlstm",
    )(p_lat, p_lat, p_lat, p_lat, p_ctx, p_ctx, p_ctx, p_ctx, gates, gate_b,
      conv_w, conv_w, conv_b, conv_b, out_norm)
    return (res[0], res[1]) if need_ctx else (res[0], None)


def kernel(x, c, ctx, c_ctx, w_ada, b_ada, norm_ff1, ff1_gate, ff1_up, ff1_down, norm_mix, w_in, na_rpb,
           gqa_q_norm, gqa_k_norm, ml_conv_w, ml_conv_b, ml_gate_b, ml_out_norm, w_out, norm_ff2,
           ff2_gate, ff2_up, ff2_down, final_norm):
    batch, seq, d = x.shape
    ctx_len = ctx.shape[1]
    depth = w_ada.shape[0]
    ctx_group = batch
    n_lat_chunks, n_ctx_chunks = seq // ML_CHUNK, ctx_len // ML_CHUNK
    n_chunk_pad = -(-(n_lat_chunks + n_ctx_chunks) // 8) * 8

    xl = x.reshape(batch * seq, d)
    xc = ctx.reshape(batch * ctx_len, d)
    cond = jnp.zeros((8, d), F32).at[:batch].set(c).at[batch].set(c_ctx)
    rope_tabs = _rope_tables(seq)
    row2 = lambda v: v.reshape(1, -1)
    fin = row2(final_norm)

    for l in range(depth):
        last = l == depth - 1
        mod = _ada_call(cond, w_ada[l], row2(b_ada[l])).reshape(8 * N_MOD, 1, d)
        lat = dict(group_rows=seq, fixed_group=None)
        cx = dict(group_rows=None, fixed_group=ctx_group)

        cast = lambda w: _cast_call(w, l, w.shape[2])
        wg, wu, wd = cast(ff1_gate), cast(ff1_up), cast(ff1_down)
        g1 = row2(norm_ff1[l])
        xl = _ffn_call(xl, mod, 0, g1, wg, wu, wd, fin, final_norm=False, **lat)
        xc = _ffn_call(xc, mod, 0, g1, wg, wu, wd, fin, final_norm=False, **cx)

        w_main = _cast_call(w_in, l, COL_GATES)
        w_gates_t = w_in[l][:, COL_GATES:].T.astype(BF16)
        gm = row2(norm_mix[l])
        p_lat, gt_lat = _inproj_call(xl, mod, gm, w_main, w_gates_t, **lat)
        p_ctx, gt_ctx = _inproj_call(xc, mod, gm, w_main, w_gates_t, **cx)

        ya_lat = _na_call(p_lat, p_ctx, *_na_toeplitz_slabs(na_rpb[l]), batch, seq, ctx_len)
        qg, kg = row2(gqa_q_norm[l]), row2(gqa_k_norm[l])
        qn_lat, kn_lat = _gqa_prep_call(p_lat, qg, kg, rope_tabs, seq, rope=True)
        qn_ctx, kn_ctx = _gqa_prep_call(p_ctx, qg, kg, rope_tabs, seq, rope=False, tm=ctx_len)
        yb_lat = _gqa_call(qn_lat, kn_lat, p_lat, kn_ctx, p_ctx, batch, seq, ctx_len)
        gates = jnp.concatenate(
            [gt_lat.reshape(N_GATES, batch, n_lat_chunks, ML_CHUNK),
             gt_ctx.reshape(N_GATES, batch, n_ctx_chunks, ML_CHUNK),
             jnp.zeros((N_GATES, batch, n_chunk_pad - n_lat_chunks - n_ctx_chunks, ML_CHUNK), F32)], axis=2)
        gate_b = jnp.broadcast_to(ml_gate_b[l][:, None, None], (N_GATES, 1, ML_CHUNK))
        yc_lat, yc_ctx = _mlstm_call(p_lat, p_ctx, gates, gate_b, ml_conv_w[l], row2(ml_conv_b[l]),
                                     row2(ml_out_norm[l]), batch, seq, ctx_len, not last)

        wo = cast(w_out)
        xl = _outproj_call(xl, mod, ya_lat, yb_lat, yc_lat, wo, **lat)
        wg, wu, wd = cast(ff2_gate), cast(ff2_up), cast(ff2_down)
        g2 = row2(norm_ff2[l])
        xl = _ffn_call(xl, mod, 6, g2, wg, wu, wd, fin, final_norm=last, **lat)
        if not last:
            ya_ctx = _ctx_attn_call(p_ctx, p_ctx, p_ctx, batch, ctx_len, NA_HEADS,
                                    COL_NA_Q // HEAD_DIM, COL_NA_K // HEAD_DIM, COL_NA_V // HEAD_DIM, 1,
                                    HEAD_DIM ** -0.5)
            yb_ctx = _ctx_attn_call(qn_ctx, kn_ctx, p_ctx, batch, ctx_len, GQA_Q_HEADS, 0, 0,
                                    COL_GQ_V // HEAD_DIM, GQA_Q_HEADS // GQA_KV_HEADS, None)
            xc = _outproj_call(xc, mod, ya_ctx, yb_ctx, yc_ctx, wo, **cx)
            xc = _ffn_call(xc, mod, 6, g2, wg, wu, wd, fin, final_norm=False, **cx)
    return xl.reshape(batch, seq, d)
```

```python
import functools

import numpy as np
import jax
import jax.numpy as jnp
from jax import lax
from jax.experimental import pallas as pl
from jax.experimental.pallas import tpu as pltpu

F32 = jnp.float32
BF16 = jnp.bfloat16

GRID_W = 64
HEAD_DIM = 128
NA_HEADS = 4
GQA_Q_HEADS = 4
GQA_KV_HEADS = 2
ML_HEADS = 8
NA_W = NA_HEADS * HEAD_DIM
GQA_W = GQA_Q_HEADS * HEAD_DIM
KV_W = GQA_KV_HEADS * HEAD_DIM
ML_W = ML_HEADS * HEAD_DIM
NA_WIN_R = 8
NA_WIN_C = 16
ML_CHUNK = 128
ROPE_THETA = 10000.0
N_MOD = 9
EPS = 1e-6
N_GATES = 4 * ML_HEADS
COL_NA_Q = 0
COL_NA_K = NA_W
COL_NA_V = 2 * NA_W
COL_GQ_Q = 3 * NA_W
COL_GQ_K = COL_GQ_Q + GQA_W
COL_GQ_V = COL_GQ_K + KV_W
COL_ML_Q = COL_GQ_V + KV_W
COL_ML_K = COL_ML_Q + ML_W
COL_ML_V = COL_ML_K + ML_W
COL_ML_O = COL_ML_V + ML_W
COL_GATES = COL_ML_O + ML_W

NEG_BIG = -1e30
VMEM_LIMIT_V7X = 56 * 1024 * 1024
BF16_ROWS = 16
NORM_ROWS = 32
FFN_SUBTILE = 256
FFN_OUT_SUBTILE = 1024

NA_QROWS = 8
NA_KROWS = 16
NA_MASKED_SLAB = 2 * NA_WIN_R - 1


def _cparams(sem):
    return pltpu.CompilerParams(dimension_semantics=sem, vmem_limit_bytes=VMEM_LIMIT_V7X)


def _rms(x, gain):
    return x * lax.rsqrt(jnp.mean(x * x, axis=-1, keepdims=True) + EPS) * gain


def _norm_modulate_store(x_ref, gain_ref, scale_ref, shift_ref, h_ref):
    gain_scale = gain_ref[...] * (1.0 + scale_ref[...])
    shift = shift_ref[...]

    def body(i, carry):
        r0 = pl.multiple_of(i * NORM_ROWS, NORM_ROWS)
        x = x_ref[pl.ds(r0, NORM_ROWS), :]
        inv = lax.rsqrt(jnp.mean(x * x, axis=-1, keepdims=True) + EPS)
        h_ref[pl.ds(r0, NORM_ROWS), :] = (x * inv * gain_scale + shift).astype(h_ref.dtype)
        return carry

    lax.fori_loop(0, x_ref.shape[0] // NORM_ROWS, body, 0, unroll=2)


def _dot(a, b):
    return jnp.dot(a, b, preferred_element_type=F32)


def _dot_nt(a, b):
    return lax.dot_general(a, b, (((1,), (1,)), ((), ())), preferred_element_type=F32)


def _dot_tn(a, b):
    return lax.dot_general(a, b, (((0,), (0,)), ((), ())), preferred_element_type=F32)


def _cast_kernel(w_ref, o_ref):
    o_ref[...] = w_ref[...].astype(o_ref.dtype)


def _cast_call(w_stack, layer, n_cols, row_blocks=8):
    _, rows, _ = w_stack.shape
    tr = rows // row_blocks
    return pl.pallas_call(
        _cast_kernel,
        grid=(row_blocks,),
        in_specs=[pl.BlockSpec((None, tr, n_cols), lambda i: (layer, i, 0))],
        out_specs=pl.BlockSpec((tr, n_cols), lambda i: (i, 0)),
        out_shape=jax.ShapeDtypeStruct((rows, n_cols), BF16),
        compiler_params=_cparams(("parallel",)),
        name="weight_cast",
    )(w_stack)


def _ada_kernel(c_ref, w_ref, b_ref, o_ref):
    c = c_ref[...]
    a = (c * jax.nn.sigmoid(c)).astype(BF16)
    o_ref[...] = _dot(a, w_ref[...].astype(BF16)) + b_ref[...]


def _ada_call(cond, w_stack, layer, b, tn=1024):
    m, d = cond.shape
    n = w_stack.shape[2]
    return pl.pallas_call(
        _ada_kernel,
        grid=(n // tn,),
        in_specs=[pl.BlockSpec((m, d), lambda j: (0, 0)),
                  pl.BlockSpec((None, d, tn), lambda j: (layer, 0, j)),
                  pl.BlockSpec((1, tn), lambda j: (0, j))],
        out_specs=pl.BlockSpec((m, tn), lambda j: (0, j)),
        out_shape=jax.ShapeDtypeStruct((m, n), F32),
        compiler_params=_cparams(("arbitrary",)),
        name="ada_mod",
    )(cond, w_stack, b)


def _mod_spec(k, tm, group_rows, fixed_group):
    def index_map(i, *_):
        grp = fixed_group if fixed_group is not None else i // (group_rows // tm)
        return (grp * N_MOD + k, 0, 0)
    return index_map


def _ffn_kernel(x_ref, sh_ref, sc_ref, gt_ref, gain_ref, wg_ref, wu_ref, wd_ref, fin_ref, o_ref,
                h_scr, *, n_ff, final_norm):
    j = pl.program_id(1)

    @pl.when(j == 0)
    def _():
        _norm_modulate_store(x_ref, gain_ref, sc_ref, sh_ref, h_scr)
        o_ref[...] = jnp.zeros_like(o_ref)

    h = h_scr[...]
    tf = wg_ref.shape[1]
    for c0 in range(0, tf, FFN_SUBTILE):
        g = _dot(h, wg_ref[:, c0:c0 + FFN_SUBTILE])
        u = _dot(h, wu_ref[:, c0:c0 + FFN_SUBTILE])
        a = ((g * jax.nn.sigmoid(g)) * u).astype(BF16)
        for n0 in range(0, o_ref.shape[1], FFN_OUT_SUBTILE):
            o_ref[:, n0:n0 + FFN_OUT_SUBTILE] += _dot(a, wd_ref[c0:c0 + FFN_SUBTILE, n0:n0 + FFN_OUT_SUBTILE])

    @pl.when(j == n_ff - 1)
    def _():
        half_gate = 0.5 * gt_ref[...]

        def body(i, carry):
            r0 = pl.multiple_of(i * NORM_ROWS, NORM_ROWS)
            out = x_ref[pl.ds(r0, NORM_ROWS), :] + half_gate * o_ref[pl.ds(r0, NORM_ROWS), :]
            if final_norm:
                out = _rms(out, fin_ref[...])
            o_ref[pl.ds(r0, NORM_ROWS), :] = out
            return carry

        lax.fori_loop(0, o_ref.shape[0] // NORM_ROWS, body, 0, unroll=2)


def _ffn_call(x, mod, k0, gain, wg, wu, wd, fin, *, group_rows, fixed_group, final_norm,
              tm=1024, tf=512):
    t, d = x.shape
    dff = wg.shape[1]
    n_ff = dff // tf
    mspec = lambda k: pl.BlockSpec((None, 1, d), _mod_spec(k, tm, group_rows, fixed_group))
    return pl.pallas_call(
        functools.partial(_ffn_kernel, n_ff=n_ff, final_norm=final_norm),
        grid=(t // tm, n_ff),
        in_specs=[pl.BlockSpec((tm, d), lambda i, j: (i, 0)),
                  mspec(k0), mspec(k0 + 1), mspec(k0 + 2),
                  pl.BlockSpec((1, d), lambda i, j: (0, 0)),
                  pl.BlockSpec((d, tf), lambda i, j: (0, j)),
                  pl.BlockSpec((d, tf), lambda i, j: (0, j)),
                  pl.BlockSpec((tf, d), lambda i, j: (j, 0)),
                  pl.BlockSpec((1, d), lambda i, j: (0, 0))],
        out_specs=pl.BlockSpec((tm, d), lambda i, j: (i, 0)),
        out_shape=jax.ShapeDtypeStruct((t, d), F32),
        scratch_shapes=[pltpu.VMEM((tm, d), BF16)],
        compiler_params=_cparams(("parallel", "arbitrary")),
        name="macaron_ffn",
    )(x, mod, mod, mod, gain, wg, wu, wd, fin)


def _inproj_kernel(x_ref, sh_ref, sc_ref, gain_ref, w_ref, wg_ref, p_ref, g_ref, h_scr):
    j = pl.program_id(1)

    @pl.when(j == 0)
    def _():
        h = _rms(x_ref[...], gain_ref[...]) * (1.0 + sc_ref[...]) + sh_ref[...]
        hb = h.astype(BF16)
        h_scr[...] = hb
        g_ref[...] = _dot(hb, wg_ref[...])

    p_ref[...] = _dot(h_scr[...], w_ref[...]).astype(p_ref.dtype)


def _inproj_call(x, mod, gain, w_main, w_gates, *, group_rows, fixed_group, tm=1024, tn=1664):
    t, d = x.shape
    n = w_main.shape[1]
    mspec = lambda k: pl.BlockSpec((None, 1, d), _mod_spec(k, tm, group_rows, fixed_group))
    return pl.pallas_call(
        _inproj_kernel,
        grid=(t // tm, n // tn),
        in_specs=[pl.BlockSpec((tm, d), lambda i, j: (i, 0)),
                  mspec(3), mspec(4),
                  pl.BlockSpec((1, d), lambda i, j: (0, 0)),
                  pl.BlockSpec((d, tn), lambda i, j: (0, j)),
                  pl.BlockSpec((d, N_GATES), lambda i, j: (0, 0))],
        out_specs=[pl.BlockSpec((tm, tn), lambda i, j: (i, j)),
                   pl.BlockSpec((tm, N_GATES), lambda i, j: (i, 0))],
        out_shape=[jax.ShapeDtypeStruct((t, n), BF16),
                   jax.ShapeDtypeStruct((t, N_GATES), F32)],
        scratch_shapes=[pltpu.VMEM((tm, d), BF16)],
        compiler_params=_cparams(("parallel", "arbitrary")),
        name="mixer_in_proj",
    )(x, mod, mod, gain, w_main, w_gates)


def _outproj_kernel(x_ref, gt_ref, ya_ref, yb_ref, yc_ref, w_ref, o_ref):
    acc = _dot(ya_ref[...], w_ref[0:NA_W, :])
    acc += _dot(yb_ref[...], w_ref[NA_W:NA_W + GQA_W, :])
    acc += _dot(yc_ref[...], w_ref[NA_W + GQA_W:, :])
    o_ref[...] = x_ref[...] + gt_ref[...] * acc


def _outproj_call(x, mod, ya, yb, yc, w, *, group_rows, fixed_group, tm=512):
    t, d = x.shape
    return pl.pallas_call(
        _outproj_kernel,
        grid=(t // tm,),
        in_specs=[pl.BlockSpec((tm, d), lambda i: (i, 0)),
                  pl.BlockSpec((None, 1, d), _mod_spec(5, tm, group_rows, fixed_group)),
                  pl.BlockSpec((tm, NA_W), lambda i: (i, 0)),
                  pl.BlockSpec((tm, GQA_W), lambda i: (i, 0)),
                  pl.BlockSpec((tm, ML_W), lambda i: (i, 0)),
                  pl.BlockSpec((d, d), lambda i: (0, 0))],
        out_specs=pl.BlockSpec((tm, d), lambda i: (i, 0)),
        out_shape=jax.ShapeDtypeStruct((t, d), F32),
        compiler_params=_cparams(("parallel",)),
        name="mixer_out_proj",
    )(x, mod, ya, yb, yc, w)


def _na_kernel(q_ref, k_ref, v_ref, kc_ref, vc_ref, tl_ref, tr_ref, o_ref, mb_scr, *, rows):
    rb = pl.program_id(2)
    nblk = rows // NA_QROWS
    key_row0 = jnp.clip(rb * NA_QROWS - NA_WIN_R // 2, 0, rows - NA_KROWS)

    @pl.when((rb <= 1) | (rb == nblk - 1))
    def _():
        for a in range(NA_QROWS):
            r = rb * NA_QROWS + a
            row_start = jnp.clip(r - NA_WIN_R // 2, 0, rows - NA_WIN_R)
            for j in range(NA_KROWS // 2):
                sel = []
                for kr in (key_row0 + 2 * j, key_row0 + 2 * j + 1):
                    in_win = (kr >= row_start) & (kr < row_start + NA_WIN_R)
                    sel.append(jnp.where(in_win, kr - r + (NA_WIN_R - 1), NA_MASKED_SLAB))
                mb_scr[a * GRID_W:(a + 1) * GRID_W, 2 * j * GRID_W:(2 * j + 2) * GRID_W] = (
                    tl_ref[sel[0]] + tr_ref[sel[1]])

    start = pl.multiple_of(key_row0 * GRID_W, GRID_W)
    nk = NA_KROWS * GRID_W
    scale = HEAD_DIM ** -0.5
    q = q_ref[...]
    k = k_ref[pl.ds(start, nk), :]
    v = v_ref[pl.ds(start, nk), :]
    s_loc = _dot_nt(q, k) * scale + mb_scr[...]
    s_ctx = _dot_nt(q, kc_ref[...]) * scale
    m = jnp.maximum(jnp.max(s_loc, axis=-1, keepdims=True), jnp.max(s_ctx, axis=-1, keepdims=True))
    p_loc = jnp.exp(s_loc - m)
    p_ctx = jnp.exp(s_ctx - m)
    den = jnp.sum(p_loc, axis=-1, keepdims=True) + jnp.sum(p_ctx, axis=-1, keepdims=True)
    out = _dot(p_loc.astype(BF16), v) + _dot(p_ctx.astype(BF16), vc_ref[...])
    o_ref[...] = (out / den).astype(o_ref.dtype)


def _na_toeplitz_slabs(rpb):
    n_h, n_dr, n_dc = rpb.shape
    edge = GRID_W - NA_WIN_C
    ext = jnp.concatenate(
        [rpb[..., NA_WIN_C - 1:], jnp.broadcast_to(rpb[..., n_dc - 1:], (n_h, n_dr, edge)),
         jnp.zeros((n_h, n_dr, 1), F32),
         jnp.broadcast_to(rpb[..., :1], (n_h, n_dr, edge)), rpb[..., :NA_WIN_C - 1]], axis=-1)
    width = 2 * GRID_W
    tiled = jnp.broadcast_to(ext[:, :, None, :], (n_h, n_dr, GRID_W, width)).reshape(n_h, n_dr, GRID_W * width)
    toep = tiled[..., :GRID_W * (width - 1)].reshape(n_h, n_dr, GRID_W, width - 1)[..., :GRID_W]
    cq = np.arange(GRID_W)
    col_start = np.clip(cq - NA_WIN_C // 2, 0, GRID_W - NA_WIN_C)
    col_ok = (cq[None, :] >= col_start[:, None]) & (cq[None, :] < col_start[:, None] + NA_WIN_C)
    slabs = jnp.where(col_ok, toep, NEG_BIG)
    slabs = jnp.concatenate([slabs, jnp.full((n_h, 1, GRID_W, GRID_W), NEG_BIG, F32)], axis=1)
    zeros = jnp.zeros_like(slabs)
    return jnp.concatenate([slabs, zeros], axis=-1), jnp.concatenate([zeros, slabs], axis=-1)


def _na_call(p_lat, p_ctx, slabs_left, slabs_right, batch, seq, ctx_len):
    rows = seq // GRID_W
    nblk = rows // NA_QROWS
    tq = NA_QROWS * GRID_W
    nk = NA_KROWS * GRID_W
    cq, ck, cv = COL_NA_Q // HEAD_DIM, COL_NA_K // HEAD_DIM, COL_NA_V // HEAD_DIM
    slab_spec = pl.BlockSpec((None, NA_MASKED_SLAB + 1, GRID_W, 2 * GRID_W), lambda b, h, r: (h, 0, 0, 0))

    return pl.pallas_call(
        functools.partial(_na_kernel, rows=rows),
        grid=(batch, NA_HEADS, nblk),
        in_specs=[pl.BlockSpec((tq, HEAD_DIM), lambda b, h, r: (b * nblk + r, cq + h)),
                  pl.BlockSpec((seq, HEAD_DIM), lambda b, h, r: (b, ck + h)),
                  pl.BlockSpec((seq, HEAD_DIM), lambda b, h, r: (b, cv + h)),
                  pl.BlockSpec((ctx_len, HEAD_DIM), lambda b, h, r: (b, ck + h)),
                  pl.BlockSpec((ctx_len, HEAD_DIM), lambda b, h, r: (b, cv + h)),
                  slab_spec, slab_spec],
        out_specs=pl.BlockSpec((tq, HEAD_DIM), lambda b, h, r: (b * nblk + r, h)),
        out_shape=jax.ShapeDtypeStruct((batch * seq, NA_W), BF16),
        scratch_shapes=[pltpu.VMEM((tq, nk), F32)],
        compiler_params=_cparams(("parallel", "parallel", "arbitrary")),
        name="neighbourhood_attn",
    )(p_lat, p_lat, p_lat, p_ctx, p_ctx, slabs_left, slabs_right)


def _ctx_attn_kernel(q_ref, k_ref, v_ref, o_ref, *, scale):
    s = _dot_nt(q_ref[...], k_ref[...])
    if scale is not None:
        s = s * scale
    m = jnp.max(s, axis=-1, keepdims=True)
    p = jnp.exp(s - m)
    den = jnp.sum(p, axis=-1, keepdims=True)
    o_ref[...] = (_dot(p.astype(BF16), v_ref[...]) / den).astype(o_ref.dtype)


def _ctx_attn_call(q_arr, k_arr, v_arr, batch, ctx_len, n_heads, q_col, k_col, v_col, kv_group, scale):
    return pl.pallas_call(
        functools.partial(_ctx_attn_kernel, scale=scale),
        grid=(batch, n_heads),
        in_specs=[pl.BlockSpec((ctx_len, HEAD_DIM), lambda b, h: (b, q_col + h)),
                  pl.BlockSpec((ctx_len, HEAD_DIM), lambda b, h: (b, k_col + h // kv_group)),
                  pl.BlockSpec((ctx_len, HEAD_DIM), lambda b, h: (b, v_col + h // kv_group))],
        out_specs=pl.BlockSpec((ctx_len, HEAD_DIM), lambda b, h: (b, h)),
        out_shape=jax.ShapeDtypeStruct((batch * ctx_len, n_heads * HEAD_DIM), BF16),
        compiler_params=_cparams(("parallel", "parallel")),
        name="ctx_attn",
    )(q_arr, k_arr, v_arr)


def _rope(x, cos, sin_lo, sin_hi):
    return (x * cos + pltpu.roll(x, HEAD_DIM - HEAD_DIM // 4, 1) * sin_lo
            + pltpu.roll(x, HEAD_DIM // 4, 1) * sin_hi)


def _gqa_prep_kernel(q_ref, k_ref, qg_ref, kg_ref, cos_ref, slo_ref, shi_ref, qo_ref, ko_ref, *, rope):
    def prep(x, gain, scale):
        y = _rms(x.astype(F32), gain)
        if rope:
            y = _rope(y, cos_ref[...], slo_ref[...], shi_ref[...])
        if scale is not None:
            y = y * scale
        return y.astype(BF16)

    for h in range(GQA_Q_HEADS):
        sl = slice(h * HEAD_DIM, (h + 1) * HEAD_DIM)
        qo_ref[:, sl] = prep(q_ref[:, sl], qg_ref[...], HEAD_DIM ** -0.5)
    for h in range(GQA_KV_HEADS):
        sl = slice(h * HEAD_DIM, (h + 1) * HEAD_DIM)
        ko_ref[:, sl] = prep(k_ref[:, sl], kg_ref[...], None)


def _rope_tables(n_tokens):
    t = jnp.arange(n_tokens)
    row = (t // GRID_W).astype(F32)
    col = (t % GRID_W).astype(F32)
    half = HEAD_DIM // 2
    inv_freq = 1.0 / (ROPE_THETA ** (jnp.arange(0, half, 2, dtype=F32) / half))
    ang_r = row[:, None] * inv_freq[None, :]
    ang_c = col[:, None] * inv_freq[None, :]
    ang = jnp.concatenate([ang_r, ang_r, ang_c, ang_c], axis=-1)
    cos, sin = jnp.cos(ang), jnp.sin(ang)
    lo = (jnp.arange(HEAD_DIM) % half) < (half // 2)
    return cos, jnp.where(lo, -sin, 0.0), jnp.where(lo, 0.0, sin)


def _gqa_prep_call(p, q_gain, k_gain, tables, seq, *, rope, tm=512):
    t = p.shape[0]
    per_seq = seq // tm if rope else 1
    tab_spec = pl.BlockSpec((tm, HEAD_DIM), lambda i: (i % per_seq, 0))
    return pl.pallas_call(
        functools.partial(_gqa_prep_kernel, rope=rope),
        grid=(t // tm,),
        in_specs=[pl.BlockSpec((tm, GQA_W), lambda i: (i, COL_GQ_Q // GQA_W)),
                  pl.BlockSpec((tm, KV_W), lambda i: (i, COL_GQ_K // KV_W)),
                  pl.BlockSpec((1, HEAD_DIM), lambda i: (0, 0)),
                  pl.BlockSpec((1, HEAD_DIM), lambda i: (0, 0)),
                  tab_spec, tab_spec, tab_spec],
        out_specs=[pl.BlockSpec((tm, GQA_W), lambda i: (i, 0)),
                   pl.BlockSpec((tm, KV_W), lambda i: (i, 0))],
        out_shape=[jax.ShapeDtypeStruct((t, GQA_W), BF16),
                   jax.ShapeDtypeStruct((t, KV_W), BF16)],
        compiler_params=_cparams(("parallel",)),
        name="gqa_prep",
    )(p, p, q_gain, k_gain, *tables)


def _gqa_kernel(q_ref, kl_ref, vl_ref, kc_ref, vc_ref, o_ref, *, tk):
    tq = q_ref.shape[0]
    grp = q_ref.shape[1] // HEAD_DIM
    q = jnp.concatenate([q_ref[:, g * HEAD_DIM:(g + 1) * HEAD_DIM] for g in range(grp)], axis=0)

    s = _dot_nt(q, kc_ref[...])
    m = jnp.max(s, axis=-1, keepdims=True)
    p = jnp.exp(s - m)
    den = jnp.sum(p, axis=-1, keepdims=True)
    acc = _dot(p.astype(BF16), vc_ref[...])

    def tile(i, carry):
        m, den, acc = carry
        t0 = pl.multiple_of(i * tk, tk)
        s = _dot_nt(q, kl_ref[pl.ds(t0, tk), :])
        m_new = jnp.maximum(m, jnp.max(s, axis=-1, keepdims=True))
        p = jnp.exp(s - m_new)
        alpha = jnp.exp(m - m_new)
        den = alpha * den + jnp.sum(p, axis=-1, keepdims=True)
        acc = alpha * acc + _dot(p.astype(BF16), vl_ref[pl.ds(t0, tk), :])
        return m_new, den, acc

    m, den, acc = lax.fori_loop(0, kl_ref.shape[0] // tk, tile, (m, den, acc), unroll=True)
    out = (acc / den).astype(o_ref.dtype)
    for g in range(grp):
        o_ref[:, g * HEAD_DIM:(g + 1) * HEAD_DIM] = out[g * tq:(g + 1) * tq]


def _gqa_call(q_lat, k_lat, p_lat, k_ctx, p_ctx, batch, seq, ctx_len, tq=256, tk=512):
    grp = GQA_Q_HEADS // GQA_KV_HEADS
    nq = seq // tq
    cv = COL_GQ_V // HEAD_DIM
    return pl.pallas_call(
        functools.partial(_gqa_kernel, tk=tk),
        grid=(batch, GQA_KV_HEADS, nq),
        in_specs=[pl.BlockSpec((tq, grp * HEAD_DIM), lambda b, h, i: (b * nq + i, h)),
                  pl.BlockSpec((seq, HEAD_DIM), lambda b, h, i: (b, h)),
                  pl.BlockSpec((seq, HEAD_DIM), lambda b, h, i: (b, cv + h)),
                  pl.BlockSpec((ctx_len, HEAD_DIM), lambda b, h, i: (b, h)),
                  pl.BlockSpec((ctx_len, HEAD_DIM), lambda b, h, i: (b, cv + h))],
        out_specs=pl.BlockSpec((tq, grp * HEAD_DIM), lambda b, h, i: (b * nq + i, h)),
        out_shape=jax.ShapeDtypeStruct((batch * seq, GQA_W), BF16),
        compiler_params=_cparams(("parallel", "parallel", "arbitrary")),
        name="gqa_attn",
    )(q_lat, k_lat, p_lat, k_ctx, p_ctx)


ST_R, ST_W, ST_BTOT, ST_AMAX, ST_MPREV, ST_DEC, ST_INP, ST_BIGR, ST_M = range(9)
N_STATS = 9
REP_R, REP_M = range(2)
ML_AUG = 2 * HEAD_DIM


def _mlstm_kernel(ql_ref, kl_ref, vl_ref, ol_ref, qc_ref, kc_ref, vc_ref, oc_ref, g_ref, gb_ref,
                  cwq_ref, cwk_ref, cbq_ref, cbk_ref, on_ref, yl_ref, *rest,
                  n_lat, n_ctx, need_ctx):
    if need_ctx:
        yc_ref = rest[0]
        rest = rest[1:]
    qa_l, kt_l, va_l, qa_c, kt_c, va_c, hf_l, hb_l, hf_c, hb_c, st, rep = rest
    L = ML_CHUNK
    n_all = n_lat + n_ctx
    h = pl.program_id(1)
    row = lax.broadcasted_iota(jnp.int32, (L, L), 0)
    col = lax.broadcasted_iota(jnp.int32, (L, L), 1)

    tri_fw = (row <= col).astype(F32)
    tri_bw = (row >= col).astype(F32)
    b_cum = []
    for d, tri in enumerate((tri_fw, tri_bw)):
        log_i = g_ref[2 * d * ML_HEADS + h] + gb_ref[2 * d * ML_HEADS + h]
        log_f = jax.nn.log_sigmoid(g_ref[(2 * d + 1) * ML_HEADS + h] + gb_ref[(2 * d + 1) * ML_HEADS + h])
        b = jnp.dot(log_f, tri, preferred_element_type=F32, precision=lax.Precision.HIGHEST)
        b_tot = b[:, L - 1:L] if d == 0 else b[:, 0:1]
        a = b_tot - b + log_i
        a_max = jnp.max(a, axis=-1, keepdims=True)
        r = log_i - b
        lane = lax.broadcasted_iota(jnp.int32, r.shape, 1)
        run = r
        shift = 1
        while shift < L:
            if d == 0:
                run = jnp.maximum(run, jnp.where(lane >= shift, pltpu.roll(run, shift, 1), NEG_BIG))
            else:
                run = jnp.maximum(run, jnp.where(lane < L - shift, pltpu.roll(run, L - shift, 1), NEG_BIG))
            shift *= 2
        base = d * N_STATS
        st[base + ST_R] = r
        st[base + ST_W] = jnp.exp(a - a_max)
        st[base + ST_BTOT] = jnp.broadcast_to(b_tot, b.shape)
        st[base + ST_AMAX] = jnp.broadcast_to(a_max, b.shape)
        st[base + ST_BIGR] = run
        st[base + ST_MPREV] = jnp.zeros_like(b)
        b_cum.append(b)

    def stabiliser_step(i, carry):
        new = []
        for d, m_prev in enumerate(carry):
            in_ctx = i < n_ctx
            j = jnp.where(in_ctx, i, i - n_ctx)
            if d == 0:
                n = jnp.where(in_ctx, n_lat + j, j)
            else:
                n = jnp.where(in_ctx, n_lat + n_ctx - 1 - j, n_lat - 1 - j)
            base = d * N_STATS
            b_tot = st[base + ST_BTOT, pl.ds(n, 1), :]
            a_max = st[base + ST_AMAX, pl.ds(n, 1), :]
            m_new = jnp.maximum(b_tot + m_prev, a_max)
            st[base + ST_MPREV, pl.ds(n, 1), :] = m_prev
            st[base + ST_DEC, pl.ds(n, 1), :] = jnp.exp(b_tot + m_prev - m_new)
            st[base + ST_INP, pl.ds(n, 1), :] = jnp.exp(a_max - m_new)
            new.append(m_new)
        return tuple(new)

    lax.fori_loop(0, n_all, stabiliser_step, (jnp.zeros((1, L), F32), jnp.zeros((1, L), F32)))

    for d in range(2):
        base = d * N_STATS
        big_r = jnp.maximum(st[base + ST_BIGR], st[base + ST_MPREV])
        st[base + ST_BIGR] = big_r
        st[base + ST_M] = b_cum[d] + big_r

    def column_tables(n, carry):
        for d in range(2):
            for k_rep, k_st in ((REP_R, ST_BIGR), (REP_M, ST_M)):
                rows = jnp.broadcast_to(st[d * N_STATS + k_st, pl.ds(n, 1), :], (L, L))
                rep[2 * d + k_rep, n] = rows.T
        return carry

    lax.fori_loop(0, n_all, column_tables, 0, unroll=2)

    def prepare_stream(q_src, k_src, v_src, q_dst, kt_dst, va_dst, n_chunks):
        n_tok = n_chunks * L
        sub = lax.broadcasted_iota(jnp.int32, (L, HEAD_DIM), 0)

        def conv_silu(src_ref, w_ref, b_ref, n, t0):
            x = src_ref[pl.ds(t0, L), :].astype(F32)
            prev_t0 = pl.multiple_of(jnp.maximum(t0 - BF16_ROWS, 0), BF16_ROWS)
            next_t0 = pl.multiple_of(jnp.minimum(t0 + L, n_tok - BF16_ROWS), BF16_ROWS)
            prev_row = (src_ref[pl.ds(prev_t0, BF16_ROWS), :].astype(F32)[BF16_ROWS - 1:, :]
                        * jnp.where(n > 0, 1.0, 0.0))
            next_row = (src_ref[pl.ds(next_t0, BF16_ROWS), :].astype(F32)[:1, :]
                        * jnp.where(n < n_chunks - 1, 1.0, 0.0))
            x_prev = jnp.where(sub == 0, prev_row, pltpu.roll(x, 1, 0))
            x_next = jnp.where(sub == L - 1, next_row, pltpu.roll(x, L - 1, 0))
            y = w_ref[0:1, :] * x_prev + w_ref[1:2, :] * x + w_ref[2:3, :] * x_next + b_ref[...]
            return y * jax.nn.sigmoid(y)

        def body(n, carry):
            t0 = pl.multiple_of(n * L, L)
            q_dst[pl.ds(t0, L), :] = conv_silu(q_src, cwq_ref, cbq_ref, n, t0).astype(BF16)
            k_act = conv_silu(k_src, cwk_ref, cbk_ref, n, t0) * (HEAD_DIM ** -0.5)
            kt_dst[:, pl.ds(t0, L)] = k_act.T
            va_dst[pl.ds(t0, L), 0:HEAD_DIM] = v_src[pl.ds(t0, L), :]
            va_dst[pl.ds(t0, L), HEAD_DIM:ML_AUG] = jnp.ones((L, HEAD_DIM), BF16)
            return carry

        lax.fori_loop(0, n_chunks, body, 0, unroll=2)

    prepare_stream(ql_ref, kl_ref, vl_ref, qa_l, kt_l, va_l, n_lat)
    prepare_stream(qc_ref, kc_ref, vc_ref, qa_c, kt_c, va_c, n_ctx)

    def chunk(d, n_stat, t0, q_src, kt_src, va_src, c_prev, want_out):
        base = d * N_STATS
        stat = lambda k: st[base + k, pl.ds(n_stat, 1), :]
        both = lambda x: jnp.concatenate([x, x], axis=-1)
        kt = kt_src[:, pl.ds(t0, L)]
        va = va_src[pl.ds(t0, L), :]
        out = None
        if want_out:
            q = q_src[pl.ds(t0, L), :]
            seen = (col <= row) if d == 0 else (col >= row)
            big_r = rep[2 * d + REP_R, n_stat]
            dec_mat = jnp.exp(jnp.where(seen, stat(ST_R) - big_r, NEG_BIG))
            s = _dot(q, kt.astype(BF16)) * dec_mat
            g = jnp.exp(stat(ST_MPREV) - big_r)
            lhs = jnp.concatenate([(g * q.astype(F32)).astype(BF16), s.astype(BF16)], axis=1)
            acc = _dot(lhs, jnp.concatenate([c_prev.astype(BF16), va], axis=0))
            floor = jnp.exp(-rep[2 * d + REP_M, n_stat])
            out = acc[:, :HEAD_DIM] / jnp.maximum(jnp.abs(acc[:, HEAD_DIM:]), floor)
        c_loc = _dot((kt * stat(ST_W)).astype(BF16), va)
        return out, both(stat(ST_DEC)) * c_prev + both(stat(ST_INP)) * c_loc

    def finish(hsum, o_src, t0):
        y = _rms(hsum, on_ref[...]) * jax.nn.sigmoid(o_src[pl.ds(t0, L), :].astype(F32))
        return y.astype(BF16)

    zero_state = jnp.zeros((HEAD_DIM, ML_AUG), F32)

    def make_body(n_chunks, stat0, q_src, kt_src, va_src, hf_dst, hb_dst, want_out):
        def body(i, carry):
            fw, bw = carry
            nf, nb = i, n_chunks - 1 - i
            tf, tb = pl.multiple_of(nf * L, L), pl.multiple_of(nb * L, L)
            out_f, fw = chunk(0, stat0 + nf, tf, q_src, kt_src, va_src, fw, want_out)
            out_b, bw = chunk(1, stat0 + nb, tb, q_src, kt_src, va_src, bw, want_out)
            if want_out:
                hf_dst[pl.ds(tf, L), :] = out_f
                hb_dst[pl.ds(tb, L), :] = out_b
            return fw, bw
        return body

    carry = lax.fori_loop(0, n_ctx, make_body(n_ctx, n_lat, qa_c, kt_c, va_c, hf_c, hb_c, need_ctx),
                          (zero_state, zero_state))
    lax.fori_loop(0, n_lat, make_body(n_lat, 0, qa_l, kt_l, va_l, hf_l, hb_l, True), carry, unroll=4)

    def finish_lat(n, carry):
        t0 = pl.multiple_of(n * L, L)
        yl_ref[pl.ds(t0, L), :] = finish(hf_l[pl.ds(t0, L), :] + hb_l[pl.ds(t0, L), :], ol_ref, t0)
        return carry

    lax.fori_loop(0, n_lat, finish_lat, 0, unroll=4)
    if need_ctx:
        def finish_ctx(n, carry):
            t0 = pl.multiple_of(n * L, L)
            yc_ref[pl.ds(t0, L), :] = finish(hf_c[pl.ds(t0, L), :] + hb_c[pl.ds(t0, L), :], oc_ref, t0)
            return carry

        lax.fori_loop(0, n_ctx, finish_ctx, 0, unroll=True)


def _mlstm_call(p_lat, p_ctx, gates, gate_b, conv_w, conv_b, out_norm, batch, seq, ctx_len, need_ctx):
    n_lat, n_ctx = seq // ML_CHUNK, ctx_len // ML_CHUNK
    n_pad = gates.shape[2]
    cq, ck = COL_ML_Q // HEAD_DIM, COL_ML_K // HEAD_DIM
    cv, co = COL_ML_V // HEAD_DIM, COL_ML_O // HEAD_DIM
    col = lambda rows, c0: pl.BlockSpec((rows, HEAD_DIM), lambda b, h: (b, c0 + h))
    out_specs = [pl.BlockSpec((seq, HEAD_DIM), lambda b, h: (b, h))]
    out_shape = [jax.ShapeDtypeStruct((batch * seq, ML_W), BF16)]
    if need_ctx:
        out_specs.append(pl.BlockSpec((ctx_len, HEAD_DIM), lambda b, h: (b, h)))
        out_shape.append(jax.ShapeDtypeStruct((batch * ctx_len, ML_W), BF16))

    def stream_scratch(n_tok):
        return [pltpu.VMEM((n_tok, HEAD_DIM), BF16), pltpu.VMEM((HEAD_DIM, n_tok), F32),
                pltpu.VMEM((n_tok, ML_AUG), BF16)]

    res = pl.pallas_call(
        functools.partial(_mlstm_kernel, n_lat=n_lat, n_ctx=n_ctx, need_ctx=need_ctx),
        grid=(batch, ML_HEADS),
        in_specs=[col(seq, cq), col(seq, ck), col(seq, cv), col(seq, co),
                  col(ctx_len, cq), col(ctx_len, ck), col(ctx_len, cv), col(ctx_len, co),
                  pl.BlockSpec((N_GATES, None, n_pad, ML_CHUNK), lambda b, h: (0, b, 0, 0)),
                  pl.BlockSpec((N_GATES, 1, ML_CHUNK), lambda b, h: (0, 0, 0)),
                  pl.BlockSpec((3, HEAD_DIM), lambda b, h: (0, h)),
                  pl.BlockSpec((3, HEAD_DIM), lambda b, h: (0, ML_HEADS + h)),
                  pl.BlockSpec((1, HEAD_DIM), lambda b, h: (0, h)),
                  pl.BlockSpec((1, HEAD_DIM), lambda b, h: (0, ML_HEADS + h)),
                  pl.BlockSpec((1, HEAD_DIM), lambda b, h: (0, h))],
        out_specs=out_specs,
        out_shape=out_shape,
        scratch_shapes=(stream_scratch(seq) + stream_scratch(ctx_len)
                        + [pltpu.VMEM((seq, HEAD_DIM), F32), pltpu.VMEM((seq, HEAD_DIM), F32),
                           pltpu.VMEM((ctx_len, HEAD_DIM), F32), pltpu.VMEM((ctx_len, HEAD_DIM), F32),
                           pltpu.VMEM((2 * N_STATS, n_pad, ML_CHUNK), F32),
                           pltpu.VMEM((4, n_lat + n_ctx, ML_CHUNK, ML_CHUNK), F32)]),
        compiler_params=_cparams(("parallel", "parallel")),
        name="mlstm",
    )(p_lat, p_lat, p_lat, p_lat, p_ctx, p_ctx, p_ctx, p_ctx, gates, gate_b,
      conv_w, conv_w, conv_b, conv_b, out_norm)
    return (res[0], res[1]) if need_ctx else (res[0], None)


def kernel(x, c, ctx, c_ctx, w_ada, b_ada, norm_ff1, ff1_gate, ff1_up, ff1_down, norm_mix, w_in, na_rpb,
           gqa_q_norm, gqa_k_norm, ml_conv_w, ml_conv_b, ml_gate_b, ml_out_norm, w_out, norm_ff2,
           ff2_gate, ff2_up, ff2_down, final_norm):
    batch, seq, d = x.shape
    ctx_len = ctx.shape[1]
    depth = w_ada.shape[0]
    ctx_group = batch
    n_lat_chunks, n_ctx_chunks = seq // ML_CHUNK, ctx_len // ML_CHUNK
    n_chunk_pad = -(-(n_lat_chunks + n_ctx_chunks) // 8) * 8

    xl = x.reshape(batch * seq, d)
    xc = ctx.reshape(batch * ctx_len, d)
    cond = jnp.zeros((8, d), F32).at[:batch].set(c).at[batch].set(c_ctx)
    rope_tabs = _rope_tables(seq)
    row2 = lambda v: v.reshape(1, -1)
    fin = row2(final_norm)

    for l in range(depth):
        last = l == depth - 1
        mod = _ada_call(cond, w_ada, l, row2(b_ada[l])).reshape(8 * N_MOD, 1, d)
        lat = dict(group_rows=seq, fixed_group=None)
        cx = dict(group_rows=None, fixed_group=ctx_group)

        cast = lambda w: _cast_call(w, l, w.shape[2])
        wg, wu, wd = cast(ff1_gate), cast(ff1_up), cast(ff1_down)
        g1 = row2(norm_ff1[l])
        xl = _ffn_call(xl, mod, 0, g1, wg, wu, wd, fin, final_norm=False, **lat)
        xc = _ffn_call(xc, mod, 0, g1, wg, wu, wd, fin, final_norm=False, **cx)

        w_main = _cast_call(w_in, l, COL_GATES)
        w_gates = w_in[l][:, COL_GATES:].astype(BF16)
        gm = row2(norm_mix[l])
        p_lat, g_lat = _inproj_call(xl, mod, gm, w_main, w_gates, **lat)
        p_ctx, g_ctx = _inproj_call(xc, mod, gm, w_main, w_gates, **cx)
        gt_lat, gt_ctx = g_lat.T, g_ctx.T

        ya_lat = _na_call(p_lat, p_ctx, *_na_toeplitz_slabs(na_rpb[l]), batch, seq, ctx_len)
        qg, kg = row2(gqa_q_norm[l]), row2(gqa_k_norm[l])
        qn_lat, kn_lat = _gqa_prep_call(p_lat, qg, kg, rope_tabs, seq, rope=True)
        qn_ctx, kn_ctx = _gqa_prep_call(p_ctx, qg, kg, rope_tabs, seq, rope=False, tm=ctx_len)
        yb_lat = _gqa_call(qn_lat, kn_lat, p_lat, kn_ctx, p_ctx, batch, seq, ctx_len)
        gates = jnp.concatenate(
            [gt_lat.reshape(N_GATES, batch, n_lat_chunks, ML_CHUNK),
             gt_ctx.reshape(N_GATES, batch, n_ctx_chunks, ML_CHUNK),
             jnp.zeros((N_GATES, batch, n_chunk_pad - n_lat_chunks - n_ctx_chunks, ML_CHUNK), F32)], axis=2)
        gate_b = jnp.broadcast_to(ml_gate_b[l][:, None, None], (N_GATES, 1, ML_CHUNK))
        yc_lat, yc_ctx = _mlstm_call(p_lat, p_ctx, gates, gate_b, ml_conv_w[l], row2(ml_conv_b[l]),
                                     row2(ml_out_norm[l]), batch, seq, ctx_len, not last)

        wo = cast(w_out)
        xl = _outproj_call(xl, mod, ya_lat, yb_lat, yc_lat, wo, **lat)
        wg, wu, wd = cast(ff2_gate), cast(ff2_up), cast(ff2_down)
        g2 = row2(norm_ff2[l])
        xl = _ffn_call(xl, mod, 6, g2, wg, wu, wd, fin, final_norm=last, **lat)
        if not last:
            ya_ctx = _ctx_attn_call(p_ctx, p_ctx, p_ctx, batch, ctx_len, NA_HEADS,
                                    COL_NA_Q // HEAD_DIM, COL_NA_K // HEAD_DIM, COL_NA_V // HEAD_DIM, 1,
                                    HEAD_DIM ** -0.5)
            yb_ctx = _ctx_attn_call(qn_ctx, kn_ctx, p_ctx, batch, ctx_len, GQA_Q_HEADS, 0, 0,
                                    COL_GQ_V // HEAD_DIM, GQA_Q_HEADS // GQA_KV_HEADS, None)
            xc = _outproj_call(xc, mod, ya_ctx, yb_ctx, yc_ctx, wo, **cx)
            xc = _ffn_call(xc, mod, 6, g2, wg, wu, wd, fin, final_norm=False, **cx)
    return xl.reshape(batch, seq, d)
```

```python
import functools

import numpy as np
import jax
import jax.numpy as jnp
from jax import lax
from jax.experimental import pallas as pl
from jax.experimental.pallas import tpu as pltpu

F32 = jnp.float32
BF16 = jnp.bfloat16

GRID_W = 64
HEAD_DIM = 128
NA_HEADS = 4
GQA_Q_HEADS = 4
GQA_KV_HEADS = 2
ML_HEADS = 8
NA_W = NA_HEADS * HEAD_DIM
GQA_W = GQA_Q_HEADS * HEAD_DIM
KV_W = GQA_KV_HEADS * HEAD_DIM
ML_W = ML_HEADS * HEAD_DIM
NA_WIN_R = 8
NA_WIN_C = 16
ML_CHUNK = 128
ROPE_THETA = 10000.0
N_MOD = 9
EPS = 1e-6
N_GATES = 4 * ML_HEADS
COL_NA_Q = 0
COL_NA_K = NA_W
COL_NA_V = 2 * NA_W
COL_GQ_Q = 3 * NA_W
COL_GQ_K = COL_GQ_Q + GQA_W
COL_GQ_V = COL_GQ_K + KV_W
COL_ML_Q = COL_GQ_V + KV_W
COL_ML_K = COL_ML_Q + ML_W
COL_ML_V = COL_ML_K + ML_W
COL_ML_O = COL_ML_V + ML_W
COL_GATES = COL_ML_O + ML_W

NEG_BIG = -1e30
VMEM_LIMIT_V7X = 56 * 1024 * 1024
BF16_ROWS = 16
NORM_ROWS = 32
FFN_SUBTILE = 256
FFN_OUT_SUBTILE = 1024

NA_QROWS = 8
NA_KROWS = 16
NA_KEY_TILE = 256
NA_MASKED_SLAB = 2 * NA_WIN_R - 1


def _cparams(sem):
    return pltpu.CompilerParams(dimension_semantics=sem, vmem_limit_bytes=VMEM_LIMIT_V7X)


def _rms(x, gain):
    return x * lax.rsqrt(jnp.mean(x * x, axis=-1, keepdims=True) + EPS) * gain


def _norm_modulate_store(x_ref, gain_ref, scale_ref, shift_ref, h_ref):
    gain_scale = gain_ref[...] * (1.0 + scale_ref[...])
    shift = shift_ref[...]

    def body(i, carry):
        r0 = pl.multiple_of(i * NORM_ROWS, NORM_ROWS)
        x = x_ref[pl.ds(r0, NORM_ROWS), :]
        inv = lax.rsqrt(jnp.mean(x * x, axis=-1, keepdims=True) + EPS)
        h_ref[pl.ds(r0, NORM_ROWS), :] = (x * inv * gain_scale + shift).astype(h_ref.dtype)
        return carry

    lax.fori_loop(0, x_ref.shape[0] // NORM_ROWS, body, 0, unroll=2)


def _dot(a, b):
    return jnp.dot(a, b, preferred_element_type=F32)


def _dot_nt(a, b):
    return lax.dot_general(a, b, (((1,), (1,)), ((), ())), preferred_element_type=F32)


def _dot_tn(a, b):
    return lax.dot_general(a, b, (((0,), (0,)), ((), ())), preferred_element_type=F32)


def _cast_kernel(w_ref, o_ref):
    o_ref[...] = w_ref[...].astype(o_ref.dtype)


def _cast_call(w_stack, layer, n_rows=None, row_blocks=8):
    _, rows, n_cols = w_stack.shape
    n_rows = rows if n_rows is None else n_rows
    tr = n_rows // row_blocks
    return pl.pallas_call(
        _cast_kernel,
        grid=(row_blocks,),
        in_specs=[pl.BlockSpec((None, tr, n_cols), lambda i: (layer, i, 0))],
        out_specs=pl.BlockSpec((tr, n_cols), lambda i: (i, 0)),
        out_shape=jax.ShapeDtypeStruct((n_rows, n_cols), BF16),
        compiler_params=_cparams(("parallel",)),
        name="weight_cast",
    )(w_stack)


def _ada_kernel(c_ref, w_ref, b_ref, o_ref):
    c = c_ref[...]
    a = (c * jax.nn.sigmoid(c)).astype(BF16)
    o_ref[...] = _dot(a, w_ref[...].astype(BF16)) + b_ref[...]


def _ada_call(cond, w_stack, layer, b, tn=1024):
    m, d = cond.shape
    n = w_stack.shape[2]
    return pl.pallas_call(
        _ada_kernel,
        grid=(n // tn,),
        in_specs=[pl.BlockSpec((m, d), lambda j: (0, 0)),
                  pl.BlockSpec((None, d, tn), lambda j: (layer, 0, j)),
                  pl.BlockSpec((1, tn), lambda j: (0, j))],
        out_specs=pl.BlockSpec((m, tn), lambda j: (0, j)),
        out_shape=jax.ShapeDtypeStruct((m, n), F32),
        compiler_params=_cparams(("arbitrary",)),
        name="ada_mod",
    )(cond, w_stack, b)


def _mod_spec(k, tm, group_rows, fixed_group):
    def index_map(i, *_):
        grp = fixed_group if fixed_group is not None else i // (group_rows // tm)
        return (grp * N_MOD + k, 0, 0)
    return index_map


def _ffn_kernel(x_ref, sh_ref, sc_ref, gt_ref, gain_ref, wg_ref, wu_ref, wd_ref, fin_ref, o_ref,
                h_scr, *, n_ff, final_norm):
    j = pl.program_id(1)

    @pl.when(j == 0)
    def _():
        _norm_modulate_store(x_ref, gain_ref, sc_ref, sh_ref, h_scr)
        o_ref[...] = jnp.zeros_like(o_ref)

    h = h_scr[...]
    tf = wg_ref.shape[1]
    for c0 in range(0, tf, FFN_SUBTILE):
        g = _dot(h, wg_ref[:, c0:c0 + FFN_SUBTILE])
        u = _dot(h, wu_ref[:, c0:c0 + FFN_SUBTILE])
        a = ((g * jax.nn.sigmoid(g)) * u).astype(BF16)
        for n0 in range(0, o_ref.shape[1], FFN_OUT_SUBTILE):
            o_ref[:, n0:n0 + FFN_OUT_SUBTILE] += _dot(a, wd_ref[c0:c0 + FFN_SUBTILE, n0:n0 + FFN_OUT_SUBTILE])

    @pl.when(j == n_ff - 1)
    def _():
        half_gate = 0.5 * gt_ref[...]

        def body(i, carry):
            r0 = pl.multiple_of(i * NORM_ROWS, NORM_ROWS)
            out = x_ref[pl.ds(r0, NORM_ROWS), :] + half_gate * o_ref[pl.ds(r0, NORM_ROWS), :]
            if final_norm:
                out = _rms(out, fin_ref[...])
            o_ref[pl.ds(r0, NORM_ROWS), :] = out
            return carry

        lax.fori_loop(0, o_ref.shape[0] // NORM_ROWS, body, 0, unroll=2)


def _ffn_call(x, mod, k0, gain, wg, wu, wd, fin, *, group_rows, fixed_group, final_norm,
              tm=1024, tf=512):
    t, d = x.shape
    dff = wg.shape[1]
    n_ff = dff // tf
    mspec = lambda k: pl.BlockSpec((None, 1, d), _mod_spec(k, tm, group_rows, fixed_group))
    return pl.pallas_call(
        functools.partial(_ffn_kernel, n_ff=n_ff, final_norm=final_norm),
        grid=(t // tm, n_ff),
        in_specs=[pl.BlockSpec((tm, d), lambda i, j: (i, 0)),
                  mspec(k0), mspec(k0 + 1), mspec(k0 + 2),
                  pl.BlockSpec((1, d), lambda i, j: (0, 0)),
                  pl.BlockSpec((d, tf), lambda i, j: (0, j)),
                  pl.BlockSpec((d, tf), lambda i, j: (0, j)),
                  pl.BlockSpec((tf, d), lambda i, j: (j, 0)),
                  pl.BlockSpec((1, d), lambda i, j: (0, 0))],
        out_specs=pl.BlockSpec((tm, d), lambda i, j: (i, 0)),
        out_shape=jax.ShapeDtypeStruct((t, d), F32),
        scratch_shapes=[pltpu.VMEM((tm, d), BF16)],
        compiler_params=_cparams(("parallel", "arbitrary")),
        name="macaron_ffn",
    )(x, mod, mod, mod, gain, wg, wu, wd, fin)


def _inproj_kernel(x_ref, sh_ref, sc_ref, gain_ref, wt_ref, wgt_ref, p_ref, gt_ref, h_scr):
    j = pl.program_id(1)

    @pl.when(j == 0)
    def _():
        h = _rms(x_ref[...], gain_ref[...]) * (1.0 + sc_ref[...]) + sh_ref[...]
        hb = h.astype(BF16)
        h_scr[...] = hb
        gt_ref[...] = _dot_nt(wgt_ref[...], hb)

    p_ref[...] = _dot_nt(h_scr[...], wt_ref[...]).astype(p_ref.dtype)


def _inproj_call(x, mod, gain, w_main_t, w_gates_t, *, group_rows, fixed_group, tm=1024, tn=1664):
    t, d = x.shape
    n = w_main_t.shape[0]
    mspec = lambda k: pl.BlockSpec((None, 1, d), _mod_spec(k, tm, group_rows, fixed_group))
    return pl.pallas_call(
        _inproj_kernel,
        grid=(t // tm, n // tn),
        in_specs=[pl.BlockSpec((tm, d), lambda i, j: (i, 0)),
                  mspec(3), mspec(4),
                  pl.BlockSpec((1, d), lambda i, j: (0, 0)),
                  pl.BlockSpec((tn, d), lambda i, j: (j, 0)),
                  pl.BlockSpec((N_GATES, d), lambda i, j: (0, 0))],
        out_specs=[pl.BlockSpec((tm, tn), lambda i, j: (i, j)),
                   pl.BlockSpec((N_GATES, tm), lambda i, j: (0, i))],
        out_shape=[jax.ShapeDtypeStruct((t, n), BF16),
                   jax.ShapeDtypeStruct((N_GATES, t), F32)],
        scratch_shapes=[pltpu.VMEM((tm, d), BF16)],
        compiler_params=_cparams(("parallel", "arbitrary")),
        name="mixer_in_proj",
    )(x, mod, mod, gain, w_main_t, w_gates_t)


def _outproj_kernel(x_ref, gt_ref, ya_ref, yb_ref, yc_ref, w_ref, o_ref):
    acc = _dot(ya_ref[...], w_ref[0:NA_W, :])
    acc += _dot(yb_ref[...], w_ref[NA_W:NA_W + GQA_W, :])
    acc += _dot(yc_ref[...], w_ref[NA_W + GQA_W:, :])
    o_ref[...] = x_ref[...] + gt_ref[...] * acc


def _outproj_call(x, mod, ya, yb, yc, w, *, group_rows, fixed_group, tm=512):
    t, d = x.shape
    return pl.pallas_call(
        _outproj_kernel,
        grid=(t // tm,),
        in_specs=[pl.BlockSpec((tm, d), lambda i: (i, 0)),
                  pl.BlockSpec((None, 1, d), _mod_spec(5, tm, group_rows, fixed_group)),
                  pl.BlockSpec((tm, NA_W), lambda i: (i, 0)),
                  pl.BlockSpec((tm, GQA_W), lambda i: (i, 0)),
                  pl.BlockSpec((tm, ML_W), lambda i: (i, 0)),
                  pl.BlockSpec((d, d), lambda i: (0, 0))],
        out_specs=pl.BlockSpec((tm, d), lambda i: (i, 0)),
        out_shape=jax.ShapeDtypeStruct((t, d), F32),
        compiler_params=_cparams(("parallel",)),
        name="mixer_out_proj",
    )(x, mod, ya, yb, yc, w)


def _na_kernel(q_ref, k_ref, v_ref, kc_ref, vc_ref, tl_ref, tr_ref, o_ref, mb_scr, *, rows):
    rb = pl.program_id(2)
    nblk = rows // NA_QROWS
    key_row0 = jnp.clip(rb * NA_QROWS - NA_WIN_R // 2, 0, rows - NA_KROWS)

    @pl.when((rb <= 1) | (rb == nblk - 1))
    def _():
        for a in range(NA_QROWS):
            r = rb * NA_QROWS + a
            row_start = jnp.clip(r - NA_WIN_R // 2, 0, rows - NA_WIN_R)
            for j in range(NA_KROWS // 2):
                sel = []
                for kr in (key_row0 + 2 * j, key_row0 + 2 * j + 1):
                    in_win = (kr >= row_start) & (kr < row_start + NA_WIN_R)
                    sel.append(jnp.where(in_win, kr - r + (NA_WIN_R - 1), NA_MASKED_SLAB))
                mb_scr[a * GRID_W:(a + 1) * GRID_W, 2 * j * GRID_W:(2 * j + 2) * GRID_W] = (
                    tl_ref[sel[0]] + tr_ref[sel[1]])

    start = pl.multiple_of(key_row0 * GRID_W, GRID_W)
    q = (q_ref[...].astype(F32) * (HEAD_DIM ** -0.5)).astype(BF16)
    s = _dot_nt(q, kc_ref[...])
    m = jnp.max(s, axis=-1, keepdims=True)
    p = jnp.exp(s - m)
    den = jnp.sum(p, axis=-1, keepdims=True)
    acc = _dot(p.astype(BF16), vc_ref[...])
    for t0 in range(0, NA_KROWS * GRID_W, NA_KEY_TILE):
        s = _dot_nt(q, k_ref[pl.ds(start + t0, NA_KEY_TILE), :]) + mb_scr[:, t0:t0 + NA_KEY_TILE]
        m_new = jnp.maximum(m, jnp.max(s, axis=-1, keepdims=True))
        p = jnp.exp(s - m_new)
        alpha = jnp.exp(m - m_new)
        den = alpha * den + jnp.sum(p, axis=-1, keepdims=True)
        acc = alpha * acc + _dot(p.astype(BF16), v_ref[pl.ds(start + t0, NA_KEY_TILE), :])
        m = m_new
    o_ref[...] = (acc / den).astype(o_ref.dtype)


def _na_toeplitz_slabs(rpb):
    n_h, n_dr, n_dc = rpb.shape
    edge = GRID_W - NA_WIN_C
    ext = jnp.concatenate(
        [rpb[..., NA_WIN_C - 1:], jnp.broadcast_to(rpb[..., n_dc - 1:], (n_h, n_dr, edge)),
         jnp.zeros((n_h, n_dr, 1), F32),
         jnp.broadcast_to(rpb[..., :1], (n_h, n_dr, edge)), rpb[..., :NA_WIN_C - 1]], axis=-1)
    width = 2 * GRID_W
    tiled = jnp.broadcast_to(ext[:, :, None, :], (n_h, n_dr, GRID_W, width)).reshape(n_h, n_dr, GRID_W * width)
    toep = tiled[..., :GRID_W * (width - 1)].reshape(n_h, n_dr, GRID_W, width - 1)[..., :GRID_W]
    cq = np.arange(GRID_W)
    col_start = np.clip(cq - NA_WIN_C // 2, 0, GRID_W - NA_WIN_C)
    col_ok = (cq[None, :] >= col_start[:, None]) & (cq[None, :] < col_start[:, None] + NA_WIN_C)
    slabs = jnp.where(col_ok, toep, NEG_BIG)
    slabs = jnp.concatenate([slabs, jnp.full((n_h, 1, GRID_W, GRID_W), NEG_BIG, F32)], axis=1)
    zeros = jnp.zeros_like(slabs)
    return jnp.concatenate([slabs, zeros], axis=-1), jnp.concatenate([zeros, slabs], axis=-1)


def _na_call(p_lat, p_ctx, slabs_left, slabs_right, batch, seq, ctx_len):
    rows = seq // GRID_W
    nblk = rows // NA_QROWS
    tq = NA_QROWS * GRID_W
    nk = NA_KROWS * GRID_W
    cq, ck, cv = COL_NA_Q // HEAD_DIM, COL_NA_K // HEAD_DIM, COL_NA_V // HEAD_DIM
    slab_spec = pl.BlockSpec((None, NA_MASKED_SLAB + 1, GRID_W, 2 * GRID_W), lambda b, h, r: (h, 0, 0, 0))

    return pl.pallas_call(
        functools.partial(_na_kernel, rows=rows),
        grid=(batch, NA_HEADS, nblk),
        in_specs=[pl.BlockSpec((tq, HEAD_DIM), lambda b, h, r: (b * nblk + r, cq + h)),
                  pl.BlockSpec((seq, HEAD_DIM), lambda b, h, r: (b, ck + h)),
                  pl.BlockSpec((seq, HEAD_DIM), lambda b, h, r: (b, cv + h)),
                  pl.BlockSpec((ctx_len, HEAD_DIM), lambda b, h, r: (b, ck + h)),
                  pl.BlockSpec((ctx_len, HEAD_DIM), lambda b, h, r: (b, cv + h)),
                  slab_spec, slab_spec],
        out_specs=pl.BlockSpec((tq, HEAD_DIM), lambda b, h, r: (b * nblk + r, h)),
        out_shape=jax.ShapeDtypeStruct((batch * seq, NA_W), BF16),
        scratch_shapes=[pltpu.VMEM((tq, nk), F32)],
        compiler_params=_cparams(("parallel", "parallel", "arbitrary")),
        name="neighbourhood_attn",
    )(p_lat, p_lat, p_lat, p_ctx, p_ctx, slabs_left, slabs_right)


def _ctx_attn_kernel(q_ref, k_ref, v_ref, o_ref, *, scale):
    s = _dot_nt(q_ref[...], k_ref[...])
    if scale is not None:
        s = s * scale
    m = jnp.max(s, axis=-1, keepdims=True)
    p = jnp.exp(s - m)
    den = jnp.sum(p, axis=-1, keepdims=True)
    o_ref[...] = (_dot(p.astype(BF16), v_ref[...]) / den).astype(o_ref.dtype)


def _ctx_attn_call(q_arr, k_arr, v_arr, batch, ctx_len, n_heads, q_col, k_col, v_col, kv_group, scale):
    return pl.pallas_call(
        functools.partial(_ctx_attn_kernel, scale=scale),
        grid=(batch, n_heads),
        in_specs=[pl.BlockSpec((ctx_len, HEAD_DIM), lambda b, h: (b, q_col + h)),
                  pl.BlockSpec((ctx_len, HEAD_DIM), lambda b, h: (b, k_col + h // kv_group)),
                  pl.BlockSpec((ctx_len, HEAD_DIM), lambda b, h: (b, v_col + h // kv_group))],
        out_specs=pl.BlockSpec((ctx_len, HEAD_DIM), lambda b, h: (b, h)),
        out_shape=jax.ShapeDtypeStruct((batch * ctx_len, n_heads * HEAD_DIM), BF16),
        compiler_params=_cparams(("parallel", "parallel")),
        name="ctx_attn",
    )(q_arr, k_arr, v_arr)


def _rope(x, cos, sin_lo, sin_hi):
    return (x * cos + pltpu.roll(x, HEAD_DIM - HEAD_DIM // 4, 1) * sin_lo
            + pltpu.roll(x, HEAD_DIM // 4, 1) * sin_hi)


def _gqa_prep_kernel(q_ref, k_ref, qg_ref, kg_ref, cos_ref, slo_ref, shi_ref, qo_ref, ko_ref, *, rope):
    def prep(x, gain, scale):
        y = _rms(x.astype(F32), gain)
        if rope:
            y = _rope(y, cos_ref[...], slo_ref[...], shi_ref[...])
        if scale is not None:
            y = y * scale
        return y.astype(BF16)

    for h in range(GQA_Q_HEADS):
        sl = slice(h * HEAD_DIM, (h + 1) * HEAD_DIM)
        qo_ref[:, sl] = prep(q_ref[:, sl], qg_ref[...], HEAD_DIM ** -0.5)
    for h in range(GQA_KV_HEADS):
        sl = slice(h * HEAD_DIM, (h + 1) * HEAD_DIM)
        ko_ref[:, sl] = prep(k_ref[:, sl], kg_ref[...], None)


def _rope_tables(n_tokens):
    t = jnp.arange(n_tokens)
    row = (t // GRID_W).astype(F32)
    col = (t % GRID_W).astype(F32)
    half = HEAD_DIM // 2
    inv_freq = 1.0 / (ROPE_THETA ** (jnp.arange(0, half, 2, dtype=F32) / half))
    ang_r = row[:, None] * inv_freq[None, :]
    ang_c = col[:, None] * inv_freq[None, :]
    ang = jnp.concatenate([ang_r, ang_r, ang_c, ang_c], axis=-1)
    cos, sin = jnp.cos(ang), jnp.sin(ang)
    lo = (jnp.arange(HEAD_DIM) % half) < (half // 2)
    return cos, jnp.where(lo, -sin, 0.0), jnp.where(lo, 0.0, sin)


def _gqa_prep_call(p, q_gain, k_gain, tables, seq, *, rope, tm=512):
    t = p.shape[0]
    per_seq = seq // tm if rope else 1
    tab_spec = pl.BlockSpec((tm, HEAD_DIM), lambda i: (i % per_seq, 0))
    return pl.pallas_call(
        functools.partial(_gqa_prep_kernel, rope=rope),
        grid=(t // tm,),
        in_specs=[pl.BlockSpec((tm, GQA_W), lambda i: (i, COL_GQ_Q // GQA_W)),
                  pl.BlockSpec((tm, KV_W), lambda i: (i, COL_GQ_K // KV_W)),
                  pl.BlockSpec((1, HEAD_DIM), lambda i: (0, 0)),
                  pl.BlockSpec((1, HEAD_DIM), lambda i: (0, 0)),
                  tab_spec, tab_spec, tab_spec],
        out_specs=[pl.BlockSpec((tm, GQA_W), lambda i: (i, 0)),
                   pl.BlockSpec((tm, KV_W), lambda i: (i, 0))],
        out_shape=[jax.ShapeDtypeStruct((t, GQA_W), BF16),
                   jax.ShapeDtypeStruct((t, KV_W), BF16)],
        compiler_params=_cparams(("parallel",)),
        name="gqa_prep",
    )(p, p, q_gain, k_gain, *tables)


def _gqa_kernel(q_ref, kl_ref, vl_ref, kc_ref, vc_ref, o_ref, *, tk):
    tq = q_ref.shape[0]
    grp = q_ref.shape[1] // HEAD_DIM
    q = jnp.concatenate([q_ref[:, g * HEAD_DIM:(g + 1) * HEAD_DIM] for g in range(grp)], axis=0)

    s = _dot_nt(q, kc_ref[...])
    m = jnp.max(s, axis=-1, keepdims=True)
    p = jnp.exp(s - m)
    den = jnp.sum(p, axis=-1, keepdims=True)
    acc = _dot(p.astype(BF16), vc_ref[...])

    def tile(i, carry):
        m, den, acc = carry
        t0 = pl.multiple_of(i * tk, tk)
        s = _dot_nt(q, kl_ref[pl.ds(t0, tk), :])
        m_new = jnp.maximum(m, jnp.max(s, axis=-1, keepdims=True))
        p = jnp.exp(s - m_new)
        alpha = jnp.exp(m - m_new)
        den = alpha * den + jnp.sum(p, axis=-1, keepdims=True)
        acc = alpha * acc + _dot(p.astype(BF16), vl_ref[pl.ds(t0, tk), :])
        return m_new, den, acc

    m, den, acc = lax.fori_loop(0, kl_ref.shape[0] // tk, tile, (m, den, acc), unroll=True)
    out = (acc / den).astype(o_ref.dtype)
    for g in range(grp):
        o_ref[:, g * HEAD_DIM:(g + 1) * HEAD_DIM] = out[g * tq:(g + 1) * tq]


def _gqa_call(q_lat, k_lat, p_lat, k_ctx, p_ctx, batch, seq, ctx_len, tq=512, tk=512):
    grp = GQA_Q_HEADS // GQA_KV_HEADS
    nq = seq // tq
    cv = COL_GQ_V // HEAD_DIM
    return pl.pallas_call(
        functools.partial(_gqa_kernel, tk=tk),
        grid=(batch, GQA_KV_HEADS, nq),
        in_specs=[pl.BlockSpec((tq, grp * HEAD_DIM), lambda b, h, i: (b * nq + i, h)),
                  pl.BlockSpec((seq, HEAD_DIM), lambda b, h, i: (b, h)),
                  pl.BlockSpec((seq, HEAD_DIM), lambda b, h, i: (b, cv + h)),
                  pl.BlockSpec((ctx_len, HEAD_DIM), lambda b, h, i: (b, h)),
                  pl.BlockSpec((ctx_len, HEAD_DIM), lambda b, h, i: (b, cv + h))],
        out_specs=pl.BlockSpec((tq, grp * HEAD_DIM), lambda b, h, i: (b * nq + i, h)),
        out_shape=jax.ShapeDtypeStruct((batch * seq, GQA_W), BF16),
        compiler_params=_cparams(("parallel", "parallel", "arbitrary")),
        name="gqa_attn",
    )(q_lat, k_lat, p_lat, k_ctx, p_ctx)


ST_R, ST_W, ST_BTOT, ST_AMAX, ST_MPREV, ST_DEC, ST_INP, ST_BIGR, ST_M = range(9)
N_STATS = 9
REP_R, REP_M = range(2)
ML_AUG = 2 * HEAD_DIM


def _mlstm_kernel(ql_ref, kl_ref, vl_ref, ol_ref, qc_ref, kc_ref, vc_ref, oc_ref, g_ref, gb_ref,
                  cwq_ref, cwk_ref, cbq_ref, cbk_ref, on_ref, yl_ref, *rest,
                  n_lat, n_ctx, need_ctx):
    if need_ctx:
        yc_ref = rest[0]
        rest = rest[1:]
    qa_l, kt_l, va_l, qa_c, kt_c, va_c, hf_l, hb_l, hf_c, hb_c, st, rep = rest
    L = ML_CHUNK
    n_all = n_lat + n_ctx
    h = pl.program_id(1)
    row = lax.broadcasted_iota(jnp.int32, (L, L), 0)
    col = lax.broadcasted_iota(jnp.int32, (L, L), 1)

    tri_fw = (row <= col).astype(F32)
    tri_bw = (row >= col).astype(F32)
    b_cum = []
    for d, tri in enumerate((tri_fw, tri_bw)):
        log_i = g_ref[2 * d * ML_HEADS + h] + gb_ref[2 * d * ML_HEADS + h]
        log_f = jax.nn.log_sigmoid(g_ref[(2 * d + 1) * ML_HEADS + h] + gb_ref[(2 * d + 1) * ML_HEADS + h])
        b = jnp.dot(log_f, tri, preferred_element_type=F32, precision=lax.Precision.HIGHEST)
        b_tot = b[:, L - 1:L] if d == 0 else b[:, 0:1]
        a = b_tot - b + log_i
        a_max = jnp.max(a, axis=-1, keepdims=True)
        r = log_i - b
        lane = lax.broadcasted_iota(jnp.int32, r.shape, 1)
        run = r
        shift = 1
        while shift < L:
            if d == 0:
                run = jnp.maximum(run, jnp.where(lane >= shift, pltpu.roll(run, shift, 1), NEG_BIG))
            else:
                run = jnp.maximum(run, jnp.where(lane < L - shift, pltpu.roll(run, L - shift, 1), NEG_BIG))
            shift *= 2
        base = d * N_STATS
        st[base + ST_R] = r
        st[base + ST_W] = jnp.exp(a - a_max)
        st[base + ST_BTOT] = jnp.broadcast_to(b_tot, b.shape)
        st[base + ST_AMAX] = jnp.broadcast_to(a_max, b.shape)
        st[base + ST_BIGR] = run
        st[base + ST_MPREV] = jnp.zeros_like(b)
        b_cum.append(b)

    def stabiliser_step(i, carry):
        new = []
        for d, m_prev in enumerate(carry):
            in_ctx = i < n_ctx
            j = jnp.where(in_ctx, i, i - n_ctx)
            if d == 0:
                n = jnp.where(in_ctx, n_lat + j, j)
            else:
                n = jnp.where(in_ctx, n_lat + n_ctx - 1 - j, n_lat - 1 - j)
            base = d * N_STATS
            b_tot = st[base + ST_BTOT, pl.ds(n, 1), :]
            a_max = st[base + ST_AMAX, pl.ds(n, 1), :]
            m_new = jnp.maximum(b_tot + m_prev, a_max)
            st[base + ST_MPREV, pl.ds(n, 1), :] = m_prev
            st[base + ST_DEC, pl.ds(n, 1), :] = jnp.exp(b_tot + m_prev - m_new)
            st[base + ST_INP, pl.ds(n, 1), :] = jnp.exp(a_max - m_new)
            new.append(m_new)
        return tuple(new)

    lax.fori_loop(0, n_all, stabiliser_step, (jnp.zeros((1, L), F32), jnp.zeros((1, L), F32)))

    for d in range(2):
        base = d * N_STATS
        big_r = jnp.maximum(st[base + ST_BIGR], st[base + ST_MPREV])
        st[base + ST_BIGR] = big_r
        st[base + ST_M] = b_cum[d] + big_r

    def column_tables(n, carry):
        for d in range(2):
            for k_rep, k_st in ((REP_R, ST_BIGR), (REP_M, ST_M)):
                rows = jnp.broadcast_to(st[d * N_STATS + k_st, pl.ds(n, 1), :], (L, L))
                rep[2 * d + k_rep, n] = rows.T
        return carry

    lax.fori_loop(0, n_all, column_tables, 0, unroll=2)

    def prepare_stream(q_src, k_src, v_src, q_dst, kt_dst, va_dst, n_chunks):
        n_tok = n_chunks * L
        sub = lax.broadcasted_iota(jnp.int32, (L, HEAD_DIM), 0)

        def conv_silu(src_ref, w_ref, b_ref, n, t0):
            x = src_ref[pl.ds(t0, L), :].astype(F32)
            prev_t0 = pl.multiple_of(jnp.maximum(t0 - BF16_ROWS, 0), BF16_ROWS)
            next_t0 = pl.multiple_of(jnp.minimum(t0 + L, n_tok - BF16_ROWS), BF16_ROWS)
            prev_row = (src_ref[pl.ds(prev_t0, BF16_ROWS), :].astype(F32)[BF16_ROWS - 1:, :]
                        * jnp.where(n > 0, 1.0, 0.0))
            next_row = (src_ref[pl.ds(next_t0, BF16_ROWS), :].astype(F32)[:1, :]
                        * jnp.where(n < n_chunks - 1, 1.0, 0.0))
            x_prev = jnp.where(sub == 0, prev_row, pltpu.roll(x, 1, 0))
            x_next = jnp.where(sub == L - 1, next_row, pltpu.roll(x, L - 1, 0))
            y = w_ref[0:1, :] * x_prev + w_ref[1:2, :] * x + w_ref[2:3, :] * x_next + b_ref[...]
            return y * jax.nn.sigmoid(y)

        def body(n, carry):
            t0 = pl.multiple_of(n * L, L)
            q_dst[pl.ds(t0, L), :] = conv_silu(q_src, cwq_ref, cbq_ref, n, t0).astype(BF16)
            k_act = conv_silu(k_src, cwk_ref, cbk_ref, n, t0) * (HEAD_DIM ** -0.5)
            kt_dst[:, pl.ds(t0, L)] = k_act.T
            va_dst[pl.ds(t0, L), 0:HEAD_DIM] = v_src[pl.ds(t0, L), :]
            va_dst[pl.ds(t0, L), HEAD_DIM:ML_AUG] = jnp.ones((L, HEAD_DIM), BF16)
            return carry

        lax.fori_loop(0, n_chunks, body, 0, unroll=2)

    prepare_stream(ql_ref, kl_ref, vl_ref, qa_l, kt_l, va_l, n_lat)
    prepare_stream(qc_ref, kc_ref, vc_ref, qa_c, kt_c, va_c, n_ctx)

    def chunk(d, n_stat, t0, q_src, kt_src, va_src, c_prev, want_out):
        base = d * N_STATS
        stat = lambda k: st[base + k, pl.ds(n_stat, 1), :]
        both = lambda x: jnp.concatenate([x, x], axis=-1)
        kt = kt_src[:, pl.ds(t0, L)]
        va = va_src[pl.ds(t0, L), :]
        out = None
        if want_out:
            q = q_src[pl.ds(t0, L), :]
            seen = (col <= row) if d == 0 else (col >= row)
            big_r = rep[2 * d + REP_R, n_stat]
            dec_mat = jnp.exp(jnp.where(seen, stat(ST_R) - big_r, NEG_BIG))
            s = _dot(q, kt.astype(BF16)) * dec_mat
            g = jnp.exp(stat(ST_MPREV) - big_r)
            lhs = jnp.concatenate([(g * q.astype(F32)).astype(BF16), s.astype(BF16)], axis=1)
            acc = _dot(lhs, jnp.concatenate([c_prev.astype(BF16), va], axis=0))
            floor = jnp.exp(-rep[2 * d + REP_M, n_stat])
            out = acc[:, :HEAD_DIM] / jnp.maximum(jnp.abs(acc[:, HEAD_DIM:]), floor)
        c_loc = _dot((kt * stat(ST_W)).astype(BF16), va)
        return out, both(stat(ST_DEC)) * c_prev + both(stat(ST_INP)) * c_loc

    def finish(hsum, o_src, t0):
        y = _rms(hsum, on_ref[...]) * jax.nn.sigmoid(o_src[pl.ds(t0, L), :].astype(F32))
        return y.astype(BF16)

    zero_state = jnp.zeros((HEAD_DIM, ML_AUG), F32)

    def make_body(n_chunks, stat0, q_src, kt_src, va_src, hf_dst, hb_dst, want_out):
        def body(i, carry):
            fw, bw = carry
            nf, nb = i, n_chunks - 1 - i
            tf, tb = pl.multiple_of(nf * L, L), pl.multiple_of(nb * L, L)
            out_f, fw = chunk(0, stat0 + nf, tf, q_src, kt_src, va_src, fw, want_out)
            out_b, bw = chunk(1, stat0 + nb, tb, q_src, kt_src, va_src, bw, want_out)
            if want_out:
                hf_dst[pl.ds(tf, L), :] = out_f
                hb_dst[pl.ds(tb, L), :] = out_b
            return fw, bw
        return body

    carry = lax.fori_loop(0, n_ctx, make_body(n_ctx, n_lat, qa_c, kt_c, va_c, hf_c, hb_c, need_ctx),
                          (zero_state, zero_state))
    lax.fori_loop(0, n_lat, make_body(n_lat, 0, qa_l, kt_l, va_l, hf_l, hb_l, True), carry, unroll=4)

    def finish_lat(n, carry):
        t0 = pl.multiple_of(n * L, L)
        yl_ref[pl.ds(t0, L), :] = finish(hf_l[pl.ds(t0, L), :] + hb_l[pl.ds(t0, L), :], ol_ref, t0)
        return carry

    lax.fori_loop(0, n_lat, finish_lat, 0, unroll=4)
    if need_ctx:
        def finish_ctx(n, carry):
            t0 = pl.multiple_of(n * L, L)
            yc_ref[pl.ds(t0, L), :] = finish(hf_c[pl.ds(t0, L), :] + hb_c[pl.ds(t0, L), :], oc_ref, t0)
            return carry

        lax.fori_loop(0, n_ctx, finish_ctx, 0, unroll=True)


def _mlstm_call(p_lat, p_ctx, gates, gate_b, conv_w, conv_b, out_norm, batch, seq, ctx_len, need_ctx):
    n_lat, n_ctx = seq // ML_CHUNK, ctx_len // ML_CHUNK
    n_pad = gates.shape[2]
    cq, ck = COL_ML_Q // HEAD_DIM, COL_ML_K // HEAD_DIM
    cv, co = COL_ML_V // HEAD_DIM, COL_ML_O // HEAD_DIM
    col = lambda rows, c0: pl.BlockSpec((rows, HEAD_DIM), lambda b, h: (b, c0 + h))
    out_specs = [pl.BlockSpec((seq, HEAD_DIM), lambda b, h: (b, h))]
    out_shape = [jax.ShapeDtypeStruct((batch * seq, ML_W), BF16)]
    if need_ctx:
        out_specs.append(pl.BlockSpec((ctx_len, HEAD_DIM), lambda b, h: (b, h)))
        out_shape.append(jax.ShapeDtypeStruct((batch * ctx_len, ML_W), BF16))

    def stream_scratch(n_tok):
        return [pltpu.VMEM((n_tok, HEAD_DIM), BF16), pltpu.VMEM((HEAD_DIM, n_tok), F32),
                pltpu.VMEM((n_tok, ML_AUG), BF16)]

    res = pl.pallas_call(
        functools.partial(_mlstm_kernel, n_lat=n_lat, n_ctx=n_ctx, need_ctx=need_ctx),
        grid=(batch, ML_HEADS),
        in_specs=[col(seq, cq), col(seq, ck), col(seq, cv), col(seq, co),
                  col(ctx_len, cq), col(ctx_len, ck), col(ctx_len, cv), col(ctx_len, co),
                  pl.BlockSpec((N_GATES, None, n_pad, ML_CHUNK), lambda b, h: (0, b, 0, 0)),
                  pl.BlockSpec((N_GATES, 1, ML_CHUNK), lambda b, h: (0, 0, 0)),
                  pl.BlockSpec((3, HEAD_DIM), lambda b, h: (0, h)),
                  pl.BlockSpec((3, HEAD_DIM), lambda b, h: (0, ML_HEADS + h)),
                  pl.BlockSpec((1, HEAD_DIM), lambda b, h: (0, h)),
                  pl.BlockSpec((1, HEAD_DIM), lambda b, h: (0, ML_HEADS + h)),
                  pl.BlockSpec((1, HEAD_DIM), lambda b, h: (0, h))],
        out_specs=out_specs,
        out_shape=out_shape,
        scratch_shapes=(stream_scratch(seq) + stream_scratch(ctx_len)
                        + [pltpu.VMEM((seq, HEAD_DIM), F32), pltpu.VMEM((seq, HEAD_DIM), F32),
                           pltpu.VMEM((ctx_len, HEAD_DIM), F32), pltpu.VMEM((ctx_len, HEAD_DIM), F32),
                           pltpu.VMEM((2 * N_STATS, n_pad, ML_CHUNK), F32),
                           pltpu.VMEM((4, n_lat + n_ctx, ML_CHUNK, ML_CHUNK), F32)]),
        compiler_params=_cparams(("parallel", "parallel")),
        name="mlstm",
    )(p_lat, p_lat, p_lat, p_lat, p_ctx, p_ctx, p_ctx, p_ctx, gates, gate_b,
      conv_w, conv_w, conv_b, conv_b, out_norm)
    return (res[0], res[1]) if need_ctx else (res[0], None)


def kernel(x, c, ctx, c_ctx, w_ada, b_ada, norm_ff1, ff1_gate, ff1_up, ff1_down, norm_mix, w_in, na_rpb,
           gqa_q_norm, gqa_k_norm, ml_conv_w, ml_conv_b, ml_gate_b, ml_out_norm, w_out, norm_ff2,
           ff2_gate, ff2_up, ff2_down, final_norm):
    batch, seq, d = x.shape
    ctx_len = ctx.shape[1]
    depth = w_ada.shape[0]
    ctx_group = batch
    n_lat_chunks, n_ctx_chunks = seq // ML_CHUNK, ctx_len // ML_CHUNK
    n_chunk_pad = -(-(n_lat_chunks + n_ctx_chunks) // 8) * 8

    xl = x.reshape(batch * seq, d)
    xc = ctx.reshape(batch * ctx_len, d)
    cond = jnp.zeros((8, d), F32).at[:batch].set(c).at[batch].set(c_ctx)
    rope_tabs = _rope_tables(seq)
    w_in_t = jnp.swapaxes(w_in, 1, 2)
    row2 = lambda v: v.reshape(1, -1)
    fin = row2(final_norm)

    for l in range(depth):
        last = l == depth - 1
        mod = _ada_call(cond, w_ada, l, row2(b_ada[l])).reshape(8 * N_MOD, 1, d)
        lat = dict(group_rows=seq, fixed_group=None)
        cx = dict(group_rows=None, fixed_group=ctx_group)

        cast = lambda w: _cast_call(w, l)
        wg, wu, wd = cast(ff1_gate), cast(ff1_up), cast(ff1_down)
        g1 = row2(norm_ff1[l])
        xl = _ffn_call(xl, mod, 0, g1, wg, wu, wd, fin, final_norm=False, **lat)
        xc = _ffn_call(xc, mod, 0, g1, wg, wu, wd, fin, final_norm=False, **cx)

        w_main_t = _cast_call(w_in_t, l, COL_GATES)
        w_gates_t = w_in_t[l, COL_GATES:].astype(BF16)
        gm = row2(norm_mix[l])
        p_lat, gt_lat = _inproj_call(xl, mod, gm, w_main_t, w_gates_t, **lat)
        p_ctx, gt_ctx = _inproj_call(xc, mod, gm, w_main_t, w_gates_t, **cx)

        ya_lat = _na_call(p_lat, p_ctx, *_na_toeplitz_slabs(na_rpb[l]), batch, seq, ctx_len)
        qg, kg = row2(gqa_q_norm[l]), row2(gqa_k_norm[l])
        qn_lat, kn_lat = _gqa_prep_call(p_lat, qg, kg, rope_tabs, seq, rope=True)
        qn_ctx, kn_ctx = _gqa_prep_call(p_ctx, qg, kg, rope_tabs, seq, rope=False, tm=ctx_len)
        yb_lat = _gqa_call(qn_lat, kn_lat, p_lat, kn_ctx, p_ctx, batch, seq, ctx_len)
        gates = jnp.concatenate(
            [gt_lat.reshape(N_GATES, batch, n_lat_chunks, ML_CHUNK),
             gt_ctx.reshape(N_GATES, batch, n_ctx_chunks, ML_CHUNK),
             jnp.zeros((N_GATES, batch, n_chunk_pad - n_lat_chunks - n_ctx_chunks, ML_CHUNK), F32)], axis=2)
        gate_b = jnp.broadcast_to(ml_gate_b[l][:, None, None], (N_GATES, 1, ML_CHUNK))
        yc_lat, yc_ctx = _mlstm_call(p_lat, p_ctx, gates, gate_b, ml_conv_w[l], row2(ml_conv_b[l]),
                                     row2(ml_out_norm[l]), batch, seq, ctx_len, not last)

        wo = cast(w_out)
        xl = _outproj_call(xl, mod, ya_lat, yb_lat, yc_lat, wo, **lat)
        wg, wu, wd = cast(ff2_gate), cast(ff2_up), cast(ff2_down)
        g2 = row2(norm_ff2[l])
        xl = _ffn_call(xl, mod, 6, g2, wg, wu, wd, fin, final_norm=last, **lat)
        if not last:
            ya_ctx = _ctx_attn_call(p_ctx, p_ctx, p_ctx, batch, ctx_len, NA_HEADS,
                                    COL_NA_Q // HEAD_DIM, COL_NA_K // HEAD_DIM, COL_NA_V // HEAD_DIM, 1,
                                    HEAD_DIM ** -0.5)
            yb_ctx = _ctx_attn_call(qn_ctx, kn_ctx, p_ctx, batch, ctx_len, GQA_Q_HEADS, 0, 0,
                                    COL_GQ_V // HEAD_DIM, GQA_Q_HEADS // GQA_KV_HEADS, None)
            xc = _outproj_call(xc, mod, ya_ctx, yb_ctx, yc_ctx, wo, **cx)
            xc = _ffn_call(xc, mod, 6, g2, wg, wu, wd, fin, final_norm=False, **cx)
    return xl.reshape(batch, seq, d)
```

```python
import functools

import numpy as np
import jax
import jax.numpy as jnp
from jax import lax
from jax.experimental import pallas as pl
from jax.experimental.pallas import tpu as pltpu

F32 = jnp.float32
BF16 = jnp.bfloat16

GRID_W = 64
HEAD_DIM = 128
NA_HEADS = 4
GQA_Q_HEADS = 4
GQA_KV_HEADS = 2
ML_HEADS = 8
NA_W = NA_HEADS * HEAD_DIM
GQA_W = GQA_Q_HEADS * HEAD_DIM
KV_W = GQA_KV_HEADS * HEAD_DIM
ML_W = ML_HEADS * HEAD_DIM
NA_WIN_R = 8
NA_WIN_C = 16
ML_CHUNK = 128
ROPE_THETA = 10000.0
N_MOD = 9
EPS = 1e-6
N_GATES = 4 * ML_HEADS
COL_NA_Q = 0
COL_NA_K = NA_W
COL_NA_V = 2 * NA_W
COL_GQ_Q = 3 * NA_W
COL_GQ_K = COL_GQ_Q + GQA_W
COL_GQ_V = COL_GQ_K + KV_W
COL_ML_Q = COL_GQ_V + KV_W
COL_ML_K = COL_ML_Q + ML_W
COL_ML_V = COL_ML_K + ML_W
COL_ML_O = COL_ML_V + ML_W
COL_GATES = COL_ML_O + ML_W

NEG_BIG = -1e30
VMEM_LIMIT_V7X = 56 * 1024 * 1024
BF16_ROWS = 16
NORM_ROWS = 32
FFN_SUBTILE = 256
FFN_OUT_SUBTILE = 1024

NA_QROWS = 8
NA_KROWS = 16
NA_KEY_TILE = 256
NA_MASKED_SLAB = 2 * NA_WIN_R - 1


def _cparams(sem):
    return pltpu.CompilerParams(dimension_semantics=sem, vmem_limit_bytes=VMEM_LIMIT_V7X)


def _rms(x, gain):
    return x * lax.rsqrt(jnp.mean(x * x, axis=-1, keepdims=True) + EPS) * gain


def _norm_modulate_store(x_ref, gain_ref, scale_ref, shift_ref, h_ref):
    gain_scale = gain_ref[...] * (1.0 + scale_ref[...])
    shift = shift_ref[...]

    def body(i, carry):
        r0 = pl.multiple_of(i * NORM_ROWS, NORM_ROWS)
        x = x_ref[pl.ds(r0, NORM_ROWS), :]
        inv = lax.rsqrt(jnp.mean(x * x, axis=-1, keepdims=True) + EPS)
        h_ref[pl.ds(r0, NORM_ROWS), :] = (x * inv * gain_scale + shift).astype(h_ref.dtype)
        return carry

    lax.fori_loop(0, x_ref.shape[0] // NORM_ROWS, body, 0, unroll=2)


def _dot(a, b):
    return jnp.dot(a, b, preferred_element_type=F32)


def _dot_nt(a, b):
    return lax.dot_general(a, b, (((1,), (1,)), ((), ())), preferred_element_type=F32)


def _dot_tn(a, b):
    return lax.dot_general(a, b, (((0,), (0,)), ((), ())), preferred_element_type=F32)


def _cast_kernel(w_ref, o_ref):
    o_ref[...] = w_ref[...].astype(o_ref.dtype)


def _cast_call(w_stack, layer, n_rows=None, row_blocks=8):
    _, rows, n_cols = w_stack.shape
    n_rows = rows if n_rows is None else n_rows
    tr = n_rows // row_blocks
    return pl.pallas_call(
        _cast_kernel,
        grid=(row_blocks,),
        in_specs=[pl.BlockSpec((None, tr, n_cols), lambda i: (layer, i, 0))],
        out_specs=pl.BlockSpec((tr, n_cols), lambda i: (i, 0)),
        out_shape=jax.ShapeDtypeStruct((n_rows, n_cols), BF16),
        compiler_params=_cparams(("parallel",)),
        name="weight_cast",
    )(w_stack)


def _ada_kernel(c_ref, w_ref, b_ref, o_ref):
    c = c_ref[...]
    a = (c * jax.nn.sigmoid(c)).astype(BF16)
    o_ref[...] = _dot(a, w_ref[...].astype(BF16)) + b_ref[...]


def _ada_call(cond, w_stack, layer, b, tn=1024):
    m, d = cond.shape
    n = w_stack.shape[2]
    return pl.pallas_call(
        _ada_kernel,
        grid=(n // tn,),
        in_specs=[pl.BlockSpec((m, d), lambda j: (0, 0)),
                  pl.BlockSpec((None, d, tn), lambda j: (layer, 0, j)),
                  pl.BlockSpec((1, tn), lambda j: (0, j))],
        out_specs=pl.BlockSpec((m, tn), lambda j: (0, j)),
        out_shape=jax.ShapeDtypeStruct((m, n), F32),
        compiler_params=_cparams(("arbitrary",)),
        name="ada_mod",
    )(cond, w_stack, b)


def _mod_spec(k, tm, group_rows, fixed_group):
    def index_map(i, *_):
        grp = fixed_group if fixed_group is not None else i // (group_rows // tm)
        return (grp * N_MOD + k, 0, 0)
    return index_map


def _ffn_kernel(x_ref, sh_ref, sc_ref, gt_ref, gain_ref, wg_ref, wu_ref, wd_ref, fin_ref, o_ref,
                h_scr, *, n_ff, final_norm):
    j = pl.program_id(1)

    @pl.when(j == 0)
    def _():
        _norm_modulate_store(x_ref, gain_ref, sc_ref, sh_ref, h_scr)
        o_ref[...] = jnp.zeros_like(o_ref)

    h = h_scr[...]
    tf = wg_ref.shape[1]
    for c0 in range(0, tf, FFN_SUBTILE):
        g = _dot(h, wg_ref[:, c0:c0 + FFN_SUBTILE])
        u = _dot(h, wu_ref[:, c0:c0 + FFN_SUBTILE])
        a = ((g * jax.nn.sigmoid(g)) * u).astype(BF16)
        for n0 in range(0, o_ref.shape[1], FFN_OUT_SUBTILE):
            o_ref[:, n0:n0 + FFN_OUT_SUBTILE] += _dot(a, wd_ref[c0:c0 + FFN_SUBTILE, n0:n0 + FFN_OUT_SUBTILE])

    @pl.when(j == n_ff - 1)
    def _():
        half_gate = 0.5 * gt_ref[...]

        def body(i, carry):
            r0 = pl.multiple_of(i * NORM_ROWS, NORM_ROWS)
            out = x_ref[pl.ds(r0, NORM_ROWS), :] + half_gate * o_ref[pl.ds(r0, NORM_ROWS), :]
            if final_norm:
                out = _rms(out, fin_ref[...])
            o_ref[pl.ds(r0, NORM_ROWS), :] = out
            return carry

        lax.fori_loop(0, o_ref.shape[0] // NORM_ROWS, body, 0, unroll=2)


def _ffn_call(x, mod, k0, gain, wg, wu, wd, fin, *, group_rows, fixed_group, final_norm,
              tm=1024, tf=512):
    t, d = x.shape
    dff = wg.shape[1]
    n_ff = dff // tf
    mspec = lambda k: pl.BlockSpec((None, 1, d), _mod_spec(k, tm, group_rows, fixed_group))
    return pl.pallas_call(
        functools.partial(_ffn_kernel, n_ff=n_ff, final_norm=final_norm),
        grid=(t // tm, n_ff),
        in_specs=[pl.BlockSpec((tm, d), lambda i, j: (i, 0)),
                  mspec(k0), mspec(k0 + 1), mspec(k0 + 2),
                  pl.BlockSpec((1, d), lambda i, j: (0, 0)),
                  pl.BlockSpec((d, tf), lambda i, j: (0, j)),
                  pl.BlockSpec((d, tf), lambda i, j: (0, j)),
                  pl.BlockSpec((tf, d), lambda i, j: (j, 0)),
                  pl.BlockSpec((1, d), lambda i, j: (0, 0))],
        out_specs=pl.BlockSpec((tm, d), lambda i, j: (i, 0)),
        out_shape=jax.ShapeDtypeStruct((t, d), F32),
        scratch_shapes=[pltpu.VMEM((tm, d), BF16)],
        compiler_params=_cparams(("parallel", "arbitrary")),
        name="macaron_ffn",
    )(x, mod, mod, mod, gain, wg, wu, wd, fin)


def _inproj_kernel(x_ref, sh_ref, sc_ref, gain_ref, wt_ref, wgt_ref, p_ref, gt_ref, h_scr):
    j = pl.program_id(1)

    @pl.when(j == 0)
    def _():
        h = _rms(x_ref[...], gain_ref[...]) * (1.0 + sc_ref[...]) + sh_ref[...]
        hb = h.astype(BF16)
        h_scr[...] = hb
        gt_ref[...] = _dot_nt(wgt_ref[...].astype(BF16), hb)

    p_ref[...] = _dot_nt(h_scr[...], wt_ref[...]).astype(p_ref.dtype)


def _inproj_call(x, mod, gain, w_main_t, w_gates_t, *, group_rows, fixed_group, tm=1024, tn=1664):
    t, d = x.shape
    n = w_main_t.shape[0]
    mspec = lambda k: pl.BlockSpec((None, 1, d), _mod_spec(k, tm, group_rows, fixed_group))
    return pl.pallas_call(
        _inproj_kernel,
        grid=(t // tm, n // tn),
        in_specs=[pl.BlockSpec((tm, d), lambda i, j: (i, 0)),
                  mspec(3), mspec(4),
                  pl.BlockSpec((1, d), lambda i, j: (0, 0)),
                  pl.BlockSpec((tn, d), lambda i, j: (j, 0)),
                  pl.BlockSpec((N_GATES, d), lambda i, j: (0, 0))],
        out_specs=[pl.BlockSpec((tm, tn), lambda i, j: (i, j)),
                   pl.BlockSpec((N_GATES, tm), lambda i, j: (0, i))],
        out_shape=[jax.ShapeDtypeStruct((t, n), BF16),
                   jax.ShapeDtypeStruct((N_GATES, t), F32)],
        scratch_shapes=[pltpu.VMEM((tm, d), BF16)],
        compiler_params=_cparams(("parallel", "arbitrary")),
        name="mixer_in_proj",
    )(x, mod, mod, gain, w_main_t, w_gates_t)


def _outproj_kernel(x_ref, gt_ref, ya_ref, yb_ref, yc_ref, w_ref, o_ref):
    acc = _dot(ya_ref[...], w_ref[0:NA_W, :])
    acc += _dot(yb_ref[...], w_ref[NA_W:NA_W + GQA_W, :])
    acc += _dot(yc_ref[...], w_ref[NA_W + GQA_W:, :])
    o_ref[...] = x_ref[...] + gt_ref[...] * acc


def _outproj_call(x, mod, ya, yb, yc, w, *, group_rows, fixed_group, tm=512):
    t, d = x.shape
    return pl.pallas_call(
        _outproj_kernel,
        grid=(t // tm,),
        in_specs=[pl.BlockSpec((tm, d), lambda i: (i, 0)),
                  pl.BlockSpec((None, 1, d), _mod_spec(5, tm, group_rows, fixed_group)),
                  pl.BlockSpec((tm, NA_W), lambda i: (i, 0)),
                  pl.BlockSpec((tm, GQA_W), lambda i: (i, 0)),
                  pl.BlockSpec((tm, ML_W), lambda i: (i, 0)),
                  pl.BlockSpec((d, d), lambda i: (0, 0))],
        out_specs=pl.BlockSpec((tm, d), lambda i: (i, 0)),
        out_shape=jax.ShapeDtypeStruct((t, d), F32),
        compiler_params=_cparams(("parallel",)),
        name="mixer_out_proj",
    )(x, mod, ya, yb, yc, w)


def _na_kernel(q_ref, k_ref, v_ref, kc_ref, vc_ref, tl_ref, tr_ref, o_ref, mb_scr, *, rows):
    rb = pl.program_id(2)
    nblk = rows // NA_QROWS
    key_row0 = jnp.clip(rb * NA_QROWS - NA_WIN_R // 2, 0, rows - NA_KROWS)

    @pl.when((rb <= 1) | (rb == nblk - 1))
    def _():
        for a in range(NA_QROWS):
            r = rb * NA_QROWS + a
            row_start = jnp.clip(r - NA_WIN_R // 2, 0, rows - NA_WIN_R)
            for j in range(NA_KROWS // 2):
                sel = []
                for kr in (key_row0 + 2 * j, key_row0 + 2 * j + 1):
                    in_win = (kr >= row_start) & (kr < row_start + NA_WIN_R)
                    sel.append(jnp.where(in_win, kr - r + (NA_WIN_R - 1), NA_MASKED_SLAB))
                mb_scr[a * GRID_W:(a + 1) * GRID_W, 2 * j * GRID_W:(2 * j + 2) * GRID_W] = (
                    tl_ref[sel[0]] + tr_ref[sel[1]])

    start = pl.multiple_of(key_row0 * GRID_W, GRID_W)
    q = (q_ref[...].astype(F32) * (HEAD_DIM ** -0.5)).astype(BF16)
    n_keys = NA_KROWS * GRID_W
    key_tile = lambda t0: k_ref[pl.ds(start + t0, NA_KEY_TILE), :]
    s = _dot_nt(q, kc_ref[...])
    s_next = _dot_nt(q, key_tile(0))
    m = jnp.max(s, axis=-1, keepdims=True)
    p = jnp.exp(s - m)
    den = jnp.sum(p, axis=-1, keepdims=True)
    acc = _dot(p.astype(BF16), vc_ref[...])
    for t0 in range(0, n_keys, NA_KEY_TILE):
        s = s_next + mb_scr[:, t0:t0 + NA_KEY_TILE]
        if t0 + NA_KEY_TILE < n_keys:
            s_next = _dot_nt(q, key_tile(t0 + NA_KEY_TILE))
        m_new = jnp.maximum(m, jnp.max(s, axis=-1, keepdims=True))
        p = jnp.exp(s - m_new)
        alpha = jnp.exp(m - m_new)
        den = alpha * den + jnp.sum(p, axis=-1, keepdims=True)
        acc = alpha * acc + _dot(p.astype(BF16), v_ref[pl.ds(start + t0, NA_KEY_TILE), :])
        m = m_new
    o_ref[...] = (acc / den).astype(o_ref.dtype)


def _na_toeplitz_slabs(rpb):
    n_h, n_dr, n_dc = rpb.shape
    edge = GRID_W - NA_WIN_C
    ext = jnp.concatenate(
        [rpb[..., NA_WIN_C - 1:], jnp.broadcast_to(rpb[..., n_dc - 1:], (n_h, n_dr, edge)),
         jnp.zeros((n_h, n_dr, 1), F32),
         jnp.broadcast_to(rpb[..., :1], (n_h, n_dr, edge)), rpb[..., :NA_WIN_C - 1]], axis=-1)
    width = 2 * GRID_W
    tiled = jnp.broadcast_to(ext[:, :, None, :], (n_h, n_dr, GRID_W, width)).reshape(n_h, n_dr, GRID_W * width)
    toep = tiled[..., :GRID_W * (width - 1)].reshape(n_h, n_dr, GRID_W, width - 1)[..., :GRID_W]
    cq = np.arange(GRID_W)
    col_start = np.clip(cq - NA_WIN_C // 2, 0, GRID_W - NA_WIN_C)
    col_ok = (cq[None, :] >= col_start[:, None]) & (cq[None, :] < col_start[:, None] + NA_WIN_C)
    slabs = jnp.where(col_ok, toep, NEG_BIG)
    slabs = jnp.concatenate([slabs, jnp.full((n_h, 1, GRID_W, GRID_W), NEG_BIG, F32)], axis=1)
    zeros = jnp.zeros_like(slabs)
    return jnp.concatenate([slabs, zeros], axis=-1), jnp.concatenate([zeros, slabs], axis=-1)


def _na_call(p_lat, p_ctx, slabs_left, slabs_right, batch, seq, ctx_len):
    rows = seq // GRID_W
    nblk = rows // NA_QROWS
    tq = NA_QROWS * GRID_W
    nk = NA_KROWS * GRID_W
    cq, ck, cv = COL_NA_Q // HEAD_DIM, COL_NA_K // HEAD_DIM, COL_NA_V // HEAD_DIM
    slab_spec = pl.BlockSpec((None, NA_MASKED_SLAB + 1, GRID_W, 2 * GRID_W), lambda b, h, r: (h, 0, 0, 0))

    return pl.pallas_call(
        functools.partial(_na_kernel, rows=rows),
        grid=(batch, NA_HEADS, nblk),
        in_specs=[pl.BlockSpec((tq, HEAD_DIM), lambda b, h, r: (b * nblk + r, cq + h)),
                  pl.BlockSpec((seq, HEAD_DIM), lambda b, h, r: (b, ck + h)),
                  pl.BlockSpec((seq, HEAD_DIM), lambda b, h, r: (b, cv + h)),
                  pl.BlockSpec((ctx_len, HEAD_DIM), lambda b, h, r: (b, ck + h)),
                  pl.BlockSpec((ctx_len, HEAD_DIM), lambda b, h, r: (b, cv + h)),
                  slab_spec, slab_spec],
        out_specs=pl.BlockSpec((tq, HEAD_DIM), lambda b, h, r: (b * nblk + r, h)),
        out_shape=jax.ShapeDtypeStruct((batch * seq, NA_W), BF16),
        scratch_shapes=[pltpu.VMEM((tq, nk), F32)],
        compiler_params=_cparams(("parallel", "parallel", "arbitrary")),
        name="neighbourhood_attn",
    )(p_lat, p_lat, p_lat, p_ctx, p_ctx, slabs_left, slabs_right)


def _ctx_attn_kernel(q_ref, k_ref, v_ref, o_ref, *, scale):
    s = _dot_nt(q_ref[...], k_ref[...])
    if scale is not None:
        s = s * scale
    m = jnp.max(s, axis=-1, keepdims=True)
    p = jnp.exp(s - m)
    den = jnp.sum(p, axis=-1, keepdims=True)
    o_ref[...] = (_dot(p.astype(BF16), v_ref[...]) / den).astype(o_ref.dtype)


def _ctx_attn_call(q_arr, k_arr, v_arr, batch, ctx_len, n_heads, q_col, k_col, v_col, kv_group, scale):
    return pl.pallas_call(
        functools.partial(_ctx_attn_kernel, scale=scale),
        grid=(batch, n_heads),
        in_specs=[pl.BlockSpec((ctx_len, HEAD_DIM), lambda b, h: (b, q_col + h)),
                  pl.BlockSpec((ctx_len, HEAD_DIM), lambda b, h: (b, k_col + h // kv_group)),
                  pl.BlockSpec((ctx_len, HEAD_DIM), lambda b, h: (b, v_col + h // kv_group))],
        out_specs=pl.BlockSpec((ctx_len, HEAD_DIM), lambda b, h: (b, h)),
        out_shape=jax.ShapeDtypeStruct((batch * ctx_len, n_heads * HEAD_DIM), BF16),
        compiler_params=_cparams(("parallel", "parallel")),
        name="ctx_attn",
    )(q_arr, k_arr, v_arr)


def _rope(x, cos, sin_lo, sin_hi):
    return (x * cos + pltpu.roll(x, HEAD_DIM - HEAD_DIM // 4, 1) * sin_lo
            + pltpu.roll(x, HEAD_DIM // 4, 1) * sin_hi)


def _gqa_prep_kernel(q_ref, k_ref, qg_ref, kg_ref, cos_ref, slo_ref, shi_ref, qo_ref, ko_ref, *, rope):
    def prep(x, gain, scale):
        y = _rms(x.astype(F32), gain)
        if rope:
            y = _rope(y, cos_ref[...], slo_ref[...], shi_ref[...])
        if scale is not None:
            y = y * scale
        return y.astype(BF16)

    for h in range(GQA_Q_HEADS):
        sl = slice(h * HEAD_DIM, (h + 1) * HEAD_DIM)
        qo_ref[:, sl] = prep(q_ref[:, sl], qg_ref[...], HEAD_DIM ** -0.5)
    for h in range(GQA_KV_HEADS):
        sl = slice(h * HEAD_DIM, (h + 1) * HEAD_DIM)
        ko_ref[:, sl] = prep(k_ref[:, sl], kg_ref[...], None)


def _rope_tables(n_tokens):
    t = jnp.arange(n_tokens)
    row = (t // GRID_W).astype(F32)
    col = (t % GRID_W).astype(F32)
    half = HEAD_DIM // 2
    inv_freq = 1.0 / (ROPE_THETA ** (jnp.arange(0, half, 2, dtype=F32) / half))
    ang_r = row[:, None] * inv_freq[None, :]
    ang_c = col[:, None] * inv_freq[None, :]
    ang = jnp.concatenate([ang_r, ang_r, ang_c, ang_c], axis=-1)
    cos, sin = jnp.cos(ang), jnp.sin(ang)
    lo = (jnp.arange(HEAD_DIM) % half) < (half // 2)
    return cos, jnp.where(lo, -sin, 0.0), jnp.where(lo, 0.0, sin)


def _gqa_prep_call(p, q_gain, k_gain, tables, seq, *, rope, tm=512):
    t = p.shape[0]
    per_seq = seq // tm if rope else 1
    tab_spec = pl.BlockSpec((tm, HEAD_DIM), lambda i: (i % per_seq, 0))
    return pl.pallas_call(
        functools.partial(_gqa_prep_kernel, rope=rope),
        grid=(t // tm,),
        in_specs=[pl.BlockSpec((tm, GQA_W), lambda i: (i, COL_GQ_Q // GQA_W)),
                  pl.BlockSpec((tm, KV_W), lambda i: (i, COL_GQ_K // KV_W)),
                  pl.BlockSpec((1, HEAD_DIM), lambda i: (0, 0)),
                  pl.BlockSpec((1, HEAD_DIM), lambda i: (0, 0)),
                  tab_spec, tab_spec, tab_spec],
        out_specs=[pl.BlockSpec((tm, GQA_W), lambda i: (i, 0)),
                   pl.BlockSpec((tm, KV_W), lambda i: (i, 0))],
        out_shape=[jax.ShapeDtypeStruct((t, GQA_W), BF16),
                   jax.ShapeDtypeStruct((t, KV_W), BF16)],
        compiler_params=_cparams(("parallel",)),
        name="gqa_prep",
    )(p, p, q_gain, k_gain, *tables)


def _gqa_kernel(q_ref, kl_ref, vl_ref, kc_ref, vc_ref, o_ref, *, tk):
    tq = q_ref.shape[0]
    grp = q_ref.shape[1] // HEAD_DIM
    q = jnp.concatenate([q_ref[:, g * HEAD_DIM:(g + 1) * HEAD_DIM] for g in range(grp)], axis=0)

    n_tiles = kl_ref.shape[0] // tk
    s = _dot_nt(q, kc_ref[...])
    s_next = _dot_nt(q, kl_ref[0:tk, :])
    m = jnp.max(s, axis=-1, keepdims=True)
    p = jnp.exp(s - m)
    den = jnp.sum(p, axis=-1, keepdims=True)
    acc = _dot(p.astype(BF16), vc_ref[...])
    for t in range(n_tiles):
        s = s_next
        if t + 1 < n_tiles:
            s_next = _dot_nt(q, kl_ref[(t + 1) * tk:(t + 2) * tk, :])
        m_new = jnp.maximum(m, jnp.max(s, axis=-1, keepdims=True))
        p = jnp.exp(s - m_new)
        alpha = jnp.exp(m - m_new)
        den = alpha * den + jnp.sum(p, axis=-1, keepdims=True)
        acc = alpha * acc + _dot(p.astype(BF16), vl_ref[t * tk:(t + 1) * tk, :])
        m = m_new
    out = (acc / den).astype(o_ref.dtype)
    for g in range(grp):
        o_ref[:, g * HEAD_DIM:(g + 1) * HEAD_DIM] = out[g * tq:(g + 1) * tq]


def _gqa_call(q_lat, k_lat, p_lat, k_ctx, p_ctx, batch, seq, ctx_len, tq=512, tk=512):
    grp = GQA_Q_HEADS // GQA_KV_HEADS
    nq = seq // tq
    cv = COL_GQ_V // HEAD_DIM
    return pl.pallas_call(
        functools.partial(_gqa_kernel, tk=tk),
        grid=(batch, GQA_KV_HEADS, nq),
        in_specs=[pl.BlockSpec((tq, grp * HEAD_DIM), lambda b, h, i: (b * nq + i, h)),
                  pl.BlockSpec((seq, HEAD_DIM), lambda b, h, i: (b, h)),
                  pl.BlockSpec((seq, HEAD_DIM), lambda b, h, i: (b, cv + h)),
                  pl.BlockSpec((ctx_len, HEAD_DIM), lambda b, h, i: (b, h)),
                  pl.BlockSpec((ctx_len, HEAD_DIM), lambda b, h, i: (b, cv + h))],
        out_specs=pl.BlockSpec((tq, grp * HEAD_DIM), lambda b, h, i: (b * nq + i, h)),
        out_shape=jax.ShapeDtypeStruct((batch * seq, GQA_W), BF16),
        compiler_params=_cparams(("parallel", "parallel", "arbitrary")),
        name="gqa_attn",
    )(q_lat, k_lat, p_lat, k_ctx, p_ctx)


ST_R, ST_W, ST_BTOT, ST_AMAX, ST_MPREV, ST_DEC, ST_INP, ST_BIGR, ST_M = range(9)
N_STATS = 9
REP_R, REP_M = range(2)
ML_AUG = 2 * HEAD_DIM
ML_GROUP = 4


def _mlstm_kernel(ql_ref, kl_ref, vl_ref, ol_ref, qc_ref, kc_ref, vc_ref, oc_ref, g_ref, gb_ref,
                  cwq_ref, cwk_ref, cbq_ref, cbk_ref, on_ref, yl_ref, *rest,
                  n_lat, n_ctx, need_ctx):
    if need_ctx:
        yc_ref = rest[0]
        rest = rest[1:]
    qa_l, kt_l, va_l, qa_c, kt_c, va_c, hf_l, hb_l, hf_c, hb_c, st, rep = rest
    L = ML_CHUNK
    n_all = n_lat + n_ctx
    h = pl.program_id(1)
    row = lax.broadcasted_iota(jnp.int32, (L, L), 0)
    col = lax.broadcasted_iota(jnp.int32, (L, L), 1)

    tri_fw = (row <= col).astype(F32)
    tri_bw = (row >= col).astype(F32)
    b_cum = []
    for d, tri in enumerate((tri_fw, tri_bw)):
        log_i = g_ref[2 * d * ML_HEADS + h] + gb_ref[2 * d * ML_HEADS + h]
        log_f = jax.nn.log_sigmoid(g_ref[(2 * d + 1) * ML_HEADS + h] + gb_ref[(2 * d + 1) * ML_HEADS + h])
        b = jnp.dot(log_f, tri, preferred_element_type=F32, precision=lax.Precision.HIGHEST)
        b_tot = b[:, L - 1:L] if d == 0 else b[:, 0:1]
        a = b_tot - b + log_i
        a_max = jnp.max(a, axis=-1, keepdims=True)
        r = log_i - b
        lane = lax.broadcasted_iota(jnp.int32, r.shape, 1)
        run = r
        shift = 1
        while shift < L:
            if d == 0:
                run = jnp.maximum(run, jnp.where(lane >= shift, pltpu.roll(run, shift, 1), NEG_BIG))
            else:
                run = jnp.maximum(run, jnp.where(lane < L - shift, pltpu.roll(run, L - shift, 1), NEG_BIG))
            shift *= 2
        base = d * N_STATS
        st[base + ST_R] = r
        st[base + ST_W] = jnp.exp(a - a_max)
        st[base + ST_BTOT] = jnp.broadcast_to(b_tot, b.shape)
        st[base + ST_AMAX] = jnp.broadcast_to(a_max, b.shape)
        st[base + ST_BIGR] = run
        st[base + ST_MPREV] = jnp.zeros_like(b)
        b_cum.append(b)

    def stabiliser_step(i, carry):
        new = []
        for d, m_prev in enumerate(carry):
            in_ctx = i < n_ctx
            j = jnp.where(in_ctx, i, i - n_ctx)
            if d == 0:
                n = jnp.where(in_ctx, n_lat + j, j)
            else:
                n = jnp.where(in_ctx, n_lat + n_ctx - 1 - j, n_lat - 1 - j)
            base = d * N_STATS
            b_tot = st[base + ST_BTOT, pl.ds(n, 1), :]
            a_max = st[base + ST_AMAX, pl.ds(n, 1), :]
            m_new = jnp.maximum(b_tot + m_prev, a_max)
            st[base + ST_MPREV, pl.ds(n, 1), :] = m_prev
            st[base + ST_DEC, pl.ds(n, 1), :] = jnp.exp(b_tot + m_prev - m_new)
            st[base + ST_INP, pl.ds(n, 1), :] = jnp.exp(a_max - m_new)
            new.append(m_new)
        return tuple(new)

    lax.fori_loop(0, n_all, stabiliser_step, (jnp.zeros((1, L), F32), jnp.zeros((1, L), F32)))

    for d in range(2):
        base = d * N_STATS
        big_r = jnp.maximum(st[base + ST_BIGR], st[base + ST_MPREV])
        st[base + ST_BIGR] = big_r
        st[base + ST_M] = b_cum[d] + big_r

    def column_tables(n, carry):
        for d in range(2):
            for k_rep, k_st in ((REP_R, ST_BIGR), (REP_M, ST_M)):
                rows = jnp.broadcast_to(st[d * N_STATS + k_st, pl.ds(n, 1), :], (L, L))
                rep[2 * d + k_rep, n] = rows.T
        return carry

    lax.fori_loop(0, n_all, column_tables, 0, unroll=2)

    def prepare_stream(q_src, k_src, v_src, q_dst, kt_dst, va_dst, n_chunks):
        n_tok = n_chunks * L
        sub = lax.broadcasted_iota(jnp.int32, (L, HEAD_DIM), 0)

        def conv_silu(src_ref, w_ref, b_ref, n, t0):
            x = src_ref[pl.ds(t0, L), :].astype(F32)
            prev_t0 = pl.multiple_of(jnp.maximum(t0 - BF16_ROWS, 0), BF16_ROWS)
            next_t0 = pl.multiple_of(jnp.minimum(t0 + L, n_tok - BF16_ROWS), BF16_ROWS)
            prev_row = (src_ref[pl.ds(prev_t0, BF16_ROWS), :].astype(F32)[BF16_ROWS - 1:, :]
                        * jnp.where(n > 0, 1.0, 0.0))
            next_row = (src_ref[pl.ds(next_t0, BF16_ROWS), :].astype(F32)[:1, :]
                        * jnp.where(n < n_chunks - 1, 1.0, 0.0))
            x_prev = jnp.where(sub == 0, prev_row, pltpu.roll(x, 1, 0))
            x_next = jnp.where(sub == L - 1, next_row, pltpu.roll(x, L - 1, 0))
            y = w_ref[0:1, :] * x_prev + w_ref[1:2, :] * x + w_ref[2:3, :] * x_next + b_ref[...]
            return y * jax.nn.sigmoid(y)

        def body(n, carry):
            t0 = pl.multiple_of(n * L, L)
            q_dst[pl.ds(t0, L), :] = conv_silu(q_src, cwq_ref, cbq_ref, n, t0).astype(BF16)
            k_act = conv_silu(k_src, cwk_ref, cbk_ref, n, t0) * (HEAD_DIM ** -0.5)
            kt_dst[:, pl.ds(t0, L)] = k_act.T
            va_dst[pl.ds(t0, L), 0:HEAD_DIM] = v_src[pl.ds(t0, L), :]
            va_dst[pl.ds(t0, L), HEAD_DIM:ML_AUG] = jnp.ones((L, HEAD_DIM), BF16)
            return carry

        lax.fori_loop(0, n_chunks, body, 0, unroll=2)

    prepare_stream(ql_ref, kl_ref, vl_ref, qa_l, kt_l, va_l, n_lat)
    prepare_stream(qc_ref, kc_ref, vc_ref, qa_c, kt_c, va_c, n_ctx)

    def stat(d, n, k):
        return st[d * N_STATS + k, pl.ds(n, 1), :]

    both = lambda x: jnp.concatenate([x, x], axis=-1)

    def make_body(group, n_chunks, stat0, q_src, kt_src, va_src, h_dst, want_out):
        def body(i, carry):
            items = []
            for u in range(group):
                nf = i * group + u
                for d, n in ((0, nf), (1, n_chunks - 1 - nf)):
                    items.append((d, stat0 + n, pl.multiple_of(n * L, L)))
            va = [va_src[pl.ds(t0, L), :] for _, _, t0 in items]
            c_loc, qk, q = [], [], []
            for (d, ns, t0), v in zip(items, va):
                kt = kt_src[:, pl.ds(t0, L)]
                c_loc.append(_dot((kt * stat(d, ns, ST_W)).astype(BF16), v))
                if want_out:
                    q.append(q_src[pl.ds(t0, L), :])
                    qk.append(_dot(q[-1], kt.astype(BF16)))
            lhs = []
            if want_out:
                for (d, ns, t0), qi, qki in zip(items, q, qk):
                    seen = (col <= row) if d == 0 else (col >= row)
                    big_r = rep[2 * d + REP_R, ns]
                    s = qki * jnp.exp(jnp.where(seen, stat(d, ns, ST_R) - big_r, NEG_BIG))
                    g = jnp.exp(stat(d, ns, ST_MPREV) - big_r)
                    lhs.append(jnp.concatenate([(g * qi.astype(F32)).astype(BF16), s.astype(BF16)], axis=1))
            state = list(carry)
            for idx, (d, ns, t0) in enumerate(items):
                if want_out:
                    acc = _dot(lhs[idx], jnp.concatenate([state[d].astype(BF16), va[idx]], axis=0))
                    floor = jnp.exp(-rep[2 * d + REP_M, ns])
                    h_dst[d][pl.ds(t0, L), :] = (acc[:, :HEAD_DIM]
                                                 / jnp.maximum(jnp.abs(acc[:, HEAD_DIM:]), floor))
                state[d] = both(stat(d, ns, ST_DEC)) * state[d] + both(stat(d, ns, ST_INP)) * c_loc[idx]
            return tuple(state)
        return body

    def finish(hsum, o_src, t0):
        y = _rms(hsum, on_ref[...]) * jax.nn.sigmoid(o_src[pl.ds(t0, L), :].astype(F32))
        return y.astype(BF16)

    zero_state = jnp.zeros((HEAD_DIM, ML_AUG), F32)
    carry = make_body(n_ctx, n_ctx, n_lat, qa_c, kt_c, va_c, (hf_c, hb_c), need_ctx)(0, (zero_state, zero_state))
    lax.fori_loop(0, n_lat // ML_GROUP, make_body(ML_GROUP, n_lat, 0, qa_l, kt_l, va_l, (hf_l, hb_l), True),
                  carry)

    def finish_lat(n, carry):
        t0 = pl.multiple_of(n * L, L)
        yl_ref[pl.ds(t0, L), :] = finish(hf_l[pl.ds(t0, L), :] + hb_l[pl.ds(t0, L), :], ol_ref, t0)
        return carry

    lax.fori_loop(0, n_lat, finish_lat, 0, unroll=4)
    if need_ctx:
        def finish_ctx(n, carry):
            t0 = pl.multiple_of(n * L, L)
            yc_ref[pl.ds(t0, L), :] = finish(hf_c[pl.ds(t0, L), :] + hb_c[pl.ds(t0, L), :], oc_ref, t0)
            return carry

        lax.fori_loop(0, n_ctx, finish_ctx, 0, unroll=True)


def _mlstm_call(p_lat, p_ctx, gates, gate_b, conv_w, conv_b, out_norm, batch, seq, ctx_len, need_ctx):
    n_lat, n_ctx = seq // ML_CHUNK, ctx_len // ML_CHUNK
    n_pad = gates.shape[2]
    cq, ck = COL_ML_Q // HEAD_DIM, COL_ML_K // HEAD_DIM
    cv, co = COL_ML_V // HEAD_DIM, COL_ML_O // HEAD_DIM
    col = lambda rows, c0: pl.BlockSpec((rows, HEAD_DIM), lambda b, h: (b, c0 + h))
    out_specs = [pl.BlockSpec((seq, HEAD_DIM), lambda b, h: (b, h))]
    out_shape = [jax.ShapeDtypeStruct((batch * seq, ML_W), BF16)]
    if need_ctx:
        out_specs.append(pl.BlockSpec((ctx_len, HEAD_DIM), lambda b, h: (b, h)))
        out_shape.append(jax.ShapeDtypeStruct((batch * ctx_len, ML_W), BF16))

    def stream_scratch(n_tok):
        return [pltpu.VMEM((n_tok, HEAD_DIM), BF16), pltpu.VMEM((HEAD_DIM, n_tok), F32),
                pltpu.VMEM((n_tok, ML_AUG), BF16)]

    res = pl.pallas_call(
        functools.partial(_mlstm_kernel, n_lat=n_lat, n_ctx=n_ctx, need_ctx=need_ctx),
        grid=(batch, ML_HEADS),
        in_specs=[col(seq, cq), col(seq, ck), col(seq, cv), col(seq, co),
                  col(ctx_len, cq), col(ctx_len, ck), col(ctx_len, cv), col(ctx_len, co),
                  pl.BlockSpec((N_GATES, None, n_pad, ML_CHUNK), lambda b, h: (0, b, 0, 0)),
                  pl.BlockSpec((N_GATES, 1, ML_CHUNK), lambda b, h: (0, 0, 0)),
                  pl.BlockSpec((3, HEAD_DIM), lambda b, h: (0, h)),
                  pl.BlockSpec((3, HEAD_DIM), lambda b, h: (0, ML_HEADS + h)),
                  pl.BlockSpec((1, HEAD_DIM), lambda b, h: (0, h)),
                  pl.BlockSpec((1, HEAD_DIM), lambda b, h: (0, ML_HEADS + h)),
                  pl.BlockSpec((1, HEAD_DIM), lambda b, h: (0, h))],
        out_specs=out_specs,
        out_shape=out_shape,
        scratch_shapes=(stream_scratch(seq) + stream_scratch(ctx_len)
                        + [pltpu.VMEM((seq, HEAD_DIM), F32), pltpu.VMEM((seq, HEAD_DIM), F32),
                           pltpu.VMEM((ctx_len, HEAD_DIM), F32), pltpu.VMEM((ctx_len, HEAD_DIM), F32),
                           pltpu.VMEM((2 * N_STATS, n_pad, ML_CHUNK), F32),
                           pltpu.VMEM((4, n_lat + n_ctx, ML_CHUNK, ML_CHUNK), F32)]),
        compiler_params=_cparams(("parallel", "parallel")),
        name="mlstm",
    )(p_lat, p_lat, p_lat, p_lat, p_ctx, p_ctx, p_ctx, p_ctx, gates, gate_b,
      conv_w, conv_w, conv_b, conv_b, out_norm)
    return (res[0], res[1]) if need_ctx else (res[0], None)


def kernel(x, c, ctx, c_ctx, w_ada, b_ada, norm_ff1, ff1_gate, ff1_up, ff1_down, norm_mix, w_in, na_rpb,
           gqa_q_norm, gqa_k_norm, ml_conv_w, ml_conv_b, ml_gate_b, ml_out_norm, w_out, norm_ff2,
           ff2_gate, ff2_up, ff2_down, final_norm):
    batch, seq, d = x.shape
    ctx_len = ctx.shape[1]
    depth = w_ada.shape[0]
    ctx_group = batch
    n_lat_chunks, n_ctx_chunks = seq // ML_CHUNK, ctx_len // ML_CHUNK
    n_chunk_pad = -(-(n_lat_chunks + n_ctx_chunks) // 8) * 8

    xl = x.reshape(batch * seq, d)
    xc = ctx.reshape(batch * ctx_len, d)
    cond = jnp.zeros((8, d), F32).at[:batch].set(c).at[batch].set(c_ctx)
    rope_tabs = _rope_tables(seq)
    w_in_t = jnp.swapaxes(w_in, 1, 2)
    row2 = lambda v: v.reshape(1, -1)
    fin = row2(final_norm)

    for l in range(depth):
        last = l == depth - 1
        mod = _ada_call(cond, w_ada, l, row2(b_ada[l])).reshape(8 * N_MOD, 1, d)
        lat = dict(group_rows=seq, fixed_group=None)
        cx = dict(group_rows=None, fixed_group=ctx_group)

        cast = lambda w: _cast_call(w, l)
        wg, wu, wd = cast(ff1_gate), cast(ff1_up), cast(ff1_down)
        g1 = row2(norm_ff1[l])
        xl = _ffn_call(xl, mod, 0, g1, wg, wu, wd, fin, final_norm=False, **lat)
        xc = _ffn_call(xc, mod, 0, g1, wg, wu, wd, fin, final_norm=False, **cx)

        w_main_t = _cast_call(w_in_t, l, COL_GATES)
        w_gates_t = w_in_t[l, COL_GATES:]
        gm = row2(norm_mix[l])
        p_lat, gt_lat = _inproj_call(xl, mod, gm, w_main_t, w_gates_t, **lat)
        p_ctx, gt_ctx = _inproj_call(xc, mod, gm, w_main_t, w_gates_t, **cx)

        ya_lat = _na_call(p_lat, p_ctx, *_na_toeplitz_slabs(na_rpb[l]), batch, seq, ctx_len)
        qg, kg = row2(gqa_q_norm[l]), row2(gqa_k_norm[l])
        qn_lat, kn_lat = _gqa_prep_call(p_lat, qg, kg, rope_tabs, seq, rope=True)
        qn_ctx, kn_ctx = _gqa_prep_call(p_ctx, qg, kg, rope_tabs, seq, rope=False, tm=ctx_len)
        yb_lat = _gqa_call(qn_lat, kn_lat, p_lat, kn_ctx, p_ctx, batch, seq, ctx_len)
        gates = jnp.concatenate(
            [gt_lat.reshape(N_GATES, batch, n_lat_chunks, ML_CHUNK),
             gt_ctx.reshape(N_GATES, batch, n_ctx_chunks, ML_CHUNK),
             jnp.zeros((N_GATES, batch, n_chunk_pad - n_lat_chunks - n_ctx_chunks, ML_CHUNK), F32)], axis=2)
        gate_b = jnp.broadcast_to(ml_gate_b[l][:, None, None], (N_GATES, 1, ML_CHUNK))
        yc_lat, yc_ctx = _mlstm_call(p_lat, p_ctx, gates, gate_b, ml_conv_w[l], row2(ml_conv_b[l]),
                                     row2(ml_out_norm[l]), batch, seq, ctx_len, not last)

        wo = cast(w_out)
        xl = _outproj_call(xl, mod, ya_lat, yb_lat, yc_lat, wo, **lat)
        wg, wu, wd = cast(ff2_gate), cast(ff2_up), cast(ff2_down)
        g2 = row2(norm_ff2[l])
        xl = _ffn_call(xl, mod, 6, g2, wg, wu, wd, fin, final_norm=last, **lat)
        if not last:
            ya_ctx = _ctx_attn_call(p_ctx, p_ctx, p_ctx, batch, ctx_len, NA_HEADS,
                                    COL_NA_Q // HEAD_DIM, COL_NA_K // HEAD_DIM, COL_NA_V // HEAD_DIM, 1,
                                    HEAD_DIM ** -0.5)
            yb_ctx = _ctx_attn_call(qn_ctx, kn_ctx, p_ctx, batch, ctx_len, GQA_Q_HEADS, 0, 0,
                                    COL_GQ_V // HEAD_DIM, GQA_Q_HEADS // GQA_KV_HEADS, None)
            xc = _outproj_call(xc, mod, ya_ctx, yb_ctx, yc_ctx, wo, **cx)
            xc = _ffn_call(xc, mod, 6, g2, wg, wu, wd, fin, final_norm=False, **cx)
    return xl.reshape(batch, seq, d)
```

```python
import functools

import numpy as np
import jax
import jax.numpy as jnp
from jax import lax
from jax.experimental import pallas as pl
from jax.experimental.pallas import tpu as pltpu

F32 = jnp.float32
BF16 = jnp.bfloat16

GRID_W = 64
HEAD_DIM = 128
NA_HEADS = 4
GQA_Q_HEADS = 4
GQA_KV_HEADS = 2
ML_HEADS = 8
NA_W = NA_HEADS * HEAD_DIM
GQA_W = GQA_Q_HEADS * HEAD_DIM
KV_W = GQA_KV_HEADS * HEAD_DIM
ML_W = ML_HEADS * HEAD_DIM
NA_WIN_R = 8
NA_WIN_C = 16
ML_CHUNK = 128
ROPE_THETA = 10000.0
N_MOD = 9
EPS = 1e-6
N_GATES = 4 * ML_HEADS
COL_NA_Q = 0
COL_NA_K = NA_W
COL_NA_V = 2 * NA_W
COL_GQ_Q = 3 * NA_W
COL_GQ_K = COL_GQ_Q + GQA_W
COL_GQ_V = COL_GQ_K + KV_W
COL_ML_Q = COL_GQ_V + KV_W
COL_ML_K = COL_ML_Q + ML_W
COL_ML_V = COL_ML_K + ML_W
COL_ML_O = COL_ML_V + ML_W
COL_GATES = COL_ML_O + ML_W

NEG_BIG = -1e30
VMEM_LIMIT_V7X = 56 * 1024 * 1024
BF16_ROWS = 16
NORM_ROWS = 32
FFN_SUBTILE = 256
FFN_OUT_SUBTILE = 1024

NA_QROWS = 8
NA_KROWS = 16
NA_KEY_TILE = 256
NA_MASKED_SLAB = 2 * NA_WIN_R - 1


def _cparams(sem):
    return pltpu.CompilerParams(dimension_semantics=sem, vmem_limit_bytes=VMEM_LIMIT_V7X)


def _rms(x, gain):
    return x * lax.rsqrt(jnp.mean(x * x, axis=-1, keepdims=True) + EPS) * gain


def _norm_modulate_store(x_ref, gain_ref, scale_ref, shift_ref, h_ref, zero_ref):
    gain_scale = gain_ref[...] * (1.0 + scale_ref[...])
    shift = shift_ref[...]

    def body(i, carry):
        r0 = pl.multiple_of(i * NORM_ROWS, NORM_ROWS)
        x = x_ref[pl.ds(r0, NORM_ROWS), :]
        inv = lax.rsqrt(jnp.mean(x * x, axis=-1, keepdims=True) + EPS)
        h_ref[pl.ds(r0, NORM_ROWS), :] = (x * inv * gain_scale + shift).astype(h_ref.dtype)
        zero_ref[pl.ds(r0, NORM_ROWS), :] = jnp.zeros((NORM_ROWS, zero_ref.shape[1]), zero_ref.dtype)
        return carry

    lax.fori_loop(0, x_ref.shape[0] // NORM_ROWS, body, 0, unroll=4)


def _dot(a, b):
    return jnp.dot(a, b, preferred_element_type=F32)


def _dot_nt(a, b):
    return lax.dot_general(a, b, (((1,), (1,)), ((), ())), preferred_element_type=F32)


def _dot_tn(a, b):
    return lax.dot_general(a, b, (((0,), (0,)), ((), ())), preferred_element_type=F32)


def _cast_kernel(w_ref, o_ref):
    o_ref[...] = w_ref[...].astype(o_ref.dtype)


def _cast_call(w_stack, layer, n_rows=None, row_blocks=8):
    _, rows, n_cols = w_stack.shape
    n_rows = rows if n_rows is None else n_rows
    tr = n_rows // row_blocks
    return pl.pallas_call(
        _cast_kernel,
        grid=(row_blocks,),
        in_specs=[pl.BlockSpec((None, tr, n_cols), lambda i: (layer, i, 0))],
        out_specs=pl.BlockSpec((tr, n_cols), lambda i: (i, 0)),
        out_shape=jax.ShapeDtypeStruct((n_rows, n_cols), BF16),
        compiler_params=_cparams(("parallel",)),
        name="weight_cast",
    )(w_stack)


def _ada_kernel(c_ref, w_ref, b_ref, o_ref):
    c = c_ref[...]
    a = (c * jax.nn.sigmoid(c)).astype(BF16)
    o_ref[...] = _dot(a, w_ref[...].astype(BF16)) + b_ref[...]


def _ada_call(cond, w_stack, layer, b, tn=1024):
    m, d = cond.shape
    n = w_stack.shape[2]
    return pl.pallas_call(
        _ada_kernel,
        grid=(n // tn,),
        in_specs=[pl.BlockSpec((m, d), lambda j: (0, 0)),
                  pl.BlockSpec((None, d, tn), lambda j: (layer, 0, j)),
                  pl.BlockSpec((1, tn), lambda j: (0, j))],
        out_specs=pl.BlockSpec((m, tn), lambda j: (0, j)),
        out_shape=jax.ShapeDtypeStruct((m, n), F32),
        compiler_params=_cparams(("arbitrary",)),
        name="ada_mod",
    )(cond, w_stack, b)


def _mod_spec(k, tm, group_rows, fixed_group):
    def index_map(i, *_):
        grp = fixed_group if fixed_group is not None else i // (group_rows // tm)
        return (grp * N_MOD + k, 0, 0)
    return index_map


def _ffn_kernel(x_ref, sh_ref, sc_ref, gt_ref, gain_ref, wg_ref, wu_ref, wd_ref, fin_ref, o_ref,
                h_scr, *, n_ff, final_norm):
    j = pl.program_id(1)

    @pl.when(j == 0)
    def _():
        _norm_modulate_store(x_ref, gain_ref, sc_ref, sh_ref, h_scr, o_ref)

    h = h_scr[...]
    tf = wg_ref.shape[1]
    for c0 in range(0, tf, FFN_SUBTILE):
        g = _dot(h, wg_ref[:, c0:c0 + FFN_SUBTILE])
        u = _dot(h, wu_ref[:, c0:c0 + FFN_SUBTILE])
        a = ((g * jax.nn.sigmoid(g)) * u).astype(BF16)
        for n0 in range(0, o_ref.shape[1], FFN_OUT_SUBTILE):
            o_ref[:, n0:n0 + FFN_OUT_SUBTILE] += _dot(a, wd_ref[c0:c0 + FFN_SUBTILE, n0:n0 + FFN_OUT_SUBTILE])

    @pl.when(j == n_ff - 1)
    def _():
        half_gate = 0.5 * gt_ref[...]

        def body(i, carry):
            r0 = pl.multiple_of(i * NORM_ROWS, NORM_ROWS)
            out = x_ref[pl.ds(r0, NORM_ROWS), :] + half_gate * o_ref[pl.ds(r0, NORM_ROWS), :]
            if final_norm:
                out = _rms(out, fin_ref[...])
            o_ref[pl.ds(r0, NORM_ROWS), :] = out
            return carry

        lax.fori_loop(0, o_ref.shape[0] // NORM_ROWS, body, 0, unroll=4)


def _ffn_call(x, mod, k0, gain, wg, wu, wd, fin, *, group_rows, fixed_group, final_norm,
              tm=1024, tf=512):
    t, d = x.shape
    dff = wg.shape[1]
    n_ff = dff // tf
    mspec = lambda k: pl.BlockSpec((None, 1, d), _mod_spec(k, tm, group_rows, fixed_group))
    return pl.pallas_call(
        functools.partial(_ffn_kernel, n_ff=n_ff, final_norm=final_norm),
        grid=(t // tm, n_ff),
        in_specs=[pl.BlockSpec((tm, d), lambda i, j: (i, 0)),
                  mspec(k0), mspec(k0 + 1), mspec(k0 + 2),
                  pl.BlockSpec((1, d), lambda i, j: (0, 0)),
                  pl.BlockSpec((d, tf), lambda i, j: (0, j)),
                  pl.BlockSpec((d, tf), lambda i, j: (0, j)),
                  pl.BlockSpec((tf, d), lambda i, j: (j, 0)),
                  pl.BlockSpec((1, d), lambda i, j: (0, 0))],
        out_specs=pl.BlockSpec((tm, d), lambda i, j: (i, 0)),
        out_shape=jax.ShapeDtypeStruct((t, d), F32),
        scratch_shapes=[pltpu.VMEM((tm, d), BF16)],
        compiler_params=_cparams(("parallel", "arbitrary")),
        name="macaron_ffn",
    )(x, mod, mod, mod, gain, wg, wu, wd, fin)


def _inproj_kernel(x_ref, sh_ref, sc_ref, gain_ref, wt_ref, wgt_ref, p_ref, gt_ref, h_scr):
    j = pl.program_id(1)

    @pl.when(j == 0)
    def _():
        h = _rms(x_ref[...], gain_ref[...]) * (1.0 + sc_ref[...]) + sh_ref[...]
        hb = h.astype(BF16)
        h_scr[...] = hb
        gt_ref[...] = _dot_nt(wgt_ref[...].astype(BF16), hb)

    p_ref[...] = _dot_nt(h_scr[...], wt_ref[...]).astype(p_ref.dtype)


def _inproj_call(x, mod, gain, w_main_t, w_gates_t, *, group_rows, fixed_group, tm=512, tn=3328):
    t, d = x.shape
    n = w_main_t.shape[0]
    mspec = lambda k: pl.BlockSpec((None, 1, d), _mod_spec(k, tm, group_rows, fixed_group))
    return pl.pallas_call(
        _inproj_kernel,
        grid=(t // tm, n // tn),
        in_specs=[pl.BlockSpec((tm, d), lambda i, j: (i, 0)),
                  mspec(3), mspec(4),
                  pl.BlockSpec((1, d), lambda i, j: (0, 0)),
                  pl.BlockSpec((tn, d), lambda i, j: (j, 0)),
                  pl.BlockSpec((N_GATES, d), lambda i, j: (0, 0))],
        out_specs=[pl.BlockSpec((tm, tn), lambda i, j: (i, j)),
                   pl.BlockSpec((N_GATES, tm), lambda i, j: (0, i))],
        out_shape=[jax.ShapeDtypeStruct((t, n), BF16),
                   jax.ShapeDtypeStruct((N_GATES, t), F32)],
        scratch_shapes=[pltpu.VMEM((tm, d), BF16)],
        compiler_params=_cparams(("parallel", "arbitrary")),
        name="mixer_in_proj",
    )(x, mod, mod, gain, w_main_t, w_gates_t)


def _outproj_kernel(x_ref, gt_ref, ya_ref, yb_ref, yc_ref, w_ref, o_ref):
    acc = _dot(ya_ref[...], w_ref[0:NA_W, :])
    acc += _dot(yb_ref[...], w_ref[NA_W:NA_W + GQA_W, :])
    acc += _dot(yc_ref[...], w_ref[NA_W + GQA_W:, :])
    o_ref[...] = x_ref[...] + gt_ref[...] * acc


def _outproj_call(x, mod, ya, yb, yc, w, *, group_rows, fixed_group, tm=512):
    t, d = x.shape
    return pl.pallas_call(
        _outproj_kernel,
        grid=(t // tm,),
        in_specs=[pl.BlockSpec((tm, d), lambda i: (i, 0)),
                  pl.BlockSpec((None, 1, d), _mod_spec(5, tm, group_rows, fixed_group)),
                  pl.BlockSpec((tm, NA_W), lambda i: (i, 0)),
                  pl.BlockSpec((tm, GQA_W), lambda i: (i, 0)),
                  pl.BlockSpec((tm, ML_W), lambda i: (i, 0)),
                  pl.BlockSpec((d, d), lambda i: (0, 0))],
        out_specs=pl.BlockSpec((tm, d), lambda i: (i, 0)),
        out_shape=jax.ShapeDtypeStruct((t, d), F32),
        compiler_params=_cparams(("parallel",)),
        name="mixer_out_proj",
    )(x, mod, ya, yb, yc, w)


def _na_kernel(q_ref, k_ref, v_ref, kc_ref, vc_ref, tl_ref, tr_ref, o_ref, mb_scr, *, rows):
    rb = pl.program_id(2)
    nblk = rows // NA_QROWS
    key_row0 = jnp.clip(rb * NA_QROWS - NA_WIN_R // 2, 0, rows - NA_KROWS)

    @pl.when((rb <= 1) | (rb == nblk - 1))
    def _():
        for a in range(NA_QROWS):
            r = rb * NA_QROWS + a
            row_start = jnp.clip(r - NA_WIN_R // 2, 0, rows - NA_WIN_R)
            for j in range(NA_KROWS // 2):
                sel = []
                for kr in (key_row0 + 2 * j, key_row0 + 2 * j + 1):
                    in_win = (kr >= row_start) & (kr < row_start + NA_WIN_R)
                    sel.append(jnp.where(in_win, kr - r + (NA_WIN_R - 1), NA_MASKED_SLAB))
                mb_scr[a * GRID_W:(a + 1) * GRID_W, 2 * j * GRID_W:(2 * j + 2) * GRID_W] = (
                    tl_ref[sel[0]] + tr_ref[sel[1]])

    start = pl.multiple_of(key_row0 * GRID_W, GRID_W)
    q = (q_ref[...].astype(F32) * (HEAD_DIM ** -0.5)).astype(BF16)
    n_keys = NA_KROWS * GRID_W
    key_tile = lambda t0: k_ref[pl.ds(start + t0, NA_KEY_TILE), :]
    s = _dot_nt(q, kc_ref[...])
    s_next = _dot_nt(q, key_tile(0))
    m = jnp.max(s, axis=-1, keepdims=True)
    p = jnp.exp(s - m)
    den = jnp.sum(p, axis=-1, keepdims=True)
    acc = _dot(p.astype(BF16), vc_ref[...])
    for t0 in range(0, n_keys, NA_KEY_TILE):
        s = s_next + mb_scr[:, t0:t0 + NA_KEY_TILE]
        if t0 + NA_KEY_TILE < n_keys:
            s_next = _dot_nt(q, key_tile(t0 + NA_KEY_TILE))
        m_new = jnp.maximum(m, jnp.max(s, axis=-1, keepdims=True))
        p = jnp.exp(s - m_new)
        alpha = jnp.exp(m - m_new)
        den = alpha * den + jnp.sum(p, axis=-1, keepdims=True)
        acc = alpha * acc + _dot(p.astype(BF16), v_ref[pl.ds(start + t0, NA_KEY_TILE), :])
        m = m_new
    o_ref[...] = (acc / den).astype(o_ref.dtype)


def _na_toeplitz_slabs(rpb):
    n_h, n_dr, n_dc = rpb.shape
    edge = GRID_W - NA_WIN_C
    ext = jnp.concatenate(
        [rpb[..., NA_WIN_C - 1:], jnp.broadcast_to(rpb[..., n_dc - 1:], (n_h, n_dr, edge)),
         jnp.zeros((n_h, n_dr, 1), F32),
         jnp.broadcast_to(rpb[..., :1], (n_h, n_dr, edge)), rpb[..., :NA_WIN_C - 1]], axis=-1)
    width = 2 * GRID_W
    tiled = jnp.broadcast_to(ext[:, :, None, :], (n_h, n_dr, GRID_W, width)).reshape(n_h, n_dr, GRID_W * width)
    toep = tiled[..., :GRID_W * (width - 1)].reshape(n_h, n_dr, GRID_W, width - 1)[..., :GRID_W]
    cq = np.arange(GRID_W)
    col_start = np.clip(cq - NA_WIN_C // 2, 0, GRID_W - NA_WIN_C)
    col_ok = (cq[None, :] >= col_start[:, None]) & (cq[None, :] < col_start[:, None] + NA_WIN_C)
    slabs = jnp.where(col_ok, toep, NEG_BIG)
    slabs = jnp.concatenate([slabs, jnp.full((n_h, 1, GRID_W, GRID_W), NEG_BIG, F32)], axis=1)
    zeros = jnp.zeros_like(slabs)
    return jnp.concatenate([slabs, zeros], axis=-1), jnp.concatenate([zeros, slabs], axis=-1)


def _na_call(p_lat, p_ctx, slabs_left, slabs_right, batch, seq, ctx_len):
    rows = seq // GRID_W
    nblk = rows // NA_QROWS
    tq = NA_QROWS * GRID_W
    nk = NA_KROWS * GRID_W
    cq, ck, cv = COL_NA_Q // HEAD_DIM, COL_NA_K // HEAD_DIM, COL_NA_V // HEAD_DIM
    slab_spec = pl.BlockSpec((None, NA_MASKED_SLAB + 1, GRID_W, 2 * GRID_W), lambda b, h, r: (h, 0, 0, 0))

    return pl.pallas_call(
        functools.partial(_na_kernel, rows=rows),
        grid=(batch, NA_HEADS, nblk),
        in_specs=[pl.BlockSpec((tq, HEAD_DIM), lambda b, h, r: (b * nblk + r, cq + h)),
                  pl.BlockSpec((seq, HEAD_DIM), lambda b, h, r: (b, ck + h)),
                  pl.BlockSpec((seq, HEAD_DIM), lambda b, h, r: (b, cv + h)),
                  pl.BlockSpec((ctx_len, HEAD_DIM), lambda b, h, r: (b, ck + h)),
                  pl.BlockSpec((ctx_len, HEAD_DIM), lambda b, h, r: (b, cv + h)),
                  slab_spec, slab_spec],
        out_specs=pl.BlockSpec((tq, HEAD_DIM), lambda b, h, r: (b * nblk + r, h)),
        out_shape=jax.ShapeDtypeStruct((batch * seq, NA_W), BF16),
        scratch_shapes=[pltpu.VMEM((tq, nk), F32)],
        compiler_params=_cparams(("parallel", "parallel", "arbitrary")),
        name="neighbourhood_attn",
    )(p_lat, p_lat, p_lat, p_ctx, p_ctx, slabs_left, slabs_right)


def _ctx_attn_kernel(q_ref, k_ref, v_ref, o_ref, *, scale):
    s = _dot_nt(q_ref[...], k_ref[...])
    if scale is not None:
        s = s * scale
    m = jnp.max(s, axis=-1, keepdims=True)
    p = jnp.exp(s - m)
    den = jnp.sum(p, axis=-1, keepdims=True)
    o_ref[...] = (_dot(p.astype(BF16), v_ref[...]) / den).astype(o_ref.dtype)


def _ctx_attn_call(q_arr, k_arr, v_arr, batch, ctx_len, n_heads, q_col, k_col, v_col, kv_group, scale):
    return pl.pallas_call(
        functools.partial(_ctx_attn_kernel, scale=scale),
        grid=(batch, n_heads),
        in_specs=[pl.BlockSpec((ctx_len, HEAD_DIM), lambda b, h: (b, q_col + h)),
                  pl.BlockSpec((ctx_len, HEAD_DIM), lambda b, h: (b, k_col + h // kv_group)),
                  pl.BlockSpec((ctx_len, HEAD_DIM), lambda b, h: (b, v_col + h // kv_group))],
        out_specs=pl.BlockSpec((ctx_len, HEAD_DIM), lambda b, h: (b, h)),
        out_shape=jax.ShapeDtypeStruct((batch * ctx_len, n_heads * HEAD_DIM), BF16),
        compiler_params=_cparams(("parallel", "parallel")),
        name="ctx_attn",
    )(q_arr, k_arr, v_arr)


def _rope(x, cos, sin_lo, sin_hi):
    return (x * cos + pltpu.roll(x, HEAD_DIM - HEAD_DIM // 4, 1) * sin_lo
            + pltpu.roll(x, HEAD_DIM // 4, 1) * sin_hi)


def _gqa_prep_kernel(q_ref, k_ref, qg_ref, kg_ref, cos_ref, slo_ref, shi_ref, qo_ref, ko_ref, *, rope):
    def prep(x, gain, scale):
        y = _rms(x.astype(F32), gain)
        if rope:
            y = _rope(y, cos_ref[...], slo_ref[...], shi_ref[...])
        if scale is not None:
            y = y * scale
        return y.astype(BF16)

    for h in range(GQA_Q_HEADS):
        sl = slice(h * HEAD_DIM, (h + 1) * HEAD_DIM)
        qo_ref[:, sl] = prep(q_ref[:, sl], qg_ref[...], HEAD_DIM ** -0.5)
    for h in range(GQA_KV_HEADS):
        sl = slice(h * HEAD_DIM, (h + 1) * HEAD_DIM)
        ko_ref[:, sl] = prep(k_ref[:, sl], kg_ref[...], None)


def _rope_tables(n_tokens):
    t = jnp.arange(n_tokens)
    row = (t // GRID_W).astype(F32)
    col = (t % GRID_W).astype(F32)
    half = HEAD_DIM // 2
    inv_freq = 1.0 / (ROPE_THETA ** (jnp.arange(0, half, 2, dtype=F32) / half))
    ang_r = row[:, None] * inv_freq[None, :]
    ang_c = col[:, None] * inv_freq[None, :]
    ang = jnp.concatenate([ang_r, ang_r, ang_c, ang_c], axis=-1)
    cos, sin = jnp.cos(ang), jnp.sin(ang)
    lo = (jnp.arange(HEAD_DIM) % half) < (half // 2)
    return cos, jnp.where(lo, -sin, 0.0), jnp.where(lo, 0.0, sin)


def _gqa_prep_call(p, q_gain, k_gain, tables, seq, *, rope, tm=512):
    t = p.shape[0]
    per_seq = seq // tm if rope else 1
    tab_spec = pl.BlockSpec((tm, HEAD_DIM), lambda i: (i % per_seq, 0))
    return pl.pallas_call(
        functools.partial(_gqa_prep_kernel, rope=rope),
        grid=(t // tm,),
        in_specs=[pl.BlockSpec((tm, GQA_W), lambda i: (i, COL_GQ_Q // GQA_W)),
                  pl.BlockSpec((tm, KV_W), lambda i: (i, COL_GQ_K // KV_W)),
                  pl.BlockSpec((1, HEAD_DIM), lambda i: (0, 0)),
                  pl.BlockSpec((1, HEAD_DIM), lambda i: (0, 0)),
                  tab_spec, tab_spec, tab_spec],
        out_specs=[pl.BlockSpec((tm, GQA_W), lambda i: (i, 0)),
                   pl.BlockSpec((tm, KV_W), lambda i: (i, 0))],
        out_shape=[jax.ShapeDtypeStruct((t, GQA_W), BF16),
                   jax.ShapeDtypeStruct((t, KV_W), BF16)],
        compiler_params=_cparams(("parallel",)),
        name="gqa_prep",
    )(p, p, q_gain, k_gain, *tables)


def _gqa_kernel(q_ref, kl_ref, vl_ref, kc_ref, vc_ref, o_ref, *, tk):
    tq = q_ref.shape[0]
    grp = q_ref.shape[1] // HEAD_DIM
    q = jnp.concatenate([q_ref[:, g * HEAD_DIM:(g + 1) * HEAD_DIM] for g in range(grp)], axis=0)

    n_tiles = kl_ref.shape[0] // tk
    s = _dot_nt(q, kc_ref[...])
    s_next = _dot_nt(q, kl_ref[0:tk, :])
    m = jnp.max(s, axis=-1, keepdims=True)
    p = jnp.exp(s - m)
    den = jnp.sum(p, axis=-1, keepdims=True)
    acc = _dot(p.astype(BF16), vc_ref[...])
    for t in range(n_tiles):
        s = s_next
        if t + 1 < n_tiles:
            s_next = _dot_nt(q, kl_ref[(t + 1) * tk:(t + 2) * tk, :])
        m_new = jnp.maximum(m, jnp.max(s, axis=-1, keepdims=True))
        p = jnp.exp(s - m_new)
        alpha = jnp.exp(m - m_new)
        den = alpha * den + jnp.sum(p, axis=-1, keepdims=True)
        acc = alpha * acc + _dot(p.astype(BF16), vl_ref[t * tk:(t + 1) * tk, :])
        m = m_new
    out = (acc / den).astype(o_ref.dtype)
    for g in range(grp):
        o_ref[:, g * HEAD_DIM:(g + 1) * HEAD_DIM] = out[g * tq:(g + 1) * tq]


def _gqa_call(q_lat, k_lat, p_lat, k_ctx, p_ctx, batch, seq, ctx_len, tq=512, tk=512):
    grp = GQA_Q_HEADS // GQA_KV_HEADS
    nq = seq // tq
    cv = COL_GQ_V // HEAD_DIM
    return pl.pallas_call(
        functools.partial(_gqa_kernel, tk=tk),
        grid=(batch, GQA_KV_HEADS, nq),
        in_specs=[pl.BlockSpec((tq, grp * HEAD_DIM), lambda b, h, i: (b * nq + i, h)),
                  pl.BlockSpec((seq, HEAD_DIM), lambda b, h, i: (b, h)),
                  pl.BlockSpec((seq, HEAD_DIM), lambda b, h, i: (b, cv + h)),
                  pl.BlockSpec((ctx_len, HEAD_DIM), lambda b, h, i: (b, h)),
                  pl.BlockSpec((ctx_len, HEAD_DIM), lambda b, h, i: (b, cv + h))],
        out_specs=pl.BlockSpec((tq, grp * HEAD_DIM), lambda b, h, i: (b * nq + i, h)),
        out_shape=jax.ShapeDtypeStruct((batch * seq, GQA_W), BF16),
        compiler_params=_cparams(("parallel", "parallel", "arbitrary")),
        name="gqa_attn",
    )(q_lat, k_lat, p_lat, k_ctx, p_ctx)


ST_R, ST_W, ST_BTOT, ST_AMAX, ST_MPREV, ST_DEC, ST_INP, ST_BIGR, ST_M = range(9)
N_STATS = 9
REP_R, REP_M = range(2)
ML_AUG = 2 * HEAD_DIM
ML_GROUP = 8


def _mlstm_kernel(ql_ref, kl_ref, vl_ref, ol_ref, qc_ref, kc_ref, vc_ref, oc_ref, g_ref, gb_ref,
                  cwq_ref, cwk_ref, cbq_ref, cbk_ref, on_ref, yl_ref, *rest,
                  n_lat, n_ctx, need_ctx):
    if need_ctx:
        yc_ref = rest[0]
        rest = rest[1:]
    qa_l, kt_l, va_l, qa_c, kt_c, va_c, hf_l, hb_l, hf_c, hb_c, st, rep = rest
    L = ML_CHUNK
    n_all = n_lat + n_ctx
    h = pl.program_id(1)
    row = lax.broadcasted_iota(jnp.int32, (L, L), 0)
    col = lax.broadcasted_iota(jnp.int32, (L, L), 1)

    tri_fw = (row <= col).astype(F32)
    tri_bw = (row >= col).astype(F32)
    b_cum = []
    for d, tri in enumerate((tri_fw, tri_bw)):
        log_i = g_ref[2 * d * ML_HEADS + h] + gb_ref[2 * d * ML_HEADS + h]
        log_f = jax.nn.log_sigmoid(g_ref[(2 * d + 1) * ML_HEADS + h] + gb_ref[(2 * d + 1) * ML_HEADS + h])
        b = jnp.dot(log_f, tri, preferred_element_type=F32, precision=lax.Precision.HIGHEST)
        b_tot = b[:, L - 1:L] if d == 0 else b[:, 0:1]
        a = b_tot - b + log_i
        a_max = jnp.max(a, axis=-1, keepdims=True)
        r = log_i - b
        lane = lax.broadcasted_iota(jnp.int32, r.shape, 1)
        run = r
        shift = 1
        while shift < L:
            if d == 0:
                run = jnp.maximum(run, jnp.where(lane >= shift, pltpu.roll(run, shift, 1), NEG_BIG))
            else:
                run = jnp.maximum(run, jnp.where(lane < L - shift, pltpu.roll(run, L - shift, 1), NEG_BIG))
            shift *= 2
        base = d * N_STATS
        st[base + ST_R] = r
        st[base + ST_W] = jnp.exp(a - a_max)
        st[base + ST_BTOT] = jnp.broadcast_to(b_tot, b.shape)
        st[base + ST_AMAX] = jnp.broadcast_to(a_max, b.shape)
        st[base + ST_BIGR] = run
        st[base + ST_MPREV] = jnp.zeros_like(b)
        b_cum.append(b)

    def stabiliser_step(i, carry):
        new = []
        for d, m_prev in enumerate(carry):
            in_ctx = i < n_ctx
            j = jnp.where(in_ctx, i, i - n_ctx)
            if d == 0:
                n = jnp.where(in_ctx, n_lat + j, j)
            else:
                n = jnp.where(in_ctx, n_lat + n_ctx - 1 - j, n_lat - 1 - j)
            base = d * N_STATS
            b_tot = st[base + ST_BTOT, pl.ds(n, 1), :]
            a_max = st[base + ST_AMAX, pl.ds(n, 1), :]
            m_new = jnp.maximum(b_tot + m_prev, a_max)
            st[base + ST_MPREV, pl.ds(n, 1), :] = m_prev
            st[base + ST_DEC, pl.ds(n, 1), :] = jnp.exp(b_tot + m_prev - m_new)
            st[base + ST_INP, pl.ds(n, 1), :] = jnp.exp(a_max - m_new)
            new.append(m_new)
        return tuple(new)

    lax.fori_loop(0, n_all, stabiliser_step, (jnp.zeros((1, L), F32), jnp.zeros((1, L), F32)))

    for d in range(2):
        base = d * N_STATS
        big_r = jnp.maximum(st[base + ST_BIGR], st[base + ST_MPREV])
        st[base + ST_BIGR] = big_r
        st[base + ST_M] = b_cum[d] + big_r

    def column_tables(n, carry):
        for d in range(2):
            for k_rep, k_st in ((REP_R, ST_BIGR), (REP_M, ST_M)):
                rows = jnp.broadcast_to(st[d * N_STATS + k_st, pl.ds(n, 1), :], (L, L))
                rep[2 * d + k_rep, n] = rows.T
        return carry

    lax.fori_loop(0, n_all, column_tables, 0, unroll=2)

    def prepare_stream(q_src, k_src, v_src, q_dst, kt_dst, va_dst, n_chunks):
        n_tok = n_chunks * L
        sub = lax.broadcasted_iota(jnp.int32, (L, HEAD_DIM), 0)

        def conv_silu(src_ref, w_ref, b_ref, n, t0):
            x = src_ref[pl.ds(t0, L), :].astype(F32)
            prev_t0 = pl.multiple_of(jnp.maximum(t0 - BF16_ROWS, 0), BF16_ROWS)
            next_t0 = pl.multiple_of(jnp.minimum(t0 + L, n_tok - BF16_ROWS), BF16_ROWS)
            prev_row = (src_ref[pl.ds(prev_t0, BF16_ROWS), :].astype(F32)[BF16_ROWS - 1:, :]
                        * jnp.where(n > 0, 1.0, 0.0))
            next_row = (src_ref[pl.ds(next_t0, BF16_ROWS), :].astype(F32)[:1, :]
                        * jnp.where(n < n_chunks - 1, 1.0, 0.0))
            x_prev = jnp.where(sub == 0, prev_row, pltpu.roll(x, 1, 0))
            x_next = jnp.where(sub == L - 1, next_row, pltpu.roll(x, L - 1, 0))
            y = w_ref[0:1, :] * x_prev + w_ref[1:2, :] * x + w_ref[2:3, :] * x_next + b_ref[...]
            return y * jax.nn.sigmoid(y)

        def body(n, carry):
            t0 = pl.multiple_of(n * L, L)
            q_dst[pl.ds(t0, L), :] = conv_silu(q_src, cwq_ref, cbq_ref, n, t0).astype(BF16)
            k_act = conv_silu(k_src, cwk_ref, cbk_ref, n, t0) * (HEAD_DIM ** -0.5)
            kt_dst[:, pl.ds(t0, L)] = k_act.T
            va_dst[pl.ds(t0, L), 0:HEAD_DIM] = v_src[pl.ds(t0, L), :]
            va_dst[pl.ds(t0, L), HEAD_DIM:ML_AUG] = jnp.ones((L, HEAD_DIM), BF16)
            return carry

        lax.fori_loop(0, n_chunks, body, 0, unroll=2)

    prepare_stream(ql_ref, kl_ref, vl_ref, qa_l, kt_l, va_l, n_lat)
    prepare_stream(qc_ref, kc_ref, vc_ref, qa_c, kt_c, va_c, n_ctx)

    def stat(d, n, k):
        return st[d * N_STATS + k, pl.ds(n, 1), :]

    both = lambda x: jnp.concatenate([x, x], axis=-1)

    def make_body(group, n_chunks, stat0, q_src, kt_src, va_src, h_dst, want_out):
        def body(i, carry):
            items = []
            for u in range(group):
                nf = i * group + u
                for d, n in ((0, nf), (1, n_chunks - 1 - nf)):
                    items.append((d, stat0 + n, pl.multiple_of(n * L, L)))
            va = [va_src[pl.ds(t0, L), :] for _, _, t0 in items]
            c_loc, qk, q = [], [], []
            for (d, ns, t0), v in zip(items, va):
                kt = kt_src[:, pl.ds(t0, L)]
                c_loc.append(_dot((kt * stat(d, ns, ST_W)).astype(BF16), v))
                if want_out:
                    q.append(q_src[pl.ds(t0, L), :])
                    qk.append(_dot(q[-1], kt.astype(BF16)))
            lhs = []
            if want_out:
                for (d, ns, t0), qi, qki in zip(items, q, qk):
                    seen = (col <= row) if d == 0 else (col >= row)
                    big_r = rep[2 * d + REP_R, ns]
                    s = qki * jnp.exp(jnp.where(seen, stat(d, ns, ST_R) - big_r, NEG_BIG))
                    g = jnp.exp(stat(d, ns, ST_MPREV) - big_r)
                    lhs.append(jnp.concatenate([(g * qi.astype(F32)).astype(BF16), s.astype(BF16)], axis=1))
            state = list(carry)
            for idx, (d, ns, t0) in enumerate(items):
                if want_out:
                    acc = _dot(lhs[idx], jnp.concatenate([state[d].astype(BF16), va[idx]], axis=0))
                    floor = jnp.exp(-rep[2 * d + REP_M, ns])
                    h_dst[d][pl.ds(t0, L), :] = (acc[:, :HEAD_DIM]
                                                 / jnp.maximum(jnp.abs(acc[:, HEAD_DIM:]), floor))
                state[d] = both(stat(d, ns, ST_DEC)) * state[d] + both(stat(d, ns, ST_INP)) * c_loc[idx]
            return tuple(state)
        return body

    def finish(hsum, o_src, t0):
        y = _rms(hsum, on_ref[...]) * jax.nn.sigmoid(o_src[pl.ds(t0, L), :].astype(F32))
        return y.astype(BF16)

    zero_state = jnp.zeros((HEAD_DIM, ML_AUG), F32)
    carry = make_body(n_ctx, n_ctx, n_lat, qa_c, kt_c, va_c, (hf_c, hb_c), need_ctx)(0, (zero_state, zero_state))
    lax.fori_loop(0, n_lat // ML_GROUP, make_body(ML_GROUP, n_lat, 0, qa_l, kt_l, va_l, (hf_l, hb_l), True),
                  carry)

    def finish_lat(n, carry):
        t0 = pl.multiple_of(n * L, L)
        yl_ref[pl.ds(t0, L), :] = finish(hf_l[pl.ds(t0, L), :] + hb_l[pl.ds(t0, L), :], ol_ref, t0)
        return carry

    lax.fori_loop(0, n_lat, finish_lat, 0, unroll=4)
    if need_ctx:
        def finish_ctx(n, carry):
            t0 = pl.multiple_of(n * L, L)
            yc_ref[pl.ds(t0, L), :] = finish(hf_c[pl.ds(t0, L), :] + hb_c[pl.ds(t0, L), :], oc_ref, t0)
            return carry

        lax.fori_loop(0, n_ctx, finish_ctx, 0, unroll=True)


def _mlstm_call(p_lat, p_ctx, gates, gate_b, conv_w, conv_b, out_norm, batch, seq, ctx_len, need_ctx):
    n_lat, n_ctx = seq // ML_CHUNK, ctx_len // ML_CHUNK
    n_pad = gates.shape[2]
    cq, ck = COL_ML_Q // HEAD_DIM, COL_ML_K // HEAD_DIM
    cv, co = COL_ML_V // HEAD_DIM, COL_ML_O // HEAD_DIM
    col = lambda rows, c0: pl.BlockSpec((rows, HEAD_DIM), lambda b, h: (b, c0 + h))
    out_specs = [pl.BlockSpec((seq, HEAD_DIM), lambda b, h: (b, h))]
    out_shape = [jax.ShapeDtypeStruct((batch * seq, ML_W), BF16)]
    if need_ctx:
        out_specs.append(pl.BlockSpec((ctx_len, HEAD_DIM), lambda b, h: (b, h)))
        out_shape.append(jax.ShapeDtypeStruct((batch * ctx_len, ML_W), BF16))

    def stream_scratch(n_tok):
        return [pltpu.VMEM((n_tok, HEAD_DIM), BF16), pltpu.VMEM((HEAD_DIM, n_tok), F32),
                pltpu.VMEM((n_tok, ML_AUG), BF16)]

    res = pl.pallas_call(
        functools.partial(_mlstm_kernel, n_lat=n_lat, n_ctx=n_ctx, need_ctx=need_ctx),
        grid=(batch, ML_HEADS),
        in_specs=[col(seq, cq), col(seq, ck), col(seq, cv), col(seq, co),
                  col(ctx_len, cq), col(ctx_len, ck), col(ctx_len, cv), col(ctx_len, co),
                  pl.BlockSpec((N_GATES, None, n_pad, ML_CHUNK), lambda b, h: (0, b, 0, 0)),
                  pl.BlockSpec((N_GATES, 1, ML_CHUNK), lambda b, h: (0, 0, 0)),
                  pl.BlockSpec((3, HEAD_DIM), lambda b, h: (0, h)),
                  pl.BlockSpec((3, HEAD_DIM), lambda b, h: (0, ML_HEADS + h)),
                  pl.BlockSpec((1, HEAD_DIM), lambda b, h: (0, h)),
                  pl.BlockSpec((1, HEAD_DIM), lambda b, h: (0, ML_HEADS + h)),
                  pl.BlockSpec((1, HEAD_DIM), lambda b, h: (0, h))],
        out_specs=out_specs,
        out_shape=out_shape,
        scratch_shapes=(stream_scratch(seq) + stream_scratch(ctx_len)
                        + [pltpu.VMEM((seq, HEAD_DIM), F32), pltpu.VMEM((seq, HEAD_DIM), F32),
                           pltpu.VMEM((ctx_len, HEAD_DIM), F32), pltpu.VMEM((ctx_len, HEAD_DIM), F32),
                           pltpu.VMEM((2 * N_STATS, n_pad, ML_CHUNK), F32),
                           pltpu.VMEM((4, n_lat + n_ctx, ML_CHUNK, ML_CHUNK), F32)]),
        compiler_params=_cparams(("parallel", "parallel")),
        name="mlstm",
    )(p_lat, p_lat, p_lat, p_lat, p_ctx, p_ctx, p_ctx, p_ctx, gates, gate_b,
      conv_w, conv_w, conv_b, conv_b, out_norm)
    return (res[0], res[1]) if need_ctx else (res[0], None)


def kernel(x, c, ctx, c_ctx, w_ada, b_ada, norm_ff1, ff1_gate, ff1_up, ff1_down, norm_mix, w_in, na_rpb,
           gqa_q_norm, gqa_k_norm, ml_conv_w, ml_conv_b, ml_gate_b, ml_out_norm, w_out, norm_ff2,
           ff2_gate, ff2_up, ff2_down, final_norm):
    batch, seq, d = x.shape
    ctx_len = ctx.shape[1]
    depth = w_ada.shape[0]
    ctx_group = batch
    n_lat_chunks, n_ctx_chunks = seq // ML_CHUNK, ctx_len // ML_CHUNK
    n_chunk_pad = -(-(n_lat_chunks + n_ctx_chunks) // 8) * 8

    xl = x.reshape(batch * seq, d)
    xc = ctx.reshape(batch * ctx_len, d)
    cond = jnp.zeros((8, d), F32).at[:batch].set(c).at[batch].set(c_ctx)
    rope_tabs = _rope_tables(seq)
    w_in_t = jnp.swapaxes(w_in, 1, 2)
    row2 = lambda v: v.reshape(1, -1)
    fin = row2(final_norm)

    for l in range(depth):
        last = l == depth - 1
        mod = _ada_call(cond, w_ada, l, row2(b_ada[l])).reshape(8 * N_MOD, 1, d)
        lat = dict(group_rows=seq, fixed_group=None)
        cx = dict(group_rows=None, fixed_group=ctx_group)

        cast = lambda w: _cast_call(w, l)
        wg, wu, wd = cast(ff1_gate), cast(ff1_up), cast(ff1_down)
        g1 = row2(norm_ff1[l])
        xl = _ffn_call(xl, mod, 0, g1, wg, wu, wd, fin, final_norm=False, **lat)
        xc = _ffn_call(xc, mod, 0, g1, wg, wu, wd, fin, final_norm=False, **cx)

        w_main_t = _cast_call(w_in_t, l, COL_GATES)
        w_gates_t = w_in_t[l, COL_GATES:]
        gm = row2(norm_mix[l])
        p_lat, gt_lat = _inproj_call(xl, mod, gm, w_main_t, w_gates_t, **lat)
        p_ctx, gt_ctx = _inproj_call(xc, mod, gm, w_main_t, w_gates_t, **cx)

        ya_lat = _na_call(p_lat, p_ctx, *_na_toeplitz_slabs(na_rpb[l]), batch, seq, ctx_len)
        qg, kg = row2(gqa_q_norm[l]), row2(gqa_k_norm[l])
        qn_lat, kn_lat = _gqa_prep_call(p_lat, qg, kg, rope_tabs, seq, rope=True)
        qn_ctx, kn_ctx = _gqa_prep_call(p_ctx, qg, kg, rope_tabs, seq, rope=False, tm=ctx_len)
        yb_lat = _gqa_call(qn_lat, kn_lat, p_lat, kn_ctx, p_ctx, batch, seq, ctx_len)
        gates = jnp.concatenate(
            [gt_lat.reshape(N_GATES, batch, n_lat_chunks, ML_CHUNK),
             gt_ctx.reshape(N_GATES, batch, n_ctx_chunks, ML_CHUNK),
             jnp.zeros((N_GATES, batch, n_chunk_pad - n_lat_chunks - n_ctx_chunks, ML_CHUNK), F32)], axis=2)
        gate_b = jnp.broadcast_to(ml_gate_b[l][:, None, None], (N_GATES, 1, ML_CHUNK))
        yc_lat, yc_ctx = _mlstm_call(p_lat, p_ctx, gates, gate_b, ml_conv_w[l], row2(ml_conv_b[l]),
                                     row2(ml_out_norm[l]), batch, seq, ctx_len, not last)

        wo = cast(w_out)
        xl = _outproj_call(xl, mod, ya_lat, yb_lat, yc_lat, wo, **lat)
        wg, wu, wd = cast(ff2_gate), cast(ff2_up), cast(ff2_down)
        g2 = row2(norm_ff2[l])
        xl = _ffn_call(xl, mod, 6, g2, wg, wu, wd, fin, final_norm=last, **lat)
        if not last:
            ya_ctx = _ctx_attn_call(p_ctx, p_ctx, p_ctx, batch, ctx_len, NA_HEADS,
                                    COL_NA_Q // HEAD_DIM, COL_NA_K // HEAD_DIM, COL_NA_V // HEAD_DIM, 1,
                                    HEAD_DIM ** -0.5)
            yb_ctx = _ctx_attn_call(qn_ctx, kn_ctx, p_ctx, batch, ctx_len, GQA_Q_HEADS, 0, 0,
                                    COL_GQ_V // HEAD_DIM, GQA_Q_HEADS // GQA_KV_HEADS, None)
            xc = _outproj_call(xc, mod, ya_ctx, yb_ctx, yc_ctx, wo, **cx)
            xc = _ffn_call(xc, mod, 6, g2, wg, wu, wd, fin, final_norm=False, **cx)
    return xl.reshape(batch, seq, d)
```

```python
import functools

import numpy as np
import jax
import jax.numpy as jnp
from jax import lax
from jax.experimental import pallas as pl
from jax.experimental.pallas import tpu as pltpu

F32 = jnp.float32
BF16 = jnp.bfloat16

GRID_W = 64
HEAD_DIM = 128
NA_HEADS = 4
GQA_Q_HEADS = 4
GQA_KV_HEADS = 2
ML_HEADS = 8
NA_W = NA_HEADS * HEAD_DIM
GQA_W = GQA_Q_HEADS * HEAD_DIM
KV_W = GQA_KV_HEADS * HEAD_DIM
ML_W = ML_HEADS * HEAD_DIM
NA_WIN_R = 8
NA_WIN_C = 16
ML_CHUNK = 128
ROPE_THETA = 10000.0
N_MOD = 9
EPS = 1e-6
N_GATES = 4 * ML_HEADS
COL_NA_Q = 0
COL_NA_K = NA_W
COL_NA_V = 2 * NA_W
COL_GQ_Q = 3 * NA_W
COL_GQ_K = COL_GQ_Q + GQA_W
COL_GQ_V = COL_GQ_K + KV_W
COL_ML_Q = COL_GQ_V + KV_W
COL_ML_K = COL_ML_Q + ML_W
COL_ML_V = COL_ML_K + ML_W
COL_ML_O = COL_ML_V + ML_W
COL_GATES = COL_ML_O + ML_W

NEG_BIG = -1e30
VMEM_LIMIT_V7X = 56 * 1024 * 1024
BF16_ROWS = 16
NORM_ROWS = 32
FFN_SUBTILE = 256
FFN_OUT_SUBTILE = 1024

NA_QROWS = 8
NA_KROWS = 16
NA_KEY_TILE = 256
NA_MASKED_SLAB = 2 * NA_WIN_R - 1


def _cparams(sem):
    return pltpu.CompilerParams(dimension_semantics=sem, vmem_limit_bytes=VMEM_LIMIT_V7X)


def _rms(x, gain):
    return x * lax.rsqrt(jnp.mean(x * x, axis=-1, keepdims=True) + EPS) * gain


def _norm_modulate_store(x_ref, gain_ref, scale_ref, shift_ref, h_ref, zero_ref):
    gain_scale = gain_ref[...] * (1.0 + scale_ref[...])
    shift = shift_ref[...]

    def body(i, carry):
        r0 = pl.multiple_of(i * NORM_ROWS, NORM_ROWS)
        x = x_ref[pl.ds(r0, NORM_ROWS), :]
        inv = lax.rsqrt(jnp.mean(x * x, axis=-1, keepdims=True) + EPS)
        h_ref[pl.ds(r0, NORM_ROWS), :] = (x * inv * gain_scale + shift).astype(h_ref.dtype)
        zero_ref[pl.ds(r0, NORM_ROWS), :] = jnp.zeros((NORM_ROWS, zero_ref.shape[1]), zero_ref.dtype)
        return carry

    lax.fori_loop(0, x_ref.shape[0] // NORM_ROWS, body, 0, unroll=4)


def _dot(a, b):
    return jnp.dot(a, b, preferred_element_type=F32)


def _dot_nt(a, b):
    return lax.dot_general(a, b, (((1,), (1,)), ((), ())), preferred_element_type=F32)


def _dot_tn(a, b):
    return lax.dot_general(a, b, (((0,), (0,)), ((), ())), preferred_element_type=F32)


def _cast_kernel(w_ref, o_ref):
    o_ref[...] = w_ref[...].astype(o_ref.dtype)


def _cast_call(w_stack, layer, n_rows=None, row_blocks=8):
    _, rows, n_cols = w_stack.shape
    n_rows = rows if n_rows is None else n_rows
    tr = n_rows // row_blocks
    return pl.pallas_call(
        _cast_kernel,
        grid=(row_blocks,),
        in_specs=[pl.BlockSpec((None, tr, n_cols), lambda i: (layer, i, 0))],
        out_specs=pl.BlockSpec((tr, n_cols), lambda i: (i, 0)),
        out_shape=jax.ShapeDtypeStruct((n_rows, n_cols), BF16),
        compiler_params=_cparams(("parallel",)),
        name="weight_cast",
    )(w_stack)


def _ada_kernel(c_ref, w_ref, b_ref, o_ref):
    c = c_ref[...]
    a = (c * jax.nn.sigmoid(c)).astype(BF16)
    o_ref[...] = _dot(a, w_ref[...].astype(BF16)) + b_ref[...]


def _ada_call(cond, w_stack, layer, b, tn=1024):
    m, d = cond.shape
    n = w_stack.shape[2]
    return pl.pallas_call(
        _ada_kernel,
        grid=(n // tn,),
        in_specs=[pl.BlockSpec((m, d), lambda j: (0, 0)),
                  pl.BlockSpec((None, d, tn), lambda j: (layer, 0, j)),
                  pl.BlockSpec((1, tn), lambda j: (0, j))],
        out_specs=pl.BlockSpec((m, tn), lambda j: (0, j)),
        out_shape=jax.ShapeDtypeStruct((m, n), F32),
        compiler_params=_cparams(("arbitrary",)),
        name="ada_mod",
    )(cond, w_stack, b)


def _mod_spec(k, tm, group_rows, fixed_group):
    def index_map(i, *_):
        grp = fixed_group if fixed_group is not None else i // (group_rows // tm)
        return (grp * N_MOD + k, 0, 0)
    return index_map


def _ffn_kernel(x_ref, sh_ref, sc_ref, gt_ref, gain_ref, wg_ref, wu_ref, wd_ref, fin_ref, o_ref,
                h_scr, *, n_ff, final_norm):
    j = pl.program_id(1)

    @pl.when(j == 0)
    def _():
        _norm_modulate_store(x_ref, gain_ref, sc_ref, sh_ref, h_scr, o_ref)

    h = h_scr[...]
    tf = wg_ref.shape[1]
    for c0 in range(0, tf, FFN_SUBTILE):
        g = _dot(h, wg_ref[:, c0:c0 + FFN_SUBTILE])
        u = _dot(h, wu_ref[:, c0:c0 + FFN_SUBTILE])
        a = ((g * jax.nn.sigmoid(g)) * u).astype(BF16)
        for n0 in range(0, o_ref.shape[1], FFN_OUT_SUBTILE):
            o_ref[:, n0:n0 + FFN_OUT_SUBTILE] += _dot(a, wd_ref[c0:c0 + FFN_SUBTILE, n0:n0 + FFN_OUT_SUBTILE])

    @pl.when(j == n_ff - 1)
    def _():
        half_gate = 0.5 * gt_ref[...]

        def body(i, carry):
            r0 = pl.multiple_of(i * NORM_ROWS, NORM_ROWS)
            out = x_ref[pl.ds(r0, NORM_ROWS), :] + half_gate * o_ref[pl.ds(r0, NORM_ROWS), :]
            if final_norm:
                out = _rms(out, fin_ref[...])
            o_ref[pl.ds(r0, NORM_ROWS), :] = out
            return carry

        lax.fori_loop(0, o_ref.shape[0] // NORM_ROWS, body, 0, unroll=4)


def _ffn_call(x, mod, k0, gain, wg, wu, wd, fin, *, group_rows, fixed_group, final_norm,
              tm=1024, tf=512):
    t, d = x.shape
    dff = wg.shape[1]
    n_ff = dff // tf
    mspec = lambda k: pl.BlockSpec((None, 1, d), _mod_spec(k, tm, group_rows, fixed_group))
    return pl.pallas_call(
        functools.partial(_ffn_kernel, n_ff=n_ff, final_norm=final_norm),
        grid=(t // tm, n_ff),
        in_specs=[pl.BlockSpec((tm, d), lambda i, j: (i, 0)),
                  mspec(k0), mspec(k0 + 1), mspec(k0 + 2),
                  pl.BlockSpec((1, d), lambda i, j: (0, 0)),
                  pl.BlockSpec((d, tf), lambda i, j: (0, j)),
                  pl.BlockSpec((d, tf), lambda i, j: (0, j)),
                  pl.BlockSpec((tf, d), lambda i, j: (j, 0)),
                  pl.BlockSpec((1, d), lambda i, j: (0, 0))],
        out_specs=pl.BlockSpec((tm, d), lambda i, j: (i, 0)),
        out_shape=jax.ShapeDtypeStruct((t, d), F32),
        scratch_shapes=[pltpu.VMEM((tm, d), BF16)],
        compiler_params=_cparams(("parallel", "arbitrary")),
        name="macaron_ffn",
    )(x, mod, mod, mod, gain, wg, wu, wd, fin)


def _rope(x, cos, sin_lo, sin_hi):
    return (x * cos + pltpu.roll(x, HEAD_DIM - HEAD_DIM // 4, 1) * sin_lo
            + pltpu.roll(x, HEAD_DIM // 4, 1) * sin_hi)


def _inproj_kernel(x_ref, sh_ref, sc_ref, gain_ref, wt_ref, wgt_ref, qg_ref, kg_ref, cos_ref, slo_ref, shi_ref,
                   p_ref, gt_ref, h_scr, *, rope):
    j = pl.program_id(1)

    def gqa_head(x, gain, scale):
        y = _rms(x, gain)
        if rope:
            y = _rope(y, cos_ref[...], slo_ref[...], shi_ref[...])
        return y if scale is None else y * scale

    @pl.when(j == 0)
    def _():
        h = _rms(x_ref[...], gain_ref[...]) * (1.0 + sc_ref[...]) + sh_ref[...]
        hb = h.astype(BF16)
        h_scr[...] = hb
        gt_ref[...] = _dot_nt(wgt_ref[...].astype(BF16), hb)
        acc = _dot_nt(hb, wt_ref[...])
        p_ref[:, :COL_GQ_Q] = acc[:, :COL_GQ_Q].astype(p_ref.dtype)
        for c0, n_heads, gain_ref_h, scale in ((COL_GQ_Q, GQA_Q_HEADS, qg_ref, HEAD_DIM ** -0.5),
                                               (COL_GQ_K, GQA_KV_HEADS, kg_ref, None)):
            for hd in range(n_heads):
                sl = slice(c0 + hd * HEAD_DIM, c0 + (hd + 1) * HEAD_DIM)
                p_ref[:, sl] = gqa_head(acc[:, sl], gain_ref_h[...], scale).astype(p_ref.dtype)
        p_ref[:, COL_GQ_V:] = acc[:, COL_GQ_V:].astype(p_ref.dtype)

    @pl.when(j != 0)
    def _():
        p_ref[...] = _dot_nt(h_scr[...], wt_ref[...]).astype(p_ref.dtype)


def _inproj_call(x, mod, gain, w_main_t, w_gates_t, q_gain, k_gain, rope_tables, seq, *, rope, group_rows,
                 fixed_group, tm=512, tn=3328):
    t, d = x.shape
    n = w_main_t.shape[0]
    assert tn >= COL_GQ_V, "mixer B's q / k columns must sit in the first column tile"
    mspec = lambda k: pl.BlockSpec((None, 1, d), _mod_spec(k, tm, group_rows, fixed_group))
    per_seq = seq // tm if rope else 1
    tab_spec = pl.BlockSpec((tm, HEAD_DIM), lambda i, j: (i % per_seq, 0))
    head_row = pl.BlockSpec((1, HEAD_DIM), lambda i, j: (0, 0))
    return pl.pallas_call(
        functools.partial(_inproj_kernel, rope=rope),
        grid=(t // tm, n // tn),
        in_specs=[pl.BlockSpec((tm, d), lambda i, j: (i, 0)),
                  mspec(3), mspec(4),
                  pl.BlockSpec((1, d), lambda i, j: (0, 0)),
                  pl.BlockSpec((tn, d), lambda i, j: (j, 0)),
                  pl.BlockSpec((N_GATES, d), lambda i, j: (0, 0)),
                  head_row, head_row, tab_spec, tab_spec, tab_spec],
        out_specs=[pl.BlockSpec((tm, tn), lambda i, j: (i, j)),
                   pl.BlockSpec((N_GATES, tm), lambda i, j: (0, i))],
        out_shape=[jax.ShapeDtypeStruct((t, n), BF16),
                   jax.ShapeDtypeStruct((N_GATES, t), F32)],
        scratch_shapes=[pltpu.VMEM((tm, d), BF16)],
        compiler_params=_cparams(("parallel", "arbitrary")),
        name="mixer_in_proj",
    )(x, mod, mod, gain, w_main_t, w_gates_t, q_gain, k_gain, *rope_tables)


def _outproj_kernel(x_ref, gt_ref, ya_ref, yb_ref, yc_ref, w_ref, o_ref):
    acc = _dot(ya_ref[...], w_ref[0:NA_W, :])
    acc += _dot(yb_ref[...], w_ref[NA_W:NA_W + GQA_W, :])
    acc += _dot(yc_ref[...], w_ref[NA_W + GQA_W:, :])
    o_ref[...] = x_ref[...] + gt_ref[...] * acc


def _outproj_call(x, mod, ya, yb, yc, w, *, group_rows, fixed_group, tm=512):
    t, d = x.shape
    return pl.pallas_call(
        _outproj_kernel,
        grid=(t // tm,),
        in_specs=[pl.BlockSpec((tm, d), lambda i: (i, 0)),
                  pl.BlockSpec((None, 1, d), _mod_spec(5, tm, group_rows, fixed_group)),
                  pl.BlockSpec((tm, NA_W), lambda i: (i, 0)),
                  pl.BlockSpec((tm, GQA_W), lambda i: (i, 0)),
                  pl.BlockSpec((tm, ML_W), lambda i: (i, 0)),
                  pl.BlockSpec((d, d), lambda i: (0, 0))],
        out_specs=pl.BlockSpec((tm, d), lambda i: (i, 0)),
        out_shape=jax.ShapeDtypeStruct((t, d), F32),
        compiler_params=_cparams(("parallel",)),
        name="mixer_out_proj",
    )(x, mod, ya, yb, yc, w)


def _na_kernel(q_ref, k_ref, v_ref, kc_ref, vc_ref, tl_ref, tr_ref, o_ref, mb_scr, *, rows):
    rb = pl.program_id(2)
    nblk = rows // NA_QROWS
    key_row0 = jnp.clip(rb * NA_QROWS - NA_WIN_R // 2, 0, rows - NA_KROWS)

    @pl.when((rb <= 1) | (rb == nblk - 1))
    def _():
        for a in range(NA_QROWS):
            r = rb * NA_QROWS + a
            row_start = jnp.clip(r - NA_WIN_R // 2, 0, rows - NA_WIN_R)
            for j in range(NA_KROWS // 2):
                sel = []
                for kr in (key_row0 + 2 * j, key_row0 + 2 * j + 1):
                    in_win = (kr >= row_start) & (kr < row_start + NA_WIN_R)
                    sel.append(jnp.where(in_win, kr - r + (NA_WIN_R - 1), NA_MASKED_SLAB))
                mb_scr[a * GRID_W:(a + 1) * GRID_W, 2 * j * GRID_W:(2 * j + 2) * GRID_W] = (
                    tl_ref[sel[0]] + tr_ref[sel[1]])

    start = pl.multiple_of(key_row0 * GRID_W, GRID_W)
    q = (q_ref[...].astype(F32) * (HEAD_DIM ** -0.5)).astype(BF16)
    n_keys = NA_KROWS * GRID_W
    key_tile = lambda t0: k_ref[pl.ds(start + t0, NA_KEY_TILE), :]
    s = _dot_nt(q, kc_ref[...])
    s_next = _dot_nt(q, key_tile(0))
    m = jnp.max(s, axis=-1, keepdims=True)
    p = jnp.exp(s - m)
    den = jnp.sum(p, axis=-1, keepdims=True)
    acc = _dot(p.astype(BF16), vc_ref[...])
    for t0 in range(0, n_keys, NA_KEY_TILE):
        s = s_next + mb_scr[:, t0:t0 + NA_KEY_TILE]
        if t0 + NA_KEY_TILE < n_keys:
            s_next = _dot_nt(q, key_tile(t0 + NA_KEY_TILE))
        m_new = jnp.maximum(m, jnp.max(s, axis=-1, keepdims=True))
        p = jnp.exp(s - m_new)
        alpha = jnp.exp(m - m_new)
        den = alpha * den + jnp.sum(p, axis=-1, keepdims=True)
        acc = alpha * acc + _dot(p.astype(BF16), v_ref[pl.ds(start + t0, NA_KEY_TILE), :])
        m = m_new
    o_ref[...] = (acc / den).astype(o_ref.dtype)


def _na_toeplitz_slabs(rpb):
    n_h, n_dr, n_dc = rpb.shape
    edge = GRID_W - NA_WIN_C
    ext = jnp.concatenate(
        [rpb[..., NA_WIN_C - 1:], jnp.broadcast_to(rpb[..., n_dc - 1:], (n_h, n_dr, edge)),
         jnp.zeros((n_h, n_dr, 1), F32),
         jnp.broadcast_to(rpb[..., :1], (n_h, n_dr, edge)), rpb[..., :NA_WIN_C - 1]], axis=-1)
    width = 2 * GRID_W
    tiled = jnp.broadcast_to(ext[:, :, None, :], (n_h, n_dr, GRID_W, width)).reshape(n_h, n_dr, GRID_W * width)
    toep = tiled[..., :GRID_W * (width - 1)].reshape(n_h, n_dr, GRID_W, width - 1)[..., :GRID_W]
    cq = np.arange(GRID_W)
    col_start = np.clip(cq - NA_WIN_C // 2, 0, GRID_W - NA_WIN_C)
    col_ok = (cq[None, :] >= col_start[:, None]) & (cq[None, :] < col_start[:, None] + NA_WIN_C)
    slabs = jnp.where(col_ok, toep, NEG_BIG)
    slabs = jnp.concatenate([slabs, jnp.full((n_h, 1, GRID_W, GRID_W), NEG_BIG, F32)], axis=1)
    zeros = jnp.zeros_like(slabs)
    return jnp.concatenate([slabs, zeros], axis=-1), jnp.concatenate([zeros, slabs], axis=-1)


def _na_call(p_lat, p_ctx, slabs_left, slabs_right, batch, seq, ctx_len):
    rows = seq // GRID_W
    nblk = rows // NA_QROWS
    tq = NA_QROWS * GRID_W
    nk = NA_KROWS * GRID_W
    cq, ck, cv = COL_NA_Q // HEAD_DIM, COL_NA_K // HEAD_DIM, COL_NA_V // HEAD_DIM
    slab_spec = pl.BlockSpec((None, NA_MASKED_SLAB + 1, GRID_W, 2 * GRID_W), lambda b, h, r: (h, 0, 0, 0))

    return pl.pallas_call(
        functools.partial(_na_kernel, rows=rows),
        grid=(batch, NA_HEADS, nblk),
        in_specs=[pl.BlockSpec((tq, HEAD_DIM), lambda b, h, r: (b * nblk + r, cq + h)),
                  pl.BlockSpec((seq, HEAD_DIM), lambda b, h, r: (b, ck + h)),
                  pl.BlockSpec((seq, HEAD_DIM), lambda b, h, r: (b, cv + h)),
                  pl.BlockSpec((ctx_len, HEAD_DIM), lambda b, h, r: (b, ck + h)),
                  pl.BlockSpec((ctx_len, HEAD_DIM), lambda b, h, r: (b, cv + h)),
                  slab_spec, slab_spec],
        out_specs=pl.BlockSpec((tq, HEAD_DIM), lambda b, h, r: (b * nblk + r, h)),
        out_shape=jax.ShapeDtypeStruct((batch * seq, NA_W), BF16),
        scratch_shapes=[pltpu.VMEM((tq, nk), F32)],
        compiler_params=_cparams(("parallel", "parallel", "arbitrary")),
        name="neighbourhood_attn",
    )(p_lat, p_lat, p_lat, p_ctx, p_ctx, slabs_left, slabs_right)


def _ctx_attn_kernel(q_ref, k_ref, v_ref, o_ref, *, scale):
    s = _dot_nt(q_ref[...], k_ref[...])
    if scale is not None:
        s = s * scale
    m = jnp.max(s, axis=-1, keepdims=True)
    p = jnp.exp(s - m)
    den = jnp.sum(p, axis=-1, keepdims=True)
    o_ref[...] = (_dot(p.astype(BF16), v_ref[...]) / den).astype(o_ref.dtype)


def _ctx_attn_call(q_arr, k_arr, v_arr, batch, ctx_len, n_heads, q_col, k_col, v_col, kv_group, scale):
    return pl.pallas_call(
        functools.partial(_ctx_attn_kernel, scale=scale),
        grid=(batch, n_heads),
        in_specs=[pl.BlockSpec((ctx_len, HEAD_DIM), lambda b, h: (b, q_col + h)),
                  pl.BlockSpec((ctx_len, HEAD_DIM), lambda b, h: (b, k_col + h // kv_group)),
                  pl.BlockSpec((ctx_len, HEAD_DIM), lambda b, h: (b, v_col + h // kv_group))],
        out_specs=pl.BlockSpec((ctx_len, HEAD_DIM), lambda b, h: (b, h)),
        out_shape=jax.ShapeDtypeStruct((batch * ctx_len, n_heads * HEAD_DIM), BF16),
        compiler_params=_cparams(("parallel", "parallel")),
        name="ctx_attn",
    )(q_arr, k_arr, v_arr)


def _rope_tables(n_tokens):
    t = jnp.arange(n_tokens)
    row = (t // GRID_W).astype(F32)
    col = (t % GRID_W).astype(F32)
    half = HEAD_DIM // 2
    inv_freq = 1.0 / (ROPE_THETA ** (jnp.arange(0, half, 2, dtype=F32) / half))
    ang_r = row[:, None] * inv_freq[None, :]
    ang_c = col[:, None] * inv_freq[None, :]
    ang = jnp.concatenate([ang_r, ang_r, ang_c, ang_c], axis=-1)
    cos, sin = jnp.cos(ang), jnp.sin(ang)
    lo = (jnp.arange(HEAD_DIM) % half) < (half // 2)
    return cos, jnp.where(lo, -sin, 0.0), jnp.where(lo, 0.0, sin)


def _gqa_kernel(q_ref, kl_ref, vl_ref, kc_ref, vc_ref, o_ref, *, tk):
    tq = q_ref.shape[0]
    grp = q_ref.shape[1] // HEAD_DIM
    q = jnp.concatenate([q_ref[:, g * HEAD_DIM:(g + 1) * HEAD_DIM] for g in range(grp)], axis=0)

    n_tiles = kl_ref.shape[0] // tk
    s = _dot_nt(q, kc_ref[...])
    s_next = _dot_nt(q, kl_ref[0:tk, :])
    m = jnp.max(s, axis=-1, keepdims=True)
    p = jnp.exp(s - m)
    den = jnp.sum(p, axis=-1, keepdims=True)
    acc = _dot(p.astype(BF16), vc_ref[...])
    for t in range(n_tiles):
        s = s_next
        if t + 1 < n_tiles:
            s_next = _dot_nt(q, kl_ref[(t + 1) * tk:(t + 2) * tk, :])
        m_new = jnp.maximum(m, jnp.max(s, axis=-1, keepdims=True))
        p = jnp.exp(s - m_new)
        alpha = jnp.exp(m - m_new)
        den = alpha * den + jnp.sum(p, axis=-1, keepdims=True)
        acc = alpha * acc + _dot(p.astype(BF16), vl_ref[t * tk:(t + 1) * tk, :])
        m = m_new
    out = (acc / den).astype(o_ref.dtype)
    for g in range(grp):
        o_ref[:, g * HEAD_DIM:(g + 1) * HEAD_DIM] = out[g * tq:(g + 1) * tq]


def _gqa_call(p_lat, p_ctx, batch, seq, ctx_len, tq=512, tk=2048):
    grp = GQA_Q_HEADS // GQA_KV_HEADS
    nq = seq // tq
    cq = COL_GQ_Q // (grp * HEAD_DIM)
    ck, cv = COL_GQ_K // HEAD_DIM, COL_GQ_V // HEAD_DIM
    return pl.pallas_call(
        functools.partial(_gqa_kernel, tk=tk),
        grid=(batch, GQA_KV_HEADS, nq),
        in_specs=[pl.BlockSpec((tq, grp * HEAD_DIM), lambda b, h, i: (b * nq + i, cq + h)),
                  pl.BlockSpec((seq, HEAD_DIM), lambda b, h, i: (b, ck + h)),
                  pl.BlockSpec((seq, HEAD_DIM), lambda b, h, i: (b, cv + h)),
                  pl.BlockSpec((ctx_len, HEAD_DIM), lambda b, h, i: (b, ck + h)),
                  pl.BlockSpec((ctx_len, HEAD_DIM), lambda b, h, i: (b, cv + h))],
        out_specs=pl.BlockSpec((tq, grp * HEAD_DIM), lambda b, h, i: (b * nq + i, h)),
        out_shape=jax.ShapeDtypeStruct((batch * seq, GQA_W), BF16),
        compiler_params=_cparams(("parallel", "parallel", "arbitrary")),
        name="gqa_attn",
    )(p_lat, p_lat, p_lat, p_ctx, p_ctx)


ST_R, ST_W, ST_BTOT, ST_AMAX, ST_MPREV, ST_DEC, ST_INP, ST_BIGR, ST_M = range(9)
N_STATS = 9
REP_R, REP_M = range(2)
ML_AUG = 2 * HEAD_DIM
ML_GROUP = 8


def _mlstm_kernel(ql_ref, kl_ref, vl_ref, ol_ref, qc_ref, kc_ref, vc_ref, oc_ref, g_ref, gb_ref,
                  cwq_ref, cwk_ref, cbq_ref, cbk_ref, on_ref, yl_ref, *rest,
                  n_lat, n_ctx, need_ctx):
    if need_ctx:
        yc_ref = rest[0]
        rest = rest[1:]
    qa_l, kt_l, va_l, qa_c, kt_c, va_c, hf_l, hb_l, hf_c, hb_c, st, rep = rest
    L = ML_CHUNK
    n_all = n_lat + n_ctx
    h = pl.program_id(1)
    row = lax.broadcasted_iota(jnp.int32, (L, L), 0)
    col = lax.broadcasted_iota(jnp.int32, (L, L), 1)

    tri_fw = (row <= col).astype(F32)
    tri_bw = (row >= col).astype(F32)
    b_cum = []
    for d, tri in enumerate((tri_fw, tri_bw)):
        log_i = g_ref[2 * d * ML_HEADS + h] + gb_ref[2 * d * ML_HEADS + h]
        log_f = jax.nn.log_sigmoid(g_ref[(2 * d + 1) * ML_HEADS + h] + gb_ref[(2 * d + 1) * ML_HEADS + h])
        b = jnp.dot(log_f, tri, preferred_element_type=F32, precision=lax.Precision.HIGHEST)
        b_tot = b[:, L - 1:L] if d == 0 else b[:, 0:1]
        a = b_tot - b + log_i
        a_max = jnp.max(a, axis=-1, keepdims=True)
        r = log_i - b
        lane = lax.broadcasted_iota(jnp.int32, r.shape, 1)
        run = r
        shift = 1
        while shift < L:
            if d == 0:
                run = jnp.maximum(run, jnp.where(lane >= shift, pltpu.roll(run, shift, 1), NEG_BIG))
            else:
                run = jnp.maximum(run, jnp.where(lane < L - shift, pltpu.roll(run, L - shift, 1), NEG_BIG))
            shift *= 2
        base = d * N_STATS
        st[base + ST_R] = r
        st[base + ST_W] = jnp.exp(a - a_max)
        st[base + ST_BTOT] = jnp.broadcast_to(b_tot, b.shape)
        st[base + ST_AMAX] = jnp.broadcast_to(a_max, b.shape)
        st[base + ST_BIGR] = run
        st[base + ST_MPREV] = jnp.zeros_like(b)
        b_cum.append(b)

    def stabiliser_step(i, carry):
        new = []
        for d, m_prev in enumerate(carry):
            in_ctx = i < n_ctx
            j = jnp.where(in_ctx, i, i - n_ctx)
            if d == 0:
                n = jnp.where(in_ctx, n_lat + j, j)
            else:
                n = jnp.where(in_ctx, n_lat + n_ctx - 1 - j, n_lat - 1 - j)
            base = d * N_STATS
            b_tot = st[base + ST_BTOT, pl.ds(n, 1), :]
            a_max = st[base + ST_AMAX, pl.ds(n, 1), :]
            m_new = jnp.maximum(b_tot + m_prev, a_max)
            st[base + ST_MPREV, pl.ds(n, 1), :] = m_prev
            st[base + ST_DEC, pl.ds(n, 1), :] = jnp.exp(b_tot + m_prev - m_new)
            st[base + ST_INP, pl.ds(n, 1), :] = jnp.exp(a_max - m_new)
            new.append(m_new)
        return tuple(new)

    lax.fori_loop(0, n_all, stabiliser_step, (jnp.zeros((1, L), F32), jnp.zeros((1, L), F32)))

    for d in range(2):
        base = d * N_STATS
        big_r = jnp.maximum(st[base + ST_BIGR], st[base + ST_MPREV])
        st[base + ST_BIGR] = big_r
        st[base + ST_M] = b_cum[d] + big_r

    def column_tables(n, carry):
        for d in range(2):
            for k_rep, k_st in ((REP_R, ST_BIGR), (REP_M, ST_M)):
                rows = jnp.broadcast_to(st[d * N_STATS + k_st, pl.ds(n, 1), :], (L, L))
                rep[2 * d + k_rep, n] = rows.T
        return carry

    lax.fori_loop(0, n_all, column_tables, 0, unroll=2)

    def prepare_stream(q_src, k_src, v_src, q_dst, kt_dst, va_dst, n_chunks):
        n_tok = n_chunks * L
        sub = lax.broadcasted_iota(jnp.int32, (L, HEAD_DIM), 0)

        def conv_silu(src_ref, w_ref, b_ref, n, t0):
            x = src_ref[pl.ds(t0, L), :].astype(F32)
            prev_t0 = pl.multiple_of(jnp.maximum(t0 - BF16_ROWS, 0), BF16_ROWS)
            next_t0 = pl.multiple_of(jnp.minimum(t0 + L, n_tok - BF16_ROWS), BF16_ROWS)
            prev_row = (src_ref[pl.ds(prev_t0, BF16_ROWS), :].astype(F32)[BF16_ROWS - 1:, :]
                        * jnp.where(n > 0, 1.0, 0.0))
            next_row = (src_ref[pl.ds(next_t0, BF16_ROWS), :].astype(F32)[:1, :]
                        * jnp.where(n < n_chunks - 1, 1.0, 0.0))
            x_prev = jnp.where(sub == 0, prev_row, pltpu.roll(x, 1, 0))
            x_next = jnp.where(sub == L - 1, next_row, pltpu.roll(x, L - 1, 0))
            y = w_ref[0:1, :] * x_prev + w_ref[1:2, :] * x + w_ref[2:3, :] * x_next + b_ref[...]
            return y * jax.nn.sigmoid(y)

        def body(n, carry):
            t0 = pl.multiple_of(n * L, L)
            q_dst[pl.ds(t0, L), :] = conv_silu(q_src, cwq_ref, cbq_ref, n, t0).astype(BF16)
            k_act = conv_silu(k_src, cwk_ref, cbk_ref, n, t0) * (HEAD_DIM ** -0.5)
            kt_dst[:, pl.ds(t0, L)] = k_act.T
            va_dst[pl.ds(t0, L), 0:HEAD_DIM] = v_src[pl.ds(t0, L), :]
            va_dst[pl.ds(t0, L), HEAD_DIM:ML_AUG] = jnp.ones((L, HEAD_DIM), BF16)
            return carry

        lax.fori_loop(0, n_chunks, body, 0, unroll=2)

    prepare_stream(ql_ref, kl_ref, vl_ref, qa_l, kt_l, va_l, n_lat)
    prepare_stream(qc_ref, kc_ref, vc_ref, qa_c, kt_c, va_c, n_ctx)

    def stat(d, n, k):
        return st[d * N_STATS + k, pl.ds(n, 1), :]

    both = lambda x: jnp.concatenate([x, x], axis=-1)

    def make_body(group, n_chunks, stat0, q_src, kt_src, va_src, h_dst, want_out):
        def body(i, carry):
            items = []
            for u in range(group):
                nf = i * group + u
                for d, n in ((0, nf), (1, n_chunks - 1 - nf)):
                    items.append((d, stat0 + n, pl.multiple_of(n * L, L)))
            va = [va_src[pl.ds(t0, L), :] for _, _, t0 in items]
            c_loc, qk, q = [], [], []
            for (d, ns, t0), v in zip(items, va):
                kt = kt_src[:, pl.ds(t0, L)]
                c_loc.append(_dot((kt * stat(d, ns, ST_W)).astype(BF16), v))
                if want_out:
                    q.append(q_src[pl.ds(t0, L), :])
                    qk.append(_dot(q[-1], kt.astype(BF16)))
            lhs = []
            if want_out:
                for (d, ns, t0), qi, qki in zip(items, q, qk):
                    seen = (col <= row) if d == 0 else (col >= row)
                    big_r = rep[2 * d + REP_R, ns]
                    s = qki * jnp.exp(jnp.where(seen, stat(d, ns, ST_R) - big_r, NEG_BIG))
                    g = jnp.exp(stat(d, ns, ST_MPREV) - big_r)
                    lhs.append(jnp.concatenate([(g * qi.astype(F32)).astype(BF16), s.astype(BF16)], axis=1))
            state = list(carry)
            for idx, (d, ns, t0) in enumerate(items):
                if want_out:
                    acc = _dot(lhs[idx], jnp.concatenate([state[d].astype(BF16), va[idx]], axis=0))
                    floor = jnp.exp(-rep[2 * d + REP_M, ns])
                    h_dst[d][pl.ds(t0, L), :] = (acc[:, :HEAD_DIM]
                                                 / jnp.maximum(jnp.abs(acc[:, HEAD_DIM:]), floor))
                state[d] = both(stat(d, ns, ST_DEC)) * state[d] + both(stat(d, ns, ST_INP)) * c_loc[idx]
            return tuple(state)
        return body

    def finish(hsum, o_src, t0):
        y = _rms(hsum, on_ref[...]) * jax.nn.sigmoid(o_src[pl.ds(t0, L), :].astype(F32))
        return y.astype(BF16)

    zero_state = jnp.zeros((HEAD_DIM, ML_AUG), F32)
    carry = make_body(n_ctx, n_ctx, n_lat, qa_c, kt_c, va_c, (hf_c, hb_c), need_ctx)(0, (zero_state, zero_state))
    lax.fori_loop(0, n_lat // ML_GROUP, make_body(ML_GROUP, n_lat, 0, qa_l, kt_l, va_l, (hf_l, hb_l), True),
                  carry)

    def finish_lat(n, carry):
        t0 = pl.multiple_of(n * L, L)
        yl_ref[pl.ds(t0, L), :] = finish(hf_l[pl.ds(t0, L), :] + hb_l[pl.ds(t0, L), :], ol_ref, t0)
        return carry

    lax.fori_loop(0, n_lat, finish_lat, 0, unroll=4)
    if need_ctx:
        def finish_ctx(n, carry):
            t0 = pl.multiple_of(n * L, L)
            yc_ref[pl.ds(t0, L), :] = finish(hf_c[pl.ds(t0, L), :] + hb_c[pl.ds(t0, L), :], oc_ref, t0)
            return carry

        lax.fori_loop(0, n_ctx, finish_ctx, 0, unroll=True)


def _mlstm_call(p_lat, p_ctx, gates, gate_b, conv_w, conv_b, out_norm, batch, seq, ctx_len, need_ctx):
    n_lat, n_ctx = seq // ML_CHUNK, ctx_len // ML_CHUNK
    n_pad = gates.shape[2]
    cq, ck = COL_ML_Q // HEAD_DIM, COL_ML_K // HEAD_DIM
    cv, co = COL_ML_V // HEAD_DIM, COL_ML_O // HEAD_DIM
    col = lambda rows, c0: pl.BlockSpec((rows, HEAD_DIM), lambda b, h: (b, c0 + h))
    out_specs = [pl.BlockSpec((seq, HEAD_DIM), lambda b, h: (b, h))]
    out_shape = [jax.ShapeDtypeStruct((batch * seq, ML_W), BF16)]
    if need_ctx:
        out_specs.append(pl.BlockSpec((ctx_len, HEAD_DIM), lambda b, h: (b, h)))
        out_shape.append(jax.ShapeDtypeStruct((batch * ctx_len, ML_W), BF16))

    def stream_scratch(n_tok):
        return [pltpu.VMEM((n_tok, HEAD_DIM), BF16), pltpu.VMEM((HEAD_DIM, n_tok), F32),
                pltpu.VMEM((n_tok, ML_AUG), BF16)]

    res = pl.pallas_call(
        functools.partial(_mlstm_kernel, n_lat=n_lat, n_ctx=n_ctx, need_ctx=need_ctx),
        grid=(batch, ML_HEADS),
        in_specs=[col(seq, cq), col(seq, ck), col(seq, cv), col(seq, co),
                  col(ctx_len, cq), col(ctx_len, ck), col(ctx_len, cv), col(ctx_len, co),
                  pl.BlockSpec((N_GATES, None, n_pad, ML_CHUNK), lambda b, h: (0, b, 0, 0)),
                  pl.BlockSpec((N_GATES, 1, ML_CHUNK), lambda b, h: (0, 0, 0)),
                  pl.BlockSpec((3, HEAD_DIM), lambda b, h: (0, h)),
                  pl.BlockSpec((3, HEAD_DIM), lambda b, h: (0, ML_HEADS + h)),
                  pl.BlockSpec((1, HEAD_DIM), lambda b, h: (0, h)),
                  pl.BlockSpec((1, HEAD_DIM), lambda b, h: (0, ML_HEADS + h)),
                  pl.BlockSpec((1, HEAD_DIM), lambda b, h: (0, h))],
        out_specs=out_specs,
        out_shape=out_shape,
        scratch_shapes=(stream_scratch(seq) + stream_scratch(ctx_len)
                        + [pltpu.VMEM((seq, HEAD_DIM), F32), pltpu.VMEM((seq, HEAD_DIM), F32),
                           pltpu.VMEM((ctx_len, HEAD_DIM), F32), pltpu.VMEM((ctx_len, HEAD_DIM), F32),
                           pltpu.VMEM((2 * N_STATS, n_pad, ML_CHUNK), F32),
                           pltpu.VMEM((4, n_lat + n_ctx, ML_CHUNK, ML_CHUNK), F32)]),
        compiler_params=_cparams(("parallel", "parallel")),
        name="mlstm",
    )(p_lat, p_lat, p_lat, p_lat, p_ctx, p_ctx, p_ctx, p_ctx, gates, gate_b,
      conv_w, conv_w, conv_b, conv_b, out_norm)
    return (res[0], res[1]) if need_ctx else (res[0], None)


def kernel(x, c, ctx, c_ctx, w_ada, b_ada, norm_ff1, ff1_gate, ff1_up, ff1_down, norm_mix, w_in, na_rpb,
           gqa_q_norm, gqa_k_norm, ml_conv_w, ml_conv_b, ml_gate_b, ml_out_norm, w_out, norm_ff2,
           ff2_gate, ff2_up, ff2_down, final_norm):
    batch, seq, d = x.shape
    ctx_len = ctx.shape[1]
    depth = w_ada.shape[0]
    ctx_group = batch
    n_lat_chunks, n_ctx_chunks = seq // ML_CHUNK, ctx_len // ML_CHUNK
    n_chunk_pad = -(-(n_lat_chunks + n_ctx_chunks) // 8) * 8

    xl = x.reshape(batch * seq, d)
    xc = ctx.reshape(batch * ctx_len, d)
    cond = jnp.zeros((8, d), F32).at[:batch].set(c).at[batch].set(c_ctx)
    rope_tabs = _rope_tables(seq)
    w_in_t = jnp.swapaxes(w_in, 1, 2)
    row2 = lambda v: v.reshape(1, -1)
    fin = row2(final_norm)

    for l in range(depth):
        last = l == depth - 1
        mod = _ada_call(cond, w_ada, l, row2(b_ada[l])).reshape(8 * N_MOD, 1, d)
        lat = dict(group_rows=seq, fixed_group=None)
        cx = dict(group_rows=None, fixed_group=ctx_group)

        cast = lambda w: _cast_call(w, l)
        wg, wu, wd = cast(ff1_gate), cast(ff1_up), cast(ff1_down)
        g1 = row2(norm_ff1[l])
        xl = _ffn_call(xl, mod, 0, g1, wg, wu, wd, fin, final_norm=False, **lat)
        xc = _ffn_call(xc, mod, 0, g1, wg, wu, wd, fin, final_norm=False, **cx)

        w_main_t = _cast_call(w_in_t, l, COL_GATES)
        w_gates_t = w_in_t[l, COL_GATES:]
        gm = row2(norm_mix[l])
        qg, kg = row2(gqa_q_norm[l]), row2(gqa_k_norm[l])
        p_lat, gt_lat = _inproj_call(xl, mod, gm, w_main_t, w_gates_t, qg, kg, rope_tabs, seq, rope=True, **lat)
        p_ctx, gt_ctx = _inproj_call(xc, mod, gm, w_main_t, w_gates_t, qg, kg, rope_tabs, seq, rope=False, **cx)

        ya_lat = _na_call(p_lat, p_ctx, *_na_toeplitz_slabs(na_rpb[l]), batch, seq, ctx_len)
        yb_lat = _gqa_call(p_lat, p_ctx, batch, seq, ctx_len)
        gates = jnp.concatenate(
            [gt_lat.reshape(N_GATES, batch, n_lat_chunks, ML_CHUNK),
             gt_ctx.reshape(N_GATES, batch, n_ctx_chunks, ML_CHUNK),
             jnp.zeros((N_GATES, batch, n_chunk_pad - n_lat_chunks - n_ctx_chunks, ML_CHUNK), F32)], axis=2)
        gate_b = jnp.broadcast_to(ml_gate_b[l][:, None, None], (N_GATES, 1, ML_CHUNK))
        yc_lat, yc_ctx = _mlstm_call(p_lat, p_ctx, gates, gate_b, ml_conv_w[l], row2(ml_conv_b[l]),
                                     row2(ml_out_norm[l]), batch, seq, ctx_len, not last)

        wo = cast(w_out)
        xl = _outproj_call(xl, mod, ya_lat, yb_lat, yc_lat, wo, **lat)
        wg, wu, wd = cast(ff2_gate), cast(ff2_up), cast(ff2_down)
        g2 = row2(norm_ff2[l])
        xl = _ffn_call(xl, mod, 6, g2, wg, wu, wd, fin, final_norm=last, **lat)
        if not last:
            ya_ctx = _ctx_attn_call(p_ctx, p_ctx, p_ctx, batch, ctx_len, NA_HEADS,
                                    COL_NA_Q // HEAD_DIM, COL_NA_K // HEAD_DIM, COL_NA_V // HEAD_DIM, 1,
                                    HEAD_DIM ** -0.5)
            yb_ctx = _ctx_attn_call(p_ctx, p_ctx, p_ctx, batch, ctx_len, GQA_Q_HEADS,
                                    COL_GQ_Q // HEAD_DIM, COL_GQ_K // HEAD_DIM, COL_GQ_V // HEAD_DIM,
                                    GQA_Q_HEADS // GQA_KV_HEADS, None)
            xc = _outproj_call(xc, mod, ya_ctx, yb_ctx, yc_ctx, wo, **cx)
            xc = _ffn_call(xc, mod, 6, g2, wg, wu, wd, fin, final_norm=False, **cx)
    return xl.reshape(batch, seq, d)
```

```python
import functools

import numpy as np
import jax
import jax.numpy as jnp
from jax import lax
from jax.experimental import pallas as pl
from jax.experimental.pallas import tpu as pltpu

F32 = jnp.float32
BF16 = jnp.bfloat16

GRID_W = 64
HEAD_DIM = 128
NA_HEADS = 4
GQA_Q_HEADS = 4
GQA_KV_HEADS = 2
ML_HEADS = 8
NA_W = NA_HEADS * HEAD_DIM
GQA_W = GQA_Q_HEADS * HEAD_DIM
KV_W = GQA_KV_HEADS * HEAD_DIM
ML_W = ML_HEADS * HEAD_DIM
NA_WIN_R = 8
NA_WIN_C = 16
ML_CHUNK = 128
ROPE_THETA = 10000.0
N_MOD = 9
EPS = 1e-6
N_GATES = 4 * ML_HEADS
COL_NA_Q = 0
COL_NA_K = NA_W
COL_NA_V = 2 * NA_W
COL_GQ_Q = 3 * NA_W
COL_GQ_K = COL_GQ_Q + GQA_W
COL_GQ_V = COL_GQ_K + KV_W
COL_ML_Q = COL_GQ_V + KV_W
COL_ML_K = COL_ML_Q + ML_W
COL_ML_V = COL_ML_K + ML_W
COL_ML_O = COL_ML_V + ML_W
COL_GATES = COL_ML_O + ML_W

NEG_BIG = -1e30
VMEM_LIMIT_V7X = 56 * 1024 * 1024
BF16_ROWS = 16
NORM_ROWS = 32
EPILOGUE_ROWS = 128
FFN_SUBTILE = 256
FFN_OUT_SUBTILE = 1024

NA_QROWS = 8
NA_KROWS = 16
NA_KEY_TILE = 256
NA_MASKED_SLAB = 2 * NA_WIN_R - 1


def _cparams(sem):
    return pltpu.CompilerParams(dimension_semantics=sem, vmem_limit_bytes=VMEM_LIMIT_V7X)


def _rms(x, gain):
    return x * lax.rsqrt(jnp.mean(x * x, axis=-1, keepdims=True) + EPS) * gain


def _norm_modulate_store(x_ref, gain_ref, scale_ref, shift_ref, h_ref, zero_ref):
    gain_scale = gain_ref[...] * (1.0 + scale_ref[...])
    shift = shift_ref[...]

    def body(i, carry):
        r0 = pl.multiple_of(i * NORM_ROWS, NORM_ROWS)
        x = x_ref[pl.ds(r0, NORM_ROWS), :]
        inv = lax.rsqrt(jnp.mean(x * x, axis=-1, keepdims=True) + EPS)
        h_ref[pl.ds(r0, NORM_ROWS), :] = (x * inv * gain_scale + shift).astype(h_ref.dtype)
        zero_ref[pl.ds(r0, NORM_ROWS), :] = jnp.zeros((NORM_ROWS, zero_ref.shape[1]), zero_ref.dtype)
        return carry

    lax.fori_loop(0, x_ref.shape[0] // NORM_ROWS, body, 0, unroll=4)


def _dot(a, b):
    return jnp.dot(a, b, preferred_element_type=F32)


def _dot_nt(a, b):
    return lax.dot_general(a, b, (((1,), (1,)), ((), ())), preferred_element_type=F32)


def _cast_kernel(w_ref, o_ref):
    o_ref[...] = w_ref[...].astype(o_ref.dtype)


def _cast_call(w_stack, layer, n_rows=None, row_blocks=8):
    _, rows, n_cols = w_stack.shape
    n_rows = rows if n_rows is None else n_rows
    tr = n_rows // row_blocks
    return pl.pallas_call(
        _cast_kernel,
        grid=(row_blocks,),
        in_specs=[pl.BlockSpec((None, tr, n_cols), lambda i: (layer, i, 0))],
        out_specs=pl.BlockSpec((tr, n_cols), lambda i: (i, 0)),
        out_shape=jax.ShapeDtypeStruct((n_rows, n_cols), BF16),
        compiler_params=_cparams(("parallel",)),
        name="weight_cast",
    )(w_stack)


def _ada_kernel(c_ref, w_ref, b_ref, o_ref):
    c = c_ref[...]
    a = (c * jax.nn.sigmoid(c)).astype(BF16)
    o_ref[...] = _dot(a, w_ref[...].astype(BF16)) + b_ref[...]


def _ada_call(cond, w_stack, layer, b, tn=1024):
    m, d = cond.shape
    n = w_stack.shape[2]
    return pl.pallas_call(
        _ada_kernel,
        grid=(n // tn,),
        in_specs=[pl.BlockSpec((m, d), lambda j: (0, 0)),
                  pl.BlockSpec((None, d, tn), lambda j: (layer, 0, j)),
                  pl.BlockSpec((1, tn), lambda j: (0, j))],
        out_specs=pl.BlockSpec((m, tn), lambda j: (0, j)),
        out_shape=jax.ShapeDtypeStruct((m, n), F32),
        compiler_params=_cparams(("arbitrary",)),
        name="ada_mod",
    )(cond, w_stack, b)


def _mod_spec(k, tm, group_rows, fixed_group):
    def index_map(i, *_):
        grp = fixed_group if fixed_group is not None else i // (group_rows // tm)
        return (grp * N_MOD + k, 0, 0)
    return index_map


def _ffn_kernel(x_ref, sh_ref, sc_ref, gt_ref, gain_ref, wg_ref, wu_ref, wd_ref, fin_ref, o_ref,
                h_scr, *, n_ff, final_norm):
    j = pl.program_id(1)

    @pl.when(j == 0)
    def _():
        _norm_modulate_store(x_ref, gain_ref, sc_ref, sh_ref, h_scr, o_ref)

    h = h_scr[...]
    tf = wg_ref.shape[1]
    for c0 in range(0, tf, FFN_SUBTILE):
        g = _dot(h, wg_ref[:, c0:c0 + FFN_SUBTILE])
        u = _dot(h, wu_ref[:, c0:c0 + FFN_SUBTILE])
        a = ((g * jax.nn.sigmoid(g)) * u).astype(BF16)
        for n0 in range(0, o_ref.shape[1], FFN_OUT_SUBTILE):
            o_ref[:, n0:n0 + FFN_OUT_SUBTILE] += _dot(a, wd_ref[c0:c0 + FFN_SUBTILE, n0:n0 + FFN_OUT_SUBTILE])

    @pl.when(j == n_ff - 1)
    def _():
        half_gate = 0.5 * gt_ref[...]

        step_rows = EPILOGUE_ROWS if final_norm else NORM_ROWS

        def body(i, carry):
            rows = [pl.multiple_of(i * step_rows + k, NORM_ROWS) for k in range(0, step_rows, NORM_ROWS)]
            outs = [x_ref[pl.ds(r, NORM_ROWS), :] + half_gate * o_ref[pl.ds(r, NORM_ROWS), :] for r in rows]
            if final_norm:
                outs = [_rms(out, fin_ref[...]) for out in outs]
            for r, out in zip(rows, outs):
                o_ref[pl.ds(r, NORM_ROWS), :] = out
            return carry

        lax.fori_loop(0, o_ref.shape[0] // step_rows, body, 0, unroll=1 if final_norm else 4)


def _ffn_call(x, mod, k0, gain, wg, wu, wd, fin, *, group_rows, fixed_group, final_norm,
              tm=1024, tf=512):
    t, d = x.shape
    dff = wg.shape[1]
    n_ff = dff // tf
    mspec = lambda k: pl.BlockSpec((None, 1, d), _mod_spec(k, tm, group_rows, fixed_group))
    return pl.pallas_call(
        functools.partial(_ffn_kernel, n_ff=n_ff, final_norm=final_norm),
        grid=(t // tm, n_ff),
        in_specs=[pl.BlockSpec((tm, d), lambda i, j: (i, 0)),
                  mspec(k0), mspec(k0 + 1), mspec(k0 + 2),
                  pl.BlockSpec((1, d), lambda i, j: (0, 0)),
                  pl.BlockSpec((d, tf), lambda i, j: (0, j)),
                  pl.BlockSpec((d, tf), lambda i, j: (0, j)),
                  pl.BlockSpec((tf, d), lambda i, j: (j, 0)),
                  pl.BlockSpec((1, d), lambda i, j: (0, 0))],
        out_specs=pl.BlockSpec((tm, d), lambda i, j: (i, 0)),
        out_shape=jax.ShapeDtypeStruct((t, d), F32),
        scratch_shapes=[pltpu.VMEM((tm, d), BF16)],
        compiler_params=_cparams(("parallel", "arbitrary")),
        name="macaron_ffn",
    )(x, mod, mod, mod, gain, wg, wu, wd, fin)


def _rope(x, cos, sin_lo, sin_hi):
    return (x * cos + pltpu.roll(x, HEAD_DIM - HEAD_DIM // 4, 1) * sin_lo
            + pltpu.roll(x, HEAD_DIM // 4, 1) * sin_hi)


def _inproj_kernel(x_ref, sh_ref, sc_ref, gain_ref, wt_ref, wgt_ref, qg_ref, kg_ref, cos_ref, slo_ref, shi_ref,
                   p_ref, gt_ref, h_scr, *, rope):
    j = pl.program_id(1)

    def gqa_head(x, gain, scale):
        y = _rms(x, gain)
        if rope:
            y = _rope(y, cos_ref[...], slo_ref[...], shi_ref[...])
        return y if scale is None else y * scale

    @pl.when(j == 0)
    def _():
        h = _rms(x_ref[...], gain_ref[...]) * (1.0 + sc_ref[...]) + sh_ref[...]
        hb = h.astype(BF16)
        h_scr[...] = hb
        gt_ref[...] = _dot_nt(wgt_ref[...].astype(BF16), hb)
        acc = _dot_nt(hb, wt_ref[...])
        p_ref[:, :COL_GQ_Q] = acc[:, :COL_GQ_Q].astype(p_ref.dtype)
        for c0, n_heads, gain_ref_h, scale in ((COL_GQ_Q, GQA_Q_HEADS, qg_ref, HEAD_DIM ** -0.5),
                                               (COL_GQ_K, GQA_KV_HEADS, kg_ref, None)):
            for hd in range(n_heads):
                sl = slice(c0 + hd * HEAD_DIM, c0 + (hd + 1) * HEAD_DIM)
                p_ref[:, sl] = gqa_head(acc[:, sl], gain_ref_h[...], scale).astype(p_ref.dtype)
        p_ref[:, COL_GQ_V:] = acc[:, COL_GQ_V:].astype(p_ref.dtype)

    @pl.when(j != 0)
    def _():
        p_ref[...] = _dot_nt(h_scr[...], wt_ref[...]).astype(p_ref.dtype)


def _inproj_call(x, mod, gain, w_main_t, w_gates_t, q_gain, k_gain, rope_tables, seq, *, rope, group_rows,
                 fixed_group, tm=512, tn=3328):
    t, d = x.shape
    n = w_main_t.shape[0]
    assert tn >= COL_GQ_V, "mixer B's q / k columns must sit in the first column tile"
    mspec = lambda k: pl.BlockSpec((None, 1, d), _mod_spec(k, tm, group_rows, fixed_group))
    per_seq = seq // tm if rope else 1
    tab_spec = pl.BlockSpec((tm, HEAD_DIM), lambda i, j: (i % per_seq, 0))
    head_row = pl.BlockSpec((1, HEAD_DIM), lambda i, j: (0, 0))
    return pl.pallas_call(
        functools.partial(_inproj_kernel, rope=rope),
        grid=(t // tm, n // tn),
        in_specs=[pl.BlockSpec((tm, d), lambda i, j: (i, 0)),
                  mspec(3), mspec(4),
                  pl.BlockSpec((1, d), lambda i, j: (0, 0)),
                  pl.BlockSpec((tn, d), lambda i, j: (j, 0)),
                  pl.BlockSpec((N_GATES, d), lambda i, j: (0, 0)),
                  head_row, head_row, tab_spec, tab_spec, tab_spec],
        out_specs=[pl.BlockSpec((tm, tn), lambda i, j: (i, j)),
                   pl.BlockSpec((N_GATES, tm), lambda i, j: (0, i))],
        out_shape=[jax.ShapeDtypeStruct((t, n), BF16),
                   jax.ShapeDtypeStruct((N_GATES, t), F32)],
        scratch_shapes=[pltpu.VMEM((tm, d), BF16)],
        compiler_params=_cparams(("parallel", "arbitrary")),
        name="mixer_in_proj",
    )(x, mod, mod, gain, w_main_t, w_gates_t, q_gain, k_gain, *rope_tables)


def _outproj_kernel(x_ref, gt_ref, ya_ref, yb_ref, yc_ref, w_ref, o_ref):
    acc = _dot(ya_ref[...], w_ref[0:NA_W, :])
    acc += _dot(yb_ref[...], w_ref[NA_W:NA_W + GQA_W, :])
    acc += _dot(yc_ref[...], w_ref[NA_W + GQA_W:, :])
    o_ref[...] = x_ref[...] + gt_ref[...] * acc


def _outproj_call(x, mod, ya, yb, yc, w, *, group_rows, fixed_group, tm=512):
    t, d = x.shape
    return pl.pallas_call(
        _outproj_kernel,
        grid=(t // tm,),
        in_specs=[pl.BlockSpec((tm, d), lambda i: (i, 0)),
                  pl.BlockSpec((None, 1, d), _mod_spec(5, tm, group_rows, fixed_group)),
                  pl.BlockSpec((tm, NA_W), lambda i: (i, 0)),
                  pl.BlockSpec((tm, GQA_W), lambda i: (i, 0)),
                  pl.BlockSpec((tm, ML_W), lambda i: (i, 0)),
                  pl.BlockSpec((d, d), lambda i: (0, 0))],
        out_specs=pl.BlockSpec((tm, d), lambda i: (i, 0)),
        out_shape=jax.ShapeDtypeStruct((t, d), F32),
        compiler_params=_cparams(("parallel",)),
        name="mixer_out_proj",
    )(x, mod, ya, yb, yc, w)


def _na_kernel(q_ref, k_ref, v_ref, kc_ref, vc_ref, tl_ref, tr_ref, o_ref, mb_scr, *, rows):
    rb = pl.program_id(2)
    nblk = rows // NA_QROWS
    key_row0 = jnp.clip(rb * NA_QROWS - NA_WIN_R // 2, 0, rows - NA_KROWS)

    @pl.when((rb <= 1) | (rb == nblk - 1))
    def _():
        for a in range(NA_QROWS):
            r = rb * NA_QROWS + a
            row_start = jnp.clip(r - NA_WIN_R // 2, 0, rows - NA_WIN_R)
            for j in range(NA_KROWS // 2):
                sel = []
                for kr in (key_row0 + 2 * j, key_row0 + 2 * j + 1):
                    in_win = (kr >= row_start) & (kr < row_start + NA_WIN_R)
                    sel.append(jnp.where(in_win, kr - r + (NA_WIN_R - 1), NA_MASKED_SLAB))
                mb_scr[a * GRID_W:(a + 1) * GRID_W, 2 * j * GRID_W:(2 * j + 2) * GRID_W] = (
                    tl_ref[sel[0]] + tr_ref[sel[1]])

    start = pl.multiple_of(key_row0 * GRID_W, GRID_W)
    q = (q_ref[...].astype(F32) * (HEAD_DIM ** -0.5)).astype(BF16)
    n_keys = NA_KROWS * GRID_W
    key_tile = lambda t0: k_ref[pl.ds(start + t0, NA_KEY_TILE), :]
    s = _dot_nt(q, kc_ref[...])
    s_next = _dot_nt(q, key_tile(0))
    m = jnp.max(s, axis=-1, keepdims=True)
    p = jnp.exp(s - m)
    den = jnp.sum(p, axis=-1, keepdims=True)
    acc = _dot(p.astype(BF16), vc_ref[...])
    for t0 in range(0, n_keys, NA_KEY_TILE):
        s = s_next + mb_scr[:, t0:t0 + NA_KEY_TILE]
        if t0 + NA_KEY_TILE < n_keys:
            s_next = _dot_nt(q, key_tile(t0 + NA_KEY_TILE))
        m_new = jnp.maximum(m, jnp.max(s, axis=-1, keepdims=True))
        p = jnp.exp(s - m_new)
        alpha = jnp.exp(m - m_new)
        den = alpha * den + jnp.sum(p, axis=-1, keepdims=True)
        acc = alpha * acc + _dot(p.astype(BF16), v_ref[pl.ds(start + t0, NA_KEY_TILE), :])
        m = m_new
    o_ref[...] = (acc / den).astype(o_ref.dtype)


def _na_toeplitz_slabs(rpb):
    n_h, n_dr, n_dc = rpb.shape
    edge = GRID_W - NA_WIN_C
    ext = jnp.concatenate(
        [rpb[..., NA_WIN_C - 1:], jnp.broadcast_to(rpb[..., n_dc - 1:], (n_h, n_dr, edge)),
         jnp.zeros((n_h, n_dr, 1), F32),
         jnp.broadcast_to(rpb[..., :1], (n_h, n_dr, edge)), rpb[..., :NA_WIN_C - 1]], axis=-1)
    width = 2 * GRID_W
    tiled = jnp.broadcast_to(ext[:, :, None, :], (n_h, n_dr, GRID_W, width)).reshape(n_h, n_dr, GRID_W * width)
    toep = tiled[..., :GRID_W * (width - 1)].reshape(n_h, n_dr, GRID_W, width - 1)[..., :GRID_W]
    cq = np.arange(GRID_W)
    col_start = np.clip(cq - NA_WIN_C // 2, 0, GRID_W - NA_WIN_C)
    col_ok = (cq[None, :] >= col_start[:, None]) & (cq[None, :] < col_start[:, None] + NA_WIN_C)
    slabs = jnp.where(col_ok, toep, NEG_BIG)
    slabs = jnp.concatenate([slabs, jnp.full((n_h, 1, GRID_W, GRID_W), NEG_BIG, F32)], axis=1)
    zeros = jnp.zeros_like(slabs)
    return jnp.concatenate([slabs, zeros], axis=-1), jnp.concatenate([zeros, slabs], axis=-1)


def _na_call(p_lat, p_ctx, slabs_left, slabs_right, batch, seq, ctx_len):
    rows = seq // GRID_W
    nblk = rows // NA_QROWS
    tq = NA_QROWS * GRID_W
    nk = NA_KROWS * GRID_W
    cq, ck, cv = COL_NA_Q // HEAD_DIM, COL_NA_K // HEAD_DIM, COL_NA_V // HEAD_DIM
    slab_spec = pl.BlockSpec((None, NA_MASKED_SLAB + 1, GRID_W, 2 * GRID_W), lambda b, h, r: (h, 0, 0, 0))

    return pl.pallas_call(
        functools.partial(_na_kernel, rows=rows),
        grid=(batch, NA_HEADS, nblk),
        in_specs=[pl.BlockSpec((tq, HEAD_DIM), lambda b, h, r: (b * nblk + r, cq + h)),
                  pl.BlockSpec((seq, HEAD_DIM), lambda b, h, r: (b, ck + h)),
                  pl.BlockSpec((seq, HEAD_DIM), lambda b, h, r: (b, cv + h)),
                  pl.BlockSpec((ctx_len, HEAD_DIM), lambda b, h, r: (b, ck + h)),
                  pl.BlockSpec((ctx_len, HEAD_DIM), lambda b, h, r: (b, cv + h)),
                  slab_spec, slab_spec],
        out_specs=pl.BlockSpec((tq, HEAD_DIM), lambda b, h, r: (b * nblk + r, h)),
        out_shape=jax.ShapeDtypeStruct((batch * seq, NA_W), BF16),
        scratch_shapes=[pltpu.VMEM((tq, nk), F32)],
        compiler_params=_cparams(("parallel", "parallel", "arbitrary")),
        name="neighbourhood_attn",
    )(p_lat, p_lat, p_lat, p_ctx, p_ctx, slabs_left, slabs_right)


def _ctx_attn_kernel(q_ref, k_ref, v_ref, o_ref, *, scale):
    s = _dot_nt(q_ref[...], k_ref[...])
    if scale is not None:
        s = s * scale
    m = jnp.max(s, axis=-1, keepdims=True)
    p = jnp.exp(s - m)
    den = jnp.sum(p, axis=-1, keepdims=True)
    o_ref[...] = (_dot(p.astype(BF16), v_ref[...]) / den).astype(o_ref.dtype)


def _ctx_attn_call(q_arr, k_arr, v_arr, batch, ctx_len, n_heads, q_col, k_col, v_col, kv_group, scale):
    return pl.pallas_call(
        functools.partial(_ctx_attn_kernel, scale=scale),
        grid=(batch, n_heads),
        in_specs=[pl.BlockSpec((ctx_len, HEAD_DIM), lambda b, h: (b, q_col + h)),
                  pl.BlockSpec((ctx_len, HEAD_DIM), lambda b, h: (b, k_col + h // kv_group)),
                  pl.BlockSpec((ctx_len, HEAD_DIM), lambda b, h: (b, v_col + h // kv_group))],
        out_specs=pl.BlockSpec((ctx_len, HEAD_DIM), lambda b, h: (b, h)),
        out_shape=jax.ShapeDtypeStruct((batch * ctx_len, n_heads * HEAD_DIM), BF16),
        compiler_params=_cparams(("parallel", "parallel")),
        name="ctx_attn",
    )(q_arr, k_arr, v_arr)


def _rope_tables(n_tokens):
    t = jnp.arange(n_tokens)
    row = (t // GRID_W).astype(F32)
    col = (t % GRID_W).astype(F32)
    half = HEAD_DIM // 2
    inv_freq = 1.0 / (ROPE_THETA ** (jnp.arange(0, half, 2, dtype=F32) / half))
    ang_r = row[:, None] * inv_freq[None, :]
    ang_c = col[:, None] * inv_freq[None, :]
    ang = jnp.concatenate([ang_r, ang_r, ang_c, ang_c], axis=-1)
    cos, sin = jnp.cos(ang), jnp.sin(ang)
    lo = (jnp.arange(HEAD_DIM) % half) < (half // 2)
    return cos, jnp.where(lo, -sin, 0.0), jnp.where(lo, 0.0, sin)


def _gqa_kernel(q_ref, kl_ref, vl_ref, kc_ref, vc_ref, o_ref, *, tk):
    tq = q_ref.shape[0]
    grp = q_ref.shape[1] // HEAD_DIM
    q = jnp.concatenate([q_ref[:, g * HEAD_DIM:(g + 1) * HEAD_DIM] for g in range(grp)], axis=0)

    n_tiles = kl_ref.shape[0] // tk
    s = _dot_nt(q, kc_ref[...])
    s_next = _dot_nt(q, kl_ref[0:tk, :])
    m = jnp.max(s, axis=-1, keepdims=True)
    p = jnp.exp(s - m)
    den = jnp.sum(p, axis=-1, keepdims=True)
    acc = _dot(p.astype(BF16), vc_ref[...])
    for t in range(n_tiles):
        s = s_next
        if t + 1 < n_tiles:
            s_next = _dot_nt(q, kl_ref[(t + 1) * tk:(t + 2) * tk, :])
        m_new = jnp.maximum(m, jnp.max(s, axis=-1, keepdims=True))
        p = jnp.exp(s - m_new)
        alpha = jnp.exp(m - m_new)
        den = alpha * den + jnp.sum(p, axis=-1, keepdims=True)
        acc = alpha * acc + _dot(p.astype(BF16), vl_ref[t * tk:(t + 1) * tk, :])
        m = m_new
    out = (acc / den).astype(o_ref.dtype)
    for g in range(grp):
        o_ref[:, g * HEAD_DIM:(g + 1) * HEAD_DIM] = out[g * tq:(g + 1) * tq]


def _gqa_call(p_lat, p_ctx, batch, seq, ctx_len, tq=512, tk=2048):
    grp = GQA_Q_HEADS // GQA_KV_HEADS
    nq = seq // tq
    cq = COL_GQ_Q // (grp * HEAD_DIM)
    ck, cv = COL_GQ_K // HEAD_DIM, COL_GQ_V // HEAD_DIM
    return pl.pallas_call(
        functools.partial(_gqa_kernel, tk=tk),
        grid=(batch, GQA_KV_HEADS, nq),
        in_specs=[pl.BlockSpec((tq, grp * HEAD_DIM), lambda b, h, i: (b * nq + i, cq + h)),
                  pl.BlockSpec((seq, HEAD_DIM), lambda b, h, i: (b, ck + h)),
                  pl.BlockSpec((seq, HEAD_DIM), lambda b, h, i: (b, cv + h)),
                  pl.BlockSpec((ctx_len, HEAD_DIM), lambda b, h, i: (b, ck + h)),
                  pl.BlockSpec((ctx_len, HEAD_DIM), lambda b, h, i: (b, cv + h))],
        out_specs=pl.BlockSpec((tq, grp * HEAD_DIM), lambda b, h, i: (b * nq + i, h)),
        out_shape=jax.ShapeDtypeStruct((batch * seq, GQA_W), BF16),
        compiler_params=_cparams(("parallel", "parallel", "arbitrary")),
        name="gqa_attn",
    )(p_lat, p_lat, p_lat, p_ctx, p_ctx)


ST_R, ST_W, ST_BTOT, ST_AMAX, ST_MPREV, ST_DEC, ST_INP, ST_BIGR, ST_M = range(9)
N_STATS = 9
REP_R, REP_M = range(2)
ML_AUG = 2 * HEAD_DIM
ML_GROUP = 8


def _mlstm_kernel(ql_ref, kl_ref, vl_ref, ol_ref, qc_ref, kc_ref, vc_ref, oc_ref, g_ref, gb_ref,
                  cwq_ref, cwk_ref, cbq_ref, cbk_ref, on_ref, yl_ref, *rest,
                  n_lat, n_ctx, need_ctx):
    if need_ctx:
        yc_ref = rest[0]
        rest = rest[1:]
    qa_l, kt_l, va_l, qa_c, kt_c, va_c, hf_l, hb_l, hf_c, hb_c, st, rep = rest
    L = ML_CHUNK
    n_all = n_lat + n_ctx
    h = pl.program_id(1)
    row = lax.broadcasted_iota(jnp.int32, (L, L), 0)
    col = lax.broadcasted_iota(jnp.int32, (L, L), 1)

    tri_fw = (row <= col).astype(F32)
    tri_bw = (row >= col).astype(F32)
    b_cum = []
    for d, tri in enumerate((tri_fw, tri_bw)):
        log_i = g_ref[2 * d * ML_HEADS + h] + gb_ref[2 * d * ML_HEADS + h]
        log_f = jax.nn.log_sigmoid(g_ref[(2 * d + 1) * ML_HEADS + h] + gb_ref[(2 * d + 1) * ML_HEADS + h])
        b = jnp.dot(log_f, tri, preferred_element_type=F32, precision=lax.Precision.HIGHEST)
        b_tot = b[:, L - 1:L] if d == 0 else b[:, 0:1]
        a = b_tot - b + log_i
        a_max = jnp.max(a, axis=-1, keepdims=True)
        r = log_i - b
        lane = lax.broadcasted_iota(jnp.int32, r.shape, 1)
        run = r
        shift = 1
        while shift < L:
            if d == 0:
                run = jnp.maximum(run, jnp.where(lane >= shift, pltpu.roll(run, shift, 1), NEG_BIG))
            else:
                run = jnp.maximum(run, jnp.where(lane < L - shift, pltpu.roll(run, L - shift, 1), NEG_BIG))
            shift *= 2
        base = d * N_STATS
        st[base + ST_R] = r
        st[base + ST_W] = jnp.exp(a - a_max)
        st[base + ST_BTOT] = jnp.broadcast_to(b_tot, b.shape)
        st[base + ST_AMAX] = jnp.broadcast_to(a_max, b.shape)
        st[base + ST_BIGR] = run
        st[base + ST_MPREV] = jnp.zeros_like(b)
        b_cum.append(b)

    def stabiliser_step(i, carry):
        new = []
        for d, m_prev in enumerate(carry):
            in_ctx = i < n_ctx
            j = jnp.where(in_ctx, i, i - n_ctx)
            if d == 0:
                n = jnp.where(in_ctx, n_lat + j, j)
            else:
                n = jnp.where(in_ctx, n_lat + n_ctx - 1 - j, n_lat - 1 - j)
            base = d * N_STATS
            b_tot = st[base + ST_BTOT, pl.ds(n, 1), :]
            a_max = st[base + ST_AMAX, pl.ds(n, 1), :]
            m_new = jnp.maximum(b_tot + m_prev, a_max)
            st[base + ST_MPREV, pl.ds(n, 1), :] = m_prev
            st[base + ST_DEC, pl.ds(n, 1), :] = jnp.exp(b_tot + m_prev - m_new)
            st[base + ST_INP, pl.ds(n, 1), :] = jnp.exp(a_max - m_new)
            new.append(m_new)
        return tuple(new)

    lax.fori_loop(0, n_all, stabiliser_step, (jnp.zeros((1, L), F32), jnp.zeros((1, L), F32)))

    for d in range(2):
        base = d * N_STATS
        big_r = jnp.maximum(st[base + ST_BIGR], st[base + ST_MPREV])
        st[base + ST_BIGR] = big_r
        st[base + ST_M] = b_cum[d] + big_r

    def column_tables(n):
        for d in range(2):
            for k_rep, k_st in ((REP_R, ST_BIGR), (REP_M, ST_M)):
                rows = jnp.broadcast_to(st[d * N_STATS + k_st, pl.ds(n, 1), :], (L, L))
                rep[2 * d + k_rep, n] = rows.T

    def prepare_stream(q_src, k_src, v_src, q_dst, kt_dst, va_dst, n_chunks, stat0):
        n_tok = n_chunks * L
        sub = lax.broadcasted_iota(jnp.int32, (L, HEAD_DIM), 0)

        def conv_silu(src_ref, w_ref, b_ref, n, t0):
            x = src_ref[pl.ds(t0, L), :].astype(F32)
            prev_t0 = pl.multiple_of(jnp.maximum(t0 - BF16_ROWS, 0), BF16_ROWS)
            next_t0 = pl.multiple_of(jnp.minimum(t0 + L, n_tok - BF16_ROWS), BF16_ROWS)
            prev_row = (src_ref[pl.ds(prev_t0, BF16_ROWS), :].astype(F32)[BF16_ROWS - 1:, :]
                        * jnp.where(n > 0, 1.0, 0.0))
            next_row = (src_ref[pl.ds(next_t0, BF16_ROWS), :].astype(F32)[:1, :]
                        * jnp.where(n < n_chunks - 1, 1.0, 0.0))
            x_prev = jnp.where(sub == 0, prev_row, pltpu.roll(x, 1, 0))
            x_next = jnp.where(sub == L - 1, next_row, pltpu.roll(x, L - 1, 0))
            y = w_ref[0:1, :] * x_prev + w_ref[1:2, :] * x + w_ref[2:3, :] * x_next + b_ref[...]
            return y * jax.nn.sigmoid(y)

        def body(n, carry):
            t0 = pl.multiple_of(n * L, L)
            q_dst[pl.ds(t0, L), :] = conv_silu(q_src, cwq_ref, cbq_ref, n, t0).astype(BF16)
            k_act = conv_silu(k_src, cwk_ref, cbk_ref, n, t0) * (HEAD_DIM ** -0.5)
            kt_dst[:, pl.ds(t0, L)] = k_act.T
            va_dst[pl.ds(t0, L), 0:HEAD_DIM] = v_src[pl.ds(t0, L), :]
            va_dst[pl.ds(t0, L), HEAD_DIM:ML_AUG] = jnp.ones((L, HEAD_DIM), BF16)
            column_tables(stat0 + n)
            return carry

        lax.fori_loop(0, n_chunks, body, 0, unroll=2)

    prepare_stream(ql_ref, kl_ref, vl_ref, qa_l, kt_l, va_l, n_lat, 0)
    prepare_stream(qc_ref, kc_ref, vc_ref, qa_c, kt_c, va_c, n_ctx, n_lat)

    def stat(d, n, k):
        return st[d * N_STATS + k, pl.ds(n, 1), :]

    both = lambda x: jnp.concatenate([x, x], axis=-1)

    def make_body(group, n_chunks, stat0, q_src, kt_src, va_src, h_dst, want_out):
        def body(i, carry):
            items = []
            for u in range(group):
                nf = i * group + u
                for d, n in ((0, nf), (1, n_chunks - 1 - nf)):
                    items.append((d, stat0 + n, pl.multiple_of(n * L, L)))
            va = [va_src[pl.ds(t0, L), :] for _, _, t0 in items]
            c_loc, qk, q = [], [], []
            for (d, ns, t0), v in zip(items, va):
                kt = kt_src[:, pl.ds(t0, L)]
                c_loc.append(_dot((kt * stat(d, ns, ST_W)).astype(BF16), v))
                if want_out:
                    q.append(q_src[pl.ds(t0, L), :])
                    qk.append(_dot(q[-1], kt.astype(BF16)))
            lhs = []
            if want_out:
                for (d, ns, t0), qi, qki in zip(items, q, qk):
                    seen = (col <= row) if d == 0 else (col >= row)
                    big_r = rep[2 * d + REP_R, ns]
                    s = qki * jnp.exp(jnp.where(seen, stat(d, ns, ST_R) - big_r, NEG_BIG))
                    g = jnp.exp(stat(d, ns, ST_MPREV) - big_r)
                    lhs.append(jnp.concatenate([(g * qi.astype(F32)).astype(BF16), s.astype(BF16)], axis=1))
            state = list(carry)
            for idx, (d, ns, t0) in enumerate(items):
                if want_out:
                    acc = _dot(lhs[idx], jnp.concatenate([state[d].astype(BF16), va[idx]], axis=0))
                    floor = jnp.exp(-rep[2 * d + REP_M, ns])
                    h_dst[d][pl.ds(t0, L), :] = (acc[:, :HEAD_DIM]
                                                 / jnp.maximum(jnp.abs(acc[:, HEAD_DIM:]), floor))
                state[d] = both(stat(d, ns, ST_DEC)) * state[d] + both(stat(d, ns, ST_INP)) * c_loc[idx]
            return tuple(state)
        return body

    def finish(hsum, o_src, t0):
        y = _rms(hsum, on_ref[...]) * jax.nn.sigmoid(o_src[pl.ds(t0, L), :].astype(F32))
        return y.astype(BF16)

    zero_state = jnp.zeros((HEAD_DIM, ML_AUG), F32)
    carry = make_body(n_ctx, n_ctx, n_lat, qa_c, kt_c, va_c, (hf_c, hb_c), need_ctx)(0, (zero_state, zero_state))
    lax.fori_loop(0, n_lat // ML_GROUP, make_body(ML_GROUP, n_lat, 0, qa_l, kt_l, va_l, (hf_l, hb_l), True),
                  carry)

    def finish_lat(n, carry):
        t0 = pl.multiple_of(n * L, L)
        yl_ref[pl.ds(t0, L), :] = finish(hf_l[pl.ds(t0, L), :] + hb_l[pl.ds(t0, L), :], ol_ref, t0)
        return carry

    lax.fori_loop(0, n_lat, finish_lat, 0, unroll=4)
    if need_ctx:
        def finish_ctx(n, carry):
            t0 = pl.multiple_of(n * L, L)
            yc_ref[pl.ds(t0, L), :] = finish(hf_c[pl.ds(t0, L), :] + hb_c[pl.ds(t0, L), :], oc_ref, t0)
            return carry

        lax.fori_loop(0, n_ctx, finish_ctx, 0, unroll=True)


def _mlstm_call(p_lat, p_ctx, gates, gate_b, conv_w, conv_b, out_norm, batch, seq, ctx_len, need_ctx):
    n_lat, n_ctx = seq // ML_CHUNK, ctx_len // ML_CHUNK
    n_pad = gates.shape[2]
    cq, ck = COL_ML_Q // HEAD_DIM, COL_ML_K // HEAD_DIM
    cv, co = COL_ML_V // HEAD_DIM, COL_ML_O // HEAD_DIM
    col = lambda rows, c0: pl.BlockSpec((rows, HEAD_DIM), lambda b, h: (b, c0 + h))
    out_specs = [pl.BlockSpec((seq, HEAD_DIM), lambda b, h: (b, h))]
    out_shape = [jax.ShapeDtypeStruct((batch * seq, ML_W), BF16)]
    if need_ctx:
        out_specs.append(pl.BlockSpec((ctx_len, HEAD_DIM), lambda b, h: (b, h)))
        out_shape.append(jax.ShapeDtypeStruct((batch * ctx_len, ML_W), BF16))

    def stream_scratch(n_tok):
        return [pltpu.VMEM((n_tok, HEAD_DIM), BF16), pltpu.VMEM((HEAD_DIM, n_tok), F32),
                pltpu.VMEM((n_tok, ML_AUG), BF16)]

    res = pl.pallas_call(
        functools.partial(_mlstm_kernel, n_lat=n_lat, n_ctx=n_ctx, need_ctx=need_ctx),
        grid=(batch, ML_HEADS),
        in_specs=[col(seq, cq), col(seq, ck), col(seq, cv), col(seq, co),
                  col(ctx_len, cq), col(ctx_len, ck), col(ctx_len, cv), col(ctx_len, co),
                  pl.BlockSpec((N_GATES, None, n_pad, ML_CHUNK), lambda b, h: (0, b, 0, 0)),
                  pl.BlockSpec((N_GATES, 1, ML_CHUNK), lambda b, h: (0, 0, 0)),
                  pl.BlockSpec((3, HEAD_DIM), lambda b, h: (0, h)),
                  pl.BlockSpec((3, HEAD_DIM), lambda b, h: (0, ML_HEADS + h)),
                  pl.BlockSpec((1, HEAD_DIM), lambda b, h: (0, h)),
                  pl.BlockSpec((1, HEAD_DIM), lambda b, h: (0, ML_HEADS + h)),
                  pl.BlockSpec((1, HEAD_DIM), lambda b, h: (0, h))],
        out_specs=out_specs,
        out_shape=out_shape,
        scratch_shapes=(stream_scratch(seq) + stream_scratch(ctx_len)
                        + [pltpu.VMEM((seq, HEAD_DIM), F32), pltpu.VMEM((seq, HEAD_DIM), F32),
                           pltpu.VMEM((ctx_len, HEAD_DIM), F32), pltpu.VMEM((ctx_len, HEAD_DIM), F32),
                           pltpu.VMEM((2 * N_STATS, n_pad, ML_CHUNK), F32),
                           pltpu.VMEM((4, n_lat + n_ctx, ML_CHUNK, ML_CHUNK), F32)]),
        compiler_params=_cparams(("parallel", "parallel")),
        name="mlstm",
    )(p_lat, p_lat, p_lat, p_lat, p_ctx, p_ctx, p_ctx, p_ctx, gates, gate_b,
      conv_w, conv_w, conv_b, conv_b, out_norm)
    return (res[0], res[1]) if need_ctx else (res[0], None)


def kernel(x, c, ctx, c_ctx, w_ada, b_ada, norm_ff1, ff1_gate, ff1_up, ff1_down, norm_mix, w_in, na_rpb,
           gqa_q_norm, gqa_k_norm, ml_conv_w, ml_conv_b, ml_gate_b, ml_out_norm, w_out, norm_ff2,
           ff2_gate, ff2_up, ff2_down, final_norm):
    batch, seq, d = x.shape
    ctx_len = ctx.shape[1]
    depth = w_ada.shape[0]
    ctx_group = batch
    n_lat_chunks, n_ctx_chunks = seq // ML_CHUNK, ctx_len // ML_CHUNK
    n_chunk_pad = -(-(n_lat_chunks + n_ctx_chunks) // 8) * 8

    xl = x.reshape(batch * seq, d)
    xc = ctx.reshape(batch * ctx_len, d)
    cond = jnp.zeros((8, d), F32).at[:batch].set(c).at[batch].set(c_ctx)
    rope_tabs = _rope_tables(seq)
    w_in_t = jnp.swapaxes(w_in, 1, 2)
    row2 = lambda v: v.reshape(1, -1)
    fin = row2(final_norm)

    for l in range(depth):
        last = l == depth - 1
        mod = _ada_call(cond, w_ada, l, row2(b_ada[l])).reshape(8 * N_MOD, 1, d)
        lat = dict(group_rows=seq, fixed_group=None)
        cx = dict(group_rows=None, fixed_group=ctx_group)

        cast = lambda w: _cast_call(w, l)
        wg, wu, wd = cast(ff1_gate), cast(ff1_up), cast(ff1_down)
        g1 = row2(norm_ff1[l])
        xl = _ffn_call(xl, mod, 0, g1, wg, wu, wd, fin, final_norm=False, **lat)
        xc = _ffn_call(xc, mod, 0, g1, wg, wu, wd, fin, final_norm=False, **cx)

        w_main_t = _cast_call(w_in_t, l, COL_GATES)
        w_gates_t = w_in_t[l, COL_GATES:]
        gm = row2(norm_mix[l])
        qg, kg = row2(gqa_q_norm[l]), row2(gqa_k_norm[l])
        p_lat, gt_lat = _inproj_call(xl, mod, gm, w_main_t, w_gates_t, qg, kg, rope_tabs, seq, rope=True, **lat)
        p_ctx, gt_ctx = _inproj_call(xc, mod, gm, w_main_t, w_gates_t, qg, kg, rope_tabs, seq, rope=False, **cx)

        ya_lat = _na_call(p_lat, p_ctx, *_na_toeplitz_slabs(na_rpb[l]), batch, seq, ctx_len)
        yb_lat = _gqa_call(p_lat, p_ctx, batch, seq, ctx_len)
        gates = jnp.concatenate(
            [gt_lat.reshape(N_GATES, batch, n_lat_chunks, ML_CHUNK),
             gt_ctx.reshape(N_GATES, batch, n_ctx_chunks, ML_CHUNK),
             jnp.zeros((N_GATES, batch, n_chunk_pad - n_lat_chunks - n_ctx_chunks, ML_CHUNK), F32)], axis=2)
        gate_b = jnp.broadcast_to(ml_gate_b[l][:, None, None], (N_GATES, 1, ML_CHUNK))
        yc_lat, yc_ctx = _mlstm_call(p_lat, p_ctx, gates, gate_b, ml_conv_w[l], row2(ml_conv_b[l]),
                                     row2(ml_out_norm[l]), batch, seq, ctx_len, not last)

        wo = cast(w_out)
        xl = _outproj_call(xl, mod, ya_lat, yb_lat, yc_lat, wo, **lat)
        wg, wu, wd = cast(ff2_gate), cast(ff2_up), cast(ff2_down)
        g2 = row2(norm_ff2[l])
        xl = _ffn_call(xl, mod, 6, g2, wg, wu, wd, fin, final_norm=last, **lat)
        if not last:
            ya_ctx = _ctx_attn_call(p_ctx, p_ctx, p_ctx, batch, ctx_len, NA_HEADS,
                                    COL_NA_Q // HEAD_DIM, COL_NA_K // HEAD_DIM, COL_NA_V // HEAD_DIM, 1,
                                    HEAD_DIM ** -0.5)
            yb_ctx = _ctx_attn_call(p_ctx, p_ctx, p_ctx, batch, ctx_len, GQA_Q_HEADS,
                                    COL_GQ_Q // HEAD_DIM, COL_GQ_K // HEAD_DIM, COL_GQ_V // HEAD_DIM,
                                    GQA_Q_HEADS // GQA_KV_HEADS, None)
            xc = _outproj_call(xc, mod, ya_ctx, yb_ctx, yc_ctx, wo, **cx)
            xc = _ffn_call(xc, mod, 6, g2, wg, wu, wd, fin, final_norm=False, **cx)
    return xl.reshape(batch, seq, d)
```

```python
import functools

import numpy as np
import jax
import jax.numpy as jnp
from jax import lax
from jax.experimental import pallas as pl
from jax.experimental.pallas import tpu as pltpu

F32 = jnp.float32
BF16 = jnp.bfloat16

GRID_W = 64
HEAD_DIM = 128
NA_HEADS = 4
GQA_Q_HEADS = 4
GQA_KV_HEADS = 2
ML_HEADS = 8
NA_W = NA_HEADS * HEAD_DIM
GQA_W = GQA_Q_HEADS * HEAD_DIM
KV_W = GQA_KV_HEADS * HEAD_DIM
ML_W = ML_HEADS * HEAD_DIM
NA_WIN_R = 8
NA_WIN_C = 16
ML_CHUNK = 128
ROPE_THETA = 10000.0
N_MOD = 9
EPS = 1e-6
N_GATES = 4 * ML_HEADS
COL_NA_Q = 0
COL_NA_K = NA_W
COL_NA_V = 2 * NA_W
COL_GQ_Q = 3 * NA_W
COL_GQ_K = COL_GQ_Q + GQA_W
COL_GQ_V = COL_GQ_K + KV_W
COL_ML_Q = COL_GQ_V + KV_W
COL_ML_K = COL_ML_Q + ML_W
COL_ML_V = COL_ML_K + ML_W
COL_ML_O = COL_ML_V + ML_W
COL_GATES = COL_ML_O + ML_W

NEG_BIG = -1e30
VMEM_LIMIT_V7X = 56 * 1024 * 1024
BF16_ROWS = 16
NORM_ROWS = 32
EPILOGUE_ROWS = 128
FFN_SUBTILE = 256
FFN_OUT_SUBTILE = 1024

NA_QROWS = 8
NA_KROWS = 16
NA_KEY_TILE = 256
NA_MASKED_SLAB = 2 * NA_WIN_R - 1


def _cparams(sem):
    return pltpu.CompilerParams(dimension_semantics=sem, vmem_limit_bytes=VMEM_LIMIT_V7X)


def _rms(x, gain):
    return x * lax.rsqrt(jnp.mean(x * x, axis=-1, keepdims=True) + EPS) * gain


def _norm_modulate_store(x_ref, gain_ref, scale_ref, shift_ref, h_ref, zero_ref):
    gain_scale = gain_ref[...] * (1.0 + scale_ref[...])
    shift = shift_ref[...]

    def body(i, carry):
        r0 = pl.multiple_of(i * NORM_ROWS, NORM_ROWS)
        x = x_ref[pl.ds(r0, NORM_ROWS), :]
        inv = lax.rsqrt(jnp.mean(x * x, axis=-1, keepdims=True) + EPS)
        h_ref[pl.ds(r0, NORM_ROWS), :] = (x * inv * gain_scale + shift).astype(h_ref.dtype)
        zero_ref[pl.ds(r0, NORM_ROWS), :] = jnp.zeros((NORM_ROWS, zero_ref.shape[1]), zero_ref.dtype)
        return carry

    lax.fori_loop(0, x_ref.shape[0] // NORM_ROWS, body, 0, unroll=4)


def _dot(a, b):
    return jnp.dot(a, b, preferred_element_type=F32)


def _dot_nt(a, b):
    return lax.dot_general(a, b, (((1,), (1,)), ((), ())), preferred_element_type=F32)


def _cast_kernel(w_ref, o_ref):
    o_ref[...] = w_ref[...].astype(o_ref.dtype)


def _cast_call(w_stack, layer, n_rows=None, row_blocks=8):
    _, rows, n_cols = w_stack.shape
    n_rows = rows if n_rows is None else n_rows
    tr = n_rows // row_blocks
    return pl.pallas_call(
        _cast_kernel,
        grid=(row_blocks,),
        in_specs=[pl.BlockSpec((None, tr, n_cols), lambda i: (layer, i, 0))],
        out_specs=pl.BlockSpec((tr, n_cols), lambda i: (i, 0)),
        out_shape=jax.ShapeDtypeStruct((n_rows, n_cols), BF16),
        compiler_params=_cparams(("parallel",)),
        name="weight_cast",
    )(w_stack)


def _ada_kernel(c_ref, w_ref, b_ref, o_ref):
    c = c_ref[...]
    a = (c * jax.nn.sigmoid(c)).astype(BF16)
    o_ref[...] = _dot(a, w_ref[...].astype(BF16)) + b_ref[...]


def _ada_call(cond, w_stack, layer, b, tn=1024):
    m, d = cond.shape
    n = w_stack.shape[2]
    return pl.pallas_call(
        _ada_kernel,
        grid=(n // tn,),
        in_specs=[pl.BlockSpec((m, d), lambda j: (0, 0)),
                  pl.BlockSpec((None, d, tn), lambda j: (layer, 0, j)),
                  pl.BlockSpec((1, tn), lambda j: (0, j))],
        out_specs=pl.BlockSpec((m, tn), lambda j: (0, j)),
        out_shape=jax.ShapeDtypeStruct((m, n), F32),
        compiler_params=_cparams(("arbitrary",)),
        name="ada_mod",
    )(cond, w_stack, b)


def _mod_spec(k, tm, group_rows, fixed_group):
    def index_map(i, *_):
        grp = fixed_group if fixed_group is not None else i // (group_rows // tm)
        return (grp * N_MOD + k, 0, 0)
    return index_map


def _ffn_kernel(x_ref, sh_ref, sc_ref, gt_ref, gain_ref, wg_ref, wu_ref, wd_ref, fin_ref, o_ref,
                h_scr, *, n_ff, final_norm):
    j = pl.program_id(1)

    @pl.when(j == 0)
    def _():
        _norm_modulate_store(x_ref, gain_ref, sc_ref, sh_ref, h_scr, o_ref)

    h = h_scr[...]
    tf = wg_ref.shape[1]
    for c0 in range(0, tf, FFN_SUBTILE):
        g = _dot(h, wg_ref[:, c0:c0 + FFN_SUBTILE])
        u = _dot(h, wu_ref[:, c0:c0 + FFN_SUBTILE])
        a = ((g * jax.nn.sigmoid(g)) * u).astype(BF16)
        for n0 in range(0, o_ref.shape[1], FFN_OUT_SUBTILE):
            o_ref[:, n0:n0 + FFN_OUT_SUBTILE] += _dot(a, wd_ref[c0:c0 + FFN_SUBTILE, n0:n0 + FFN_OUT_SUBTILE])

    @pl.when(j == n_ff - 1)
    def _():
        half_gate = 0.5 * gt_ref[...]

        step_rows = EPILOGUE_ROWS if final_norm else NORM_ROWS

        def body(i, carry):
            rows = [pl.multiple_of(i * step_rows + k, NORM_ROWS) for k in range(0, step_rows, NORM_ROWS)]
            outs = [x_ref[pl.ds(r, NORM_ROWS), :] + half_gate * o_ref[pl.ds(r, NORM_ROWS), :] for r in rows]
            if final_norm:
                outs = [_rms(out, fin_ref[...]) for out in outs]
            for r, out in zip(rows, outs):
                o_ref[pl.ds(r, NORM_ROWS), :] = out
            return carry

        lax.fori_loop(0, o_ref.shape[0] // step_rows, body, 0, unroll=1 if final_norm else 4)


def _ffn_call(x, mod, k0, gain, wg, wu, wd, fin, *, group_rows, fixed_group, final_norm,
              tm=1024, tf=512):
    t, d = x.shape
    dff = wg.shape[1]
    n_ff = dff // tf
    mspec = lambda k: pl.BlockSpec((None, 1, d), _mod_spec(k, tm, group_rows, fixed_group))
    return pl.pallas_call(
        functools.partial(_ffn_kernel, n_ff=n_ff, final_norm=final_norm),
        grid=(t // tm, n_ff),
        in_specs=[pl.BlockSpec((tm, d), lambda i, j: (i, 0)),
                  mspec(k0), mspec(k0 + 1), mspec(k0 + 2),
                  pl.BlockSpec((1, d), lambda i, j: (0, 0)),
                  pl.BlockSpec((d, tf), lambda i, j: (0, j)),
                  pl.BlockSpec((d, tf), lambda i, j: (0, j)),
                  pl.BlockSpec((tf, d), lambda i, j: (j, 0)),
                  pl.BlockSpec((1, d), lambda i, j: (0, 0))],
        out_specs=pl.BlockSpec((tm, d), lambda i, j: (i, 0)),
        out_shape=jax.ShapeDtypeStruct((t, d), F32),
        scratch_shapes=[pltpu.VMEM((tm, d), BF16)],
        compiler_params=_cparams(("parallel", "arbitrary")),
        name="macaron_ffn",
    )(x, mod, mod, mod, gain, wg, wu, wd, fin)


def _rope(x, cos, sin_lo, sin_hi):
    return (x * cos + pltpu.roll(x, HEAD_DIM - HEAD_DIM // 4, 1) * sin_lo
            + pltpu.roll(x, HEAD_DIM // 4, 1) * sin_hi)


def _inproj_kernel(x_ref, sh_ref, sc_ref, gain_ref, wt_ref, wgt_ref, qg_ref, kg_ref, cos_ref, slo_ref, shi_ref,
                   p_ref, gt_ref, h_scr, *, rope):
    j = pl.program_id(1)

    def gqa_head(x, gain, scale):
        y = _rms(x, gain)
        if rope:
            y = _rope(y, cos_ref[...], slo_ref[...], shi_ref[...])
        return y if scale is None else y * scale

    @pl.when(j == 0)
    def _():
        h = _rms(x_ref[...], gain_ref[...]) * (1.0 + sc_ref[...]) + sh_ref[...]
        hb = h.astype(BF16)
        h_scr[...] = hb
        gt_ref[...] = _dot_nt(wgt_ref[...].astype(BF16), hb)
        acc = _dot_nt(hb, wt_ref[...])
        p_ref[:, :COL_GQ_Q] = acc[:, :COL_GQ_Q].astype(p_ref.dtype)
        for c0, n_heads, gain_ref_h, scale in ((COL_GQ_Q, GQA_Q_HEADS, qg_ref, HEAD_DIM ** -0.5),
                                               (COL_GQ_K, GQA_KV_HEADS, kg_ref, None)):
            for hd in range(n_heads):
                sl = slice(c0 + hd * HEAD_DIM, c0 + (hd + 1) * HEAD_DIM)
                p_ref[:, sl] = gqa_head(acc[:, sl], gain_ref_h[...], scale).astype(p_ref.dtype)
        p_ref[:, COL_GQ_V:] = acc[:, COL_GQ_V:].astype(p_ref.dtype)

    @pl.when(j != 0)
    def _():
        p_ref[...] = _dot_nt(h_scr[...], wt_ref[...]).astype(p_ref.dtype)


def _inproj_call(x, mod, gain, w_main_t, w_gates_t, q_gain, k_gain, rope_tables, seq, *, rope, group_rows,
                 fixed_group, tm=512, tn=3328):
    t, d = x.shape
    n = w_main_t.shape[0]
    assert tn >= COL_GQ_V, "mixer B's q / k columns must sit in the first column tile"
    mspec = lambda k: pl.BlockSpec((None, 1, d), _mod_spec(k, tm, group_rows, fixed_group))
    per_seq = seq // tm if rope else 1
    tab_spec = pl.BlockSpec((tm, HEAD_DIM), lambda i, j: (i % per_seq, 0))
    head_row = pl.BlockSpec((1, HEAD_DIM), lambda i, j: (0, 0))
    return pl.pallas_call(
        functools.partial(_inproj_kernel, rope=rope),
        grid=(t // tm, n // tn),
        in_specs=[pl.BlockSpec((tm, d), lambda i, j: (i, 0)),
                  mspec(3), mspec(4),
                  pl.BlockSpec((1, d), lambda i, j: (0, 0)),
                  pl.BlockSpec((tn, d), lambda i, j: (j, 0)),
                  pl.BlockSpec((N_GATES, d), lambda i, j: (0, 0)),
                  head_row, head_row, tab_spec, tab_spec, tab_spec],
        out_specs=[pl.BlockSpec((tm, tn), lambda i, j: (i, j)),
                   pl.BlockSpec((N_GATES, tm), lambda i, j: (0, i))],
        out_shape=[jax.ShapeDtypeStruct((t, n), BF16),
                   jax.ShapeDtypeStruct((N_GATES, t), F32)],
        scratch_shapes=[pltpu.VMEM((tm, d), BF16)],
        compiler_params=_cparams(("parallel", "arbitrary")),
        name="mixer_in_proj",
    )(x, mod, mod, gain, w_main_t, w_gates_t, q_gain, k_gain, *rope_tables)


def _outproj_kernel(x_ref, gt_ref, ya_ref, yb_ref, yc_ref, w_ref, o_ref):
    acc = _dot(ya_ref[...], w_ref[0:NA_W, :])
    acc += _dot(yb_ref[...], w_ref[NA_W:NA_W + GQA_W, :])
    acc += _dot(yc_ref[...], w_ref[NA_W + GQA_W:, :])
    o_ref[...] = x_ref[...] + gt_ref[...] * acc


def _outproj_call(x, mod, ya, yb, yc, w, *, group_rows, fixed_group, tm=512):
    t, d = x.shape
    return pl.pallas_call(
        _outproj_kernel,
        grid=(t // tm,),
        in_specs=[pl.BlockSpec((tm, d), lambda i: (i, 0)),
                  pl.BlockSpec((None, 1, d), _mod_spec(5, tm, group_rows, fixed_group)),
                  pl.BlockSpec((tm, NA_W), lambda i: (i, 0)),
                  pl.BlockSpec((tm, GQA_W), lambda i: (i, 0)),
                  pl.BlockSpec((tm, ML_W), lambda i: (i, 0)),
                  pl.BlockSpec((d, d), lambda i: (0, 0))],
        out_specs=pl.BlockSpec((tm, d), lambda i: (i, 0)),
        out_shape=jax.ShapeDtypeStruct((t, d), F32),
        compiler_params=_cparams(("parallel",)),
        name="mixer_out_proj",
    )(x, mod, ya, yb, yc, w)


def _na_kernel(q_ref, k_ref, v_ref, kc_ref, vc_ref, tl_ref, tr_ref, o_ref, mb_scr, *, rows):
    rb = pl.program_id(2)
    nblk = rows // NA_QROWS
    key_row0 = jnp.clip(rb * NA_QROWS - NA_WIN_R // 2, 0, rows - NA_KROWS)

    @pl.when((rb <= 1) | (rb == nblk - 1))
    def _():
        for a in range(NA_QROWS):
            r = rb * NA_QROWS + a
            row_start = jnp.clip(r - NA_WIN_R // 2, 0, rows - NA_WIN_R)
            for j in range(NA_KROWS // 2):
                sel = []
                for kr in (key_row0 + 2 * j, key_row0 + 2 * j + 1):
                    in_win = (kr >= row_start) & (kr < row_start + NA_WIN_R)
                    sel.append(jnp.where(in_win, kr - r + (NA_WIN_R - 1), NA_MASKED_SLAB))
                mb_scr[a * GRID_W:(a + 1) * GRID_W, 2 * j * GRID_W:(2 * j + 2) * GRID_W] = (
                    tl_ref[sel[0]] + tr_ref[sel[1]])

    start = pl.multiple_of(key_row0 * GRID_W, GRID_W)
    q = (q_ref[...].astype(F32) * (HEAD_DIM ** -0.5)).astype(BF16)
    n_keys = NA_KROWS * GRID_W
    key_tile = lambda t0: k_ref[pl.ds(start + t0, NA_KEY_TILE), :]
    s = _dot_nt(q, kc_ref[...])
    s_next = _dot_nt(q, key_tile(0))
    m = jnp.max(s, axis=-1, keepdims=True)
    p = jnp.exp(s - m)
    den = jnp.sum(p, axis=-1, keepdims=True)
    acc = _dot(p.astype(BF16), vc_ref[...])
    for t0 in range(0, n_keys, NA_KEY_TILE):
        s = s_next + mb_scr[:, t0:t0 + NA_KEY_TILE]
        if t0 + NA_KEY_TILE < n_keys:
            s_next = _dot_nt(q, key_tile(t0 + NA_KEY_TILE))
        m_new = jnp.maximum(m, jnp.max(s, axis=-1, keepdims=True))
        p = jnp.exp(s - m_new)
        alpha = jnp.exp(m - m_new)
        den = alpha * den + jnp.sum(p, axis=-1, keepdims=True)
        acc = alpha * acc + _dot(p.astype(BF16), v_ref[pl.ds(start + t0, NA_KEY_TILE), :])
        m = m_new
    o_ref[...] = (acc / den).astype(o_ref.dtype)


def _na_toeplitz_slabs(rpb):
    n_h, n_dr, n_dc = rpb.shape
    edge = GRID_W - NA_WIN_C
    ext = jnp.concatenate(
        [rpb[..., NA_WIN_C - 1:], jnp.broadcast_to(rpb[..., n_dc - 1:], (n_h, n_dr, edge)),
         jnp.zeros((n_h, n_dr, 1), F32),
         jnp.broadcast_to(rpb[..., :1], (n_h, n_dr, edge)), rpb[..., :NA_WIN_C - 1]], axis=-1)
    width = 2 * GRID_W
    tiled = jnp.broadcast_to(ext[:, :, None, :], (n_h, n_dr, GRID_W, width)).reshape(n_h, n_dr, GRID_W * width)
    toep = tiled[..., :GRID_W * (width - 1)].reshape(n_h, n_dr, GRID_W, width - 1)[..., :GRID_W]
    cq = np.arange(GRID_W)
    col_start = np.clip(cq - NA_WIN_C // 2, 0, GRID_W - NA_WIN_C)
    col_ok = (cq[None, :] >= col_start[:, None]) & (cq[None, :] < col_start[:, None] + NA_WIN_C)
    slabs = jnp.where(col_ok, toep, NEG_BIG)
    slabs = jnp.concatenate([slabs, jnp.full((n_h, 1, GRID_W, GRID_W), NEG_BIG, F32)], axis=1)
    zeros = jnp.zeros_like(slabs)
    return jnp.concatenate([slabs, zeros], axis=-1), jnp.concatenate([zeros, slabs], axis=-1)


def _na_call(p_lat, p_ctx, slabs_left, slabs_right, batch, seq, ctx_len):
    rows = seq // GRID_W
    nblk = rows // NA_QROWS
    tq = NA_QROWS * GRID_W
    nk = NA_KROWS * GRID_W
    cq, ck, cv = COL_NA_Q // HEAD_DIM, COL_NA_K // HEAD_DIM, COL_NA_V // HEAD_DIM
    slab_spec = pl.BlockSpec((None, NA_MASKED_SLAB + 1, GRID_W, 2 * GRID_W), lambda b, h, r: (h, 0, 0, 0))

    return pl.pallas_call(
        functools.partial(_na_kernel, rows=rows),
        grid=(batch, NA_HEADS, nblk),
        in_specs=[pl.BlockSpec((tq, HEAD_DIM), lambda b, h, r: (b * nblk + r, cq + h)),
                  pl.BlockSpec((seq, HEAD_DIM), lambda b, h, r: (b, ck + h)),
                  pl.BlockSpec((seq, HEAD_DIM), lambda b, h, r: (b, cv + h)),
                  pl.BlockSpec((ctx_len, HEAD_DIM), lambda b, h, r: (b, ck + h)),
                  pl.BlockSpec((ctx_len, HEAD_DIM), lambda b, h, r: (b, cv + h)),
                  slab_spec, slab_spec],
        out_specs=pl.BlockSpec((tq, HEAD_DIM), lambda b, h, r: (b * nblk + r, h)),
        out_shape=jax.ShapeDtypeStruct((batch * seq, NA_W), BF16),
        scratch_shapes=[pltpu.VMEM((tq, nk), F32)],
        compiler_params=_cparams(("parallel", "parallel", "arbitrary")),
        name="neighbourhood_attn",
    )(p_lat, p_lat, p_lat, p_ctx, p_ctx, slabs_left, slabs_right)


def _ctx_attn_kernel(q_ref, k_ref, v_ref, o_ref, *, scale):
    s = _dot_nt(q_ref[...], k_ref[...])
    if scale is not None:
        s = s * scale
    m = jnp.max(s, axis=-1, keepdims=True)
    p = jnp.exp(s - m)
    den = jnp.sum(p, axis=-1, keepdims=True)
    o_ref[...] = (_dot(p.astype(BF16), v_ref[...]) / den).astype(o_ref.dtype)


def _ctx_attn_call(q_arr, k_arr, v_arr, batch, ctx_len, n_heads, q_col, k_col, v_col, kv_group, scale):
    return pl.pallas_call(
        functools.partial(_ctx_attn_kernel, scale=scale),
        grid=(batch, n_heads),
        in_specs=[pl.BlockSpec((ctx_len, HEAD_DIM), lambda b, h: (b, q_col + h)),
                  pl.BlockSpec((ctx_len, HEAD_DIM), lambda b, h: (b, k_col + h // kv_group)),
                  pl.BlockSpec((ctx_len, HEAD_DIM), lambda b, h: (b, v_col + h // kv_group))],
        out_specs=pl.BlockSpec((ctx_len, HEAD_DIM), lambda b, h: (b, h)),
        out_shape=jax.ShapeDtypeStruct((batch * ctx_len, n_heads * HEAD_DIM), BF16),
        compiler_params=_cparams(("parallel", "parallel")),
        name="ctx_attn",
    )(q_arr, k_arr, v_arr)


def _rope_tables(n_tokens):
    t = jnp.arange(n_tokens)
    row = (t // GRID_W).astype(F32)
    col = (t % GRID_W).astype(F32)
    half = HEAD_DIM // 2
    inv_freq = 1.0 / (ROPE_THETA ** (jnp.arange(0, half, 2, dtype=F32) / half))
    ang_r = row[:, None] * inv_freq[None, :]
    ang_c = col[:, None] * inv_freq[None, :]
    ang = jnp.concatenate([ang_r, ang_r, ang_c, ang_c], axis=-1)
    cos, sin = jnp.cos(ang), jnp.sin(ang)
    lo = (jnp.arange(HEAD_DIM) % half) < (half // 2)
    return cos, jnp.where(lo, -sin, 0.0), jnp.where(lo, 0.0, sin)


def _gqa_kernel(q_ref, kl_ref, vl_ref, kc_ref, vc_ref, o_ref, *, tk):
    tq = q_ref.shape[0]
    grp = q_ref.shape[1] // HEAD_DIM
    q = jnp.concatenate([q_ref[:, g * HEAD_DIM:(g + 1) * HEAD_DIM] for g in range(grp)], axis=0)

    n_tiles = kl_ref.shape[0] // tk
    s = _dot_nt(q, kc_ref[...])
    s_next = _dot_nt(q, kl_ref[0:tk, :])
    m = jnp.max(s, axis=-1, keepdims=True)
    p = jnp.exp(s - m)
    den = jnp.sum(p, axis=-1, keepdims=True)
    acc = _dot(p.astype(BF16), vc_ref[...])
    for t in range(n_tiles):
        s = s_next
        if t + 1 < n_tiles:
            s_next = _dot_nt(q, kl_ref[(t + 1) * tk:(t + 2) * tk, :])
        m_new = jnp.maximum(m, jnp.max(s, axis=-1, keepdims=True))
        p = jnp.exp(s - m_new)
        alpha = jnp.exp(m - m_new)
        den = alpha * den + jnp.sum(p, axis=-1, keepdims=True)
        acc = alpha * acc + _dot(p.astype(BF16), vl_ref[t * tk:(t + 1) * tk, :])
        m = m_new
    out = (acc / den).astype(o_ref.dtype)
    for g in range(grp):
        o_ref[:, g * HEAD_DIM:(g + 1) * HEAD_DIM] = out[g * tq:(g + 1) * tq]


def _gqa_call(p_lat, p_ctx, batch, seq, ctx_len, tq=512, tk=2048):
    grp = GQA_Q_HEADS // GQA_KV_HEADS
    nq = seq // tq
    cq = COL_GQ_Q // (grp * HEAD_DIM)
    ck, cv = COL_GQ_K // HEAD_DIM, COL_GQ_V // HEAD_DIM
    return pl.pallas_call(
        functools.partial(_gqa_kernel, tk=tk),
        grid=(batch, GQA_KV_HEADS, nq),
        in_specs=[pl.BlockSpec((tq, grp * HEAD_DIM), lambda b, h, i: (b * nq + i, cq + h)),
                  pl.BlockSpec((seq, HEAD_DIM), lambda b, h, i: (b, ck + h)),
                  pl.BlockSpec((seq, HEAD_DIM), lambda b, h, i: (b, cv + h)),
                  pl.BlockSpec((ctx_len, HEAD_DIM), lambda b, h, i: (b, ck + h)),
                  pl.BlockSpec((ctx_len, HEAD_DIM), lambda b, h, i: (b, cv + h))],
        out_specs=pl.BlockSpec((tq, grp * HEAD_DIM), lambda b, h, i: (b * nq + i, h)),
        out_shape=jax.ShapeDtypeStruct((batch * seq, GQA_W), BF16),
        compiler_params=_cparams(("parallel", "parallel", "arbitrary")),
        name="gqa_attn",
    )(p_lat, p_lat, p_lat, p_ctx, p_ctx)


ST_R, ST_W, ST_BTOT, ST_AMAX, ST_MPREV, ST_DEC, ST_INP, ST_BIGR, ST_M = range(9)
N_STATS = 9
REP_R, REP_M = range(2)
ML_AUG = 2 * HEAD_DIM
ML_GROUP = 8


def _mlstm_kernel(ql_ref, kl_ref, vl_ref, ol_ref, qc_ref, kc_ref, vc_ref, oc_ref, g_ref, gb_ref,
                  cwq_ref, cwk_ref, cbq_ref, cbk_ref, on_ref, yl_ref, *rest,
                  n_lat, n_ctx, need_ctx):
    if need_ctx:
        yc_ref = rest[0]
        rest = rest[1:]
    qa_l, kt_l, va_l, qa_c, kt_c, va_c, hf_l, hb_l, hf_c, hb_c, st, rep = rest
    L = ML_CHUNK
    n_all = n_lat + n_ctx
    h = pl.program_id(1)
    row = lax.broadcasted_iota(jnp.int32, (L, L), 0)
    col = lax.broadcasted_iota(jnp.int32, (L, L), 1)

    tri_fw = (row <= col).astype(F32)
    tri_bw = (row >= col).astype(F32)
    b_cum = []
    for d, tri in enumerate((tri_fw, tri_bw)):
        log_i = g_ref[2 * d * ML_HEADS + h] + gb_ref[2 * d * ML_HEADS + h]
        log_f = jax.nn.log_sigmoid(g_ref[(2 * d + 1) * ML_HEADS + h] + gb_ref[(2 * d + 1) * ML_HEADS + h])
        b = jnp.dot(log_f, tri, preferred_element_type=F32, precision=lax.Precision.HIGHEST)
        b_tot = b[:, L - 1:L] if d == 0 else b[:, 0:1]
        a = b_tot - b + log_i
        a_max = jnp.max(a, axis=-1, keepdims=True)
        r = log_i - b
        lane = lax.broadcasted_iota(jnp.int32, r.shape, 1)
        run = r
        shift = 1
        while shift < L:
            if d == 0:
                run = jnp.maximum(run, jnp.where(lane >= shift, pltpu.roll(run, shift, 1), NEG_BIG))
            else:
                run = jnp.maximum(run, jnp.where(lane < L - shift, pltpu.roll(run, L - shift, 1), NEG_BIG))
            shift *= 2
        base = d * N_STATS
        st[base + ST_R] = r
        st[base + ST_W] = jnp.exp(a - a_max)
        st[base + ST_BTOT] = jnp.broadcast_to(b_tot, b.shape)
        st[base + ST_AMAX] = jnp.broadcast_to(a_max, b.shape)
        st[base + ST_BIGR] = run
        st[base + ST_MPREV] = jnp.zeros_like(b)
        b_cum.append(b)

    def stabiliser_step(i, carry):
        new = []
        for d, m_prev in enumerate(carry):
            in_ctx = i < n_ctx
            j = jnp.where(in_ctx, i, i - n_ctx)
            if d == 0:
                n = jnp.where(in_ctx, n_lat + j, j)
            else:
                n = jnp.where(in_ctx, n_lat + n_ctx - 1 - j, n_lat - 1 - j)
            base = d * N_STATS
            b_tot = st[base + ST_BTOT, pl.ds(n, 1), :]
            a_max = st[base + ST_AMAX, pl.ds(n, 1), :]
            m_new = jnp.maximum(b_tot + m_prev, a_max)
            st[base + ST_MPREV, pl.ds(n, 1), :] = m_prev
            st[base + ST_DEC, pl.ds(n, 1), :] = jnp.exp(b_tot + m_prev - m_new)
            st[base + ST_INP, pl.ds(n, 1), :] = jnp.exp(a_max - m_new)
            new.append(m_new)
        return tuple(new)

    lax.fori_loop(0, n_all, stabiliser_step, (jnp.zeros((1, L), F32), jnp.zeros((1, L), F32)))

    for d in range(2):
        base = d * N_STATS
        big_r = jnp.maximum(st[base + ST_BIGR], st[base + ST_MPREV])
        st[base + ST_BIGR] = big_r
        st[base + ST_M] = b_cum[d] + big_r

    def column_tables(n):
        for d in range(2):
            for k_rep, k_st in ((REP_R, ST_BIGR), (REP_M, ST_M)):
                rows = jnp.broadcast_to(st[d * N_STATS + k_st, pl.ds(n, 1), :], (L, L))
                rep[2 * d + k_rep, n] = rows.T

    def prepare_stream(q_src, k_src, v_src, q_dst, kt_dst, va_dst, n_chunks, stat0):
        n_tok = n_chunks * L
        sub = lax.broadcasted_iota(jnp.int32, (L, HEAD_DIM), 0)

        def conv_silu(src_ref, w_ref, b_ref, n, t0):
            x = src_ref[pl.ds(t0, L), :].astype(F32)
            prev_t0 = pl.multiple_of(jnp.maximum(t0 - BF16_ROWS, 0), BF16_ROWS)
            next_t0 = pl.multiple_of(jnp.minimum(t0 + L, n_tok - BF16_ROWS), BF16_ROWS)
            prev_row = (src_ref[pl.ds(prev_t0, BF16_ROWS), :].astype(F32)[BF16_ROWS - 1:, :]
                        * jnp.where(n > 0, 1.0, 0.0))
            next_row = (src_ref[pl.ds(next_t0, BF16_ROWS), :].astype(F32)[:1, :]
                        * jnp.where(n < n_chunks - 1, 1.0, 0.0))
            x_prev = jnp.where(sub == 0, prev_row, pltpu.roll(x, 1, 0))
            x_next = jnp.where(sub == L - 1, next_row, pltpu.roll(x, L - 1, 0))
            y = w_ref[0:1, :] * x_prev + w_ref[1:2, :] * x + w_ref[2:3, :] * x_next + b_ref[...]
            return y * jax.nn.sigmoid(y)

        def body(n, carry):
            t0 = pl.multiple_of(n * L, L)
            q_dst[pl.ds(t0, L), :] = conv_silu(q_src, cwq_ref, cbq_ref, n, t0).astype(BF16)
            k_act = conv_silu(k_src, cwk_ref, cbk_ref, n, t0) * (HEAD_DIM ** -0.5)
            kt_dst[:, pl.ds(t0, L)] = k_act.T
            va_dst[pl.ds(t0, L), 0:HEAD_DIM] = v_src[pl.ds(t0, L), :]
            va_dst[pl.ds(t0, L), HEAD_DIM:ML_AUG] = jnp.ones((L, HEAD_DIM), BF16)
            column_tables(stat0 + n)
            return carry

        lax.fori_loop(0, n_chunks, body, 0, unroll=min(8, n_chunks))

    prepare_stream(ql_ref, kl_ref, vl_ref, qa_l, kt_l, va_l, n_lat, 0)
    prepare_stream(qc_ref, kc_ref, vc_ref, qa_c, kt_c, va_c, n_ctx, n_lat)

    def stat(d, n, k):
        return st[d * N_STATS + k, pl.ds(n, 1), :]

    both = lambda x: jnp.concatenate([x, x], axis=-1)

    def make_body(group, n_chunks, stat0, q_src, kt_src, va_src, h_dst, want_out):
        def body(i, carry):
            items = []
            for u in range(group):
                nf = i * group + u
                for d, n in ((0, nf), (1, n_chunks - 1 - nf)):
                    items.append((d, stat0 + n, pl.multiple_of(n * L, L)))
            va = [va_src[pl.ds(t0, L), :] for _, _, t0 in items]
            c_loc, qk, q = [], [], []
            for (d, ns, t0), v in zip(items, va):
                kt = kt_src[:, pl.ds(t0, L)]
                c_loc.append(_dot((kt * stat(d, ns, ST_W)).astype(BF16), v))
                if want_out:
                    q.append(q_src[pl.ds(t0, L), :])
                    qk.append(_dot(q[-1], kt.astype(BF16)))
            lhs = []
            if want_out:
                for (d, ns, t0), qi, qki in zip(items, q, qk):
                    seen = (col <= row) if d == 0 else (col >= row)
                    big_r = rep[2 * d + REP_R, ns]
                    s = qki * jnp.exp(jnp.where(seen, stat(d, ns, ST_R) - big_r, NEG_BIG))
                    g = jnp.exp(stat(d, ns, ST_MPREV) - big_r)
                    lhs.append(jnp.concatenate([(g * qi.astype(F32)).astype(BF16), s.astype(BF16)], axis=1))
            state = list(carry)
            for idx, (d, ns, t0) in enumerate(items):
                if want_out:
                    acc = _dot(lhs[idx], jnp.concatenate([state[d].astype(BF16), va[idx]], axis=0))
                    floor = jnp.exp(-rep[2 * d + REP_M, ns])
                    h_dst[d][pl.ds(t0, L), :] = (acc[:, :HEAD_DIM]
                                                 / jnp.maximum(jnp.abs(acc[:, HEAD_DIM:]), floor))
                state[d] = both(stat(d, ns, ST_DEC)) * state[d] + both(stat(d, ns, ST_INP)) * c_loc[idx]
            return tuple(state)
        return body

    def finish(hsum, o_src, t0):
        y = _rms(hsum, on_ref[...]) * jax.nn.sigmoid(o_src[pl.ds(t0, L), :].astype(F32))
        return y.astype(BF16)

    zero_state = jnp.zeros((HEAD_DIM, ML_AUG), F32)
    carry = make_body(n_ctx, n_ctx, n_lat, qa_c, kt_c, va_c, (hf_c, hb_c), need_ctx)(0, (zero_state, zero_state))
    lax.fori_loop(0, n_lat // ML_GROUP, make_body(ML_GROUP, n_lat, 0, qa_l, kt_l, va_l, (hf_l, hb_l), True),
                  carry)

    def finish_lat(n, carry):
        t0 = pl.multiple_of(n * L, L)
        yl_ref[pl.ds(t0, L), :] = finish(hf_l[pl.ds(t0, L), :] + hb_l[pl.ds(t0, L), :], ol_ref, t0)
        return carry

    lax.fori_loop(0, n_lat, finish_lat, 0, unroll=4)
    if need_ctx:
        def finish_ctx(n, carry):
            t0 = pl.multiple_of(n * L, L)
            yc_ref[pl.ds(t0, L), :] = finish(hf_c[pl.ds(t0, L), :] + hb_c[pl.ds(t0, L), :], oc_ref, t0)
            return carry

        lax.fori_loop(0, n_ctx, finish_ctx, 0, unroll=True)


def _mlstm_call(p_lat, p_ctx, gates, gate_b, conv_w, conv_b, out_norm, batch, seq, ctx_len, need_ctx):
    n_lat, n_ctx = seq // ML_CHUNK, ctx_len // ML_CHUNK
    n_pad = gates.shape[2]
    cq, ck = COL_ML_Q // HEAD_DIM, COL_ML_K // HEAD_DIM
    cv, co = COL_ML_V // HEAD_DIM, COL_ML_O // HEAD_DIM
    col = lambda rows, c0: pl.BlockSpec((rows, HEAD_DIM), lambda b, h: (b, c0 + h))
    out_specs = [pl.BlockSpec((seq, HEAD_DIM), lambda b, h: (b, h))]
    out_shape = [jax.ShapeDtypeStruct((batch * seq, ML_W), BF16)]
    if need_ctx:
        out_specs.append(pl.BlockSpec((ctx_len, HEAD_DIM), lambda b, h: (b, h)))
        out_shape.append(jax.ShapeDtypeStruct((batch * ctx_len, ML_W), BF16))

    def stream_scratch(n_tok):
        return [pltpu.VMEM((n_tok, HEAD_DIM), BF16), pltpu.VMEM((HEAD_DIM, n_tok), F32),
                pltpu.VMEM((n_tok, ML_AUG), BF16)]

    res = pl.pallas_call(
        functools.partial(_mlstm_kernel, n_lat=n_lat, n_ctx=n_ctx, need_ctx=need_ctx),
        grid=(batch, ML_HEADS),
        in_specs=[col(seq, cq), col(seq, ck), col(seq, cv), col(seq, co),
                  col(ctx_len, cq), col(ctx_len, ck), col(ctx_len, cv), col(ctx_len, co),
                  pl.BlockSpec((N_GATES, None, n_pad, ML_CHUNK), lambda b, h: (0, b, 0, 0)),
                  pl.BlockSpec((N_GATES, 1, ML_CHUNK), lambda b, h: (0, 0, 0)),
                  pl.BlockSpec((3, HEAD_DIM), lambda b, h: (0, h)),
                  pl.BlockSpec((3, HEAD_DIM), lambda b, h: (0, ML_HEADS + h)),
                  pl.BlockSpec((1, HEAD_DIM), lambda b, h: (0, h)),
                  pl.BlockSpec((1, HEAD_DIM), lambda b, h: (0, ML_HEADS + h)),
                  pl.BlockSpec((1, HEAD_DIM), lambda b, h: (0, h))],
        out_specs=out_specs,
        out_shape=out_shape,
        scratch_shapes=(stream_scratch(seq) + stream_scratch(ctx_len)
                        + [pltpu.VMEM((seq, HEAD_DIM), F32), pltpu.VMEM((seq, HEAD_DIM), F32),
                           pltpu.VMEM((ctx_len, HEAD_DIM), F32), pltpu.VMEM((ctx_len, HEAD_DIM), F32),
                           pltpu.VMEM((2 * N_STATS, n_pad, ML_CHUNK), F32),
                           pltpu.VMEM((4, n_lat + n_ctx, ML_CHUNK, ML_CHUNK), F32)]),
        compiler_params=_cparams(("parallel", "parallel")),
        name="mlstm",
    )(p_lat, p_lat, p_lat, p_lat, p_ctx, p_ctx, p_ctx, p_ctx, gates, gate_b,
      conv_w, conv_w, conv_b, conv_b, out_norm)
    return (res[0], res[1]) if need_ctx else (res[0], None)


def kernel(x, c, ctx, c_ctx, w_ada, b_ada, norm_ff1, ff1_gate, ff1_up, ff1_down, norm_mix, w_in, na_rpb,
           gqa_q_norm, gqa_k_norm, ml_conv_w, ml_conv_b, ml_gate_b, ml_out_norm, w_out, norm_ff2,
           ff2_gate, ff2_up, ff2_down, final_norm):
    batch, seq, d = x.shape
    ctx_len = ctx.shape[1]
    depth = w_ada.shape[0]
    ctx_group = batch
    n_lat_chunks, n_ctx_chunks = seq // ML_CHUNK, ctx_len // ML_CHUNK
    n_chunk_pad = -(-(n_lat_chunks + n_ctx_chunks) // 8) * 8

    xl = x.reshape(batch * seq, d)
    xc = ctx.reshape(batch * ctx_len, d)
    cond = jnp.zeros((8, d), F32).at[:batch].set(c).at[batch].set(c_ctx)
    rope_tabs = _rope_tables(seq)
    w_in_t = jnp.swapaxes(w_in, 1, 2)
    row2 = lambda v: v.reshape(1, -1)
    fin = row2(final_norm)

    for l in range(depth):
        last = l == depth - 1
        mod = _ada_call(cond, w_ada, l, row2(b_ada[l])).reshape(8 * N_MOD, 1, d)
        lat = dict(group_rows=seq, fixed_group=None)
        cx = dict(group_rows=None, fixed_group=ctx_group)

        cast = lambda w: _cast_call(w, l)
        wg, wu, wd = cast(ff1_gate), cast(ff1_up), cast(ff1_down)
        g1 = row2(norm_ff1[l])
        xl = _ffn_call(xl, mod, 0, g1, wg, wu, wd, fin, final_norm=False, **lat)
        xc = _ffn_call(xc, mod, 0, g1, wg, wu, wd, fin, final_norm=False, **cx)

        w_main_t = _cast_call(w_in_t, l, COL_GATES)
        w_gates_t = w_in_t[l, COL_GATES:]
        gm = row2(norm_mix[l])
        qg, kg = row2(gqa_q_norm[l]), row2(gqa_k_norm[l])
        p_lat, gt_lat = _inproj_call(xl, mod, gm, w_main_t, w_gates_t, qg, kg, rope_tabs, seq, rope=True, **lat)
        p_ctx, gt_ctx = _inproj_call(xc, mod, gm, w_main_t, w_gates_t, qg, kg, rope_tabs, seq, rope=False, **cx)

        ya_lat = _na_call(p_lat, p_ctx, *_na_toeplitz_slabs(na_rpb[l]), batch, seq, ctx_len)
        yb_lat = _gqa_call(p_lat, p_ctx, batch, seq, ctx_len)
        gates = jnp.concatenate(
            [gt_lat.reshape(N_GATES, batch, n_lat_chunks, ML_CHUNK),
             gt_ctx.reshape(N_GATES, batch, n_ctx_chunks, ML_CHUNK),
             jnp.zeros((N_GATES, batch, n_chunk_pad - n_lat_chunks - n_ctx_chunks, ML_CHUNK), F32)], axis=2)
        gate_b = jnp.broadcast_to(ml_gate_b[l][:, None, None], (N_GATES, 1, ML_CHUNK))
        yc_lat, yc_ctx = _mlstm_call(p_lat, p_ctx, gates, gate_b, ml_conv_w[l], row2(ml_conv_b[l]),
                                     row2(ml_out_norm[l]), batch, seq, ctx_len, not last)

        wo = cast(w_out)
        xl = _outproj_call(xl, mod, ya_lat, yb_lat, yc_lat, wo, **lat)
        wg, wu, wd = cast(ff2_gate), cast(ff2_up), cast(ff2_down)
        g2 = row2(norm_ff2[l])
        xl = _ffn_call(xl, mod, 6, g2, wg, wu, wd, fin, final_norm=last, **lat)
        if not last:
            ya_ctx = _ctx_attn_call(p_ctx, p_ctx, p_ctx, batch, ctx_len, NA_HEADS,
                                    COL_NA_Q // HEAD_DIM, COL_NA_K // HEAD_DIM, COL_NA_V // HEAD_DIM, 1,
                                    HEAD_DIM ** -0.5)
            yb_ctx = _ctx_attn_call(p_ctx, p_ctx, p_ctx, batch, ctx_len, GQA_Q_HEADS,
                                    COL_GQ_Q // HEAD_DIM, COL_GQ_K // HEAD_DIM, COL_GQ_V // HEAD_DIM,
                                    GQA_Q_HEADS // GQA_KV_HEADS, None)
            xc = _outproj_call(xc, mod, ya_ctx, yb_ctx, yc_ctx, wo, **cx)
            xc = _ffn_call(xc, mod, 6, g2, wg, wu, wd, fin, final_norm=False, **cx)
    return xl.reshape(batch, seq, d)
```

```python
import functools

import numpy as np
import jax
import jax.numpy as jnp
from jax import lax
from jax.experimental import pallas as pl
from jax.experimental.pallas import tpu as pltpu

F32 = jnp.float32
BF16 = jnp.bfloat16

GRID_W = 64
HEAD_DIM = 128
NA_HEADS = 4
GQA_Q_HEADS = 4
GQA_KV_HEADS = 2
ML_HEADS = 8
NA_W = NA_HEADS * HEAD_DIM
GQA_W = GQA_Q_HEADS * HEAD_DIM
KV_W = GQA_KV_HEADS * HEAD_DIM
ML_W = ML_HEADS * HEAD_DIM
NA_WIN_R = 8
NA_WIN_C = 16
ML_CHUNK = 128
ROPE_THETA = 10000.0
N_MOD = 9
EPS = 1e-6
N_GATES = 4 * ML_HEADS
COL_NA_Q = 0
COL_NA_K = NA_W
COL_NA_V = 2 * NA_W
COL_GQ_Q = 3 * NA_W
COL_GQ_K = COL_GQ_Q + GQA_W
COL_GQ_V = COL_GQ_K + KV_W
COL_ML_Q = COL_GQ_V + KV_W
COL_ML_K = COL_ML_Q + ML_W
COL_ML_V = COL_ML_K + ML_W
COL_ML_O = COL_ML_V + ML_W
COL_GATES = COL_ML_O + ML_W

NEG_BIG = -1e30
VMEM_LIMIT_V7X = 56 * 1024 * 1024
BF16_ROWS = 16
NORM_ROWS = 32
EPILOGUE_ROWS = 128
FFN_SUBTILE = 256
FFN_OUT_SUBTILE = 1024

NA_QROWS = 8
NA_KROWS = 16
NA_KEY_TILE = 256
NA_MASKED_SLAB = 2 * NA_WIN_R - 1


def _cparams(sem):
    return pltpu.CompilerParams(dimension_semantics=sem, vmem_limit_bytes=VMEM_LIMIT_V7X)


def _rms(x, gain):
    return x * lax.rsqrt(jnp.mean(x * x, axis=-1, keepdims=True) + EPS) * gain


def _norm_modulate_store(x_ref, gain_ref, scale_ref, shift_ref, h_ref, zero_ref):
    gain_scale = gain_ref[...] * (1.0 + scale_ref[...])
    shift = shift_ref[...]

    def body(i, carry):
        r0 = pl.multiple_of(i * NORM_ROWS, NORM_ROWS)
        x = x_ref[pl.ds(r0, NORM_ROWS), :]
        inv = lax.rsqrt(jnp.mean(x * x, axis=-1, keepdims=True) + EPS)
        h_ref[pl.ds(r0, NORM_ROWS), :] = (x * inv * gain_scale + shift).astype(h_ref.dtype)
        zero_ref[pl.ds(r0, NORM_ROWS), :] = jnp.zeros((NORM_ROWS, zero_ref.shape[1]), zero_ref.dtype)
        return carry

    lax.fori_loop(0, x_ref.shape[0] // NORM_ROWS, body, 0, unroll=4)


def _dot(a, b):
    return jnp.dot(a, b, preferred_element_type=F32)


def _dot_nt(a, b):
    return lax.dot_general(a, b, (((1,), (1,)), ((), ())), preferred_element_type=F32)


def _cast_kernel(w_ref, o_ref):
    o_ref[...] = w_ref[...].astype(o_ref.dtype)


def _cast_call(w_stack, layer, n_rows=None, row_blocks=8):
    _, rows, n_cols = w_stack.shape
    n_rows = rows if n_rows is None else n_rows
    tr = n_rows // row_blocks
    return pl.pallas_call(
        _cast_kernel,
        grid=(row_blocks,),
        in_specs=[pl.BlockSpec((None, tr, n_cols), lambda i: (layer, i, 0))],
        out_specs=pl.BlockSpec((tr, n_cols), lambda i: (i, 0)),
        out_shape=jax.ShapeDtypeStruct((n_rows, n_cols), BF16),
        compiler_params=_cparams(("parallel",)),
        name="weight_cast",
    )(w_stack)


def _ada_kernel(c_ref, w_ref, b_ref, o_ref):
    c = c_ref[...]
    a = (c * jax.nn.sigmoid(c)).astype(BF16)
    o_ref[...] = _dot(a, w_ref[...].astype(BF16)) + b_ref[...]


def _ada_call(cond, w_stack, layer, b, tn=1024):
    m, d = cond.shape
    n = w_stack.shape[2]
    return pl.pallas_call(
        _ada_kernel,
        grid=(n // tn,),
        in_specs=[pl.BlockSpec((m, d), lambda j: (0, 0)),
                  pl.BlockSpec((None, d, tn), lambda j: (layer, 0, j)),
                  pl.BlockSpec((1, tn), lambda j: (0, j))],
        out_specs=pl.BlockSpec((m, tn), lambda j: (0, j)),
        out_shape=jax.ShapeDtypeStruct((m, n), F32),
        compiler_params=_cparams(("arbitrary",)),
        name="ada_mod",
    )(cond, w_stack, b)


def _mod_spec(k, tm, group_rows, fixed_group):
    def index_map(i, *_):
        grp = fixed_group if fixed_group is not None else i // (group_rows // tm)
        return (grp * N_MOD + k, 0, 0)
    return index_map


def _ffn_kernel(x_ref, sh_ref, sc_ref, gt_ref, gain_ref, wg_ref, wu_ref, wd_ref, fin_ref, o_ref,
                h_scr, *, n_ff, final_norm):
    j = pl.program_id(1)

    @pl.when(j == 0)
    def _():
        _norm_modulate_store(x_ref, gain_ref, sc_ref, sh_ref, h_scr, o_ref)

    h = h_scr[...]
    tf = wg_ref.shape[1]
    for c0 in range(0, tf, FFN_SUBTILE):
        g = _dot(h, wg_ref[:, c0:c0 + FFN_SUBTILE])
        u = _dot(h, wu_ref[:, c0:c0 + FFN_SUBTILE])
        a = ((g * jax.nn.sigmoid(g)) * u).astype(BF16)
        for n0 in range(0, o_ref.shape[1], FFN_OUT_SUBTILE):
            o_ref[:, n0:n0 + FFN_OUT_SUBTILE] += _dot(a, wd_ref[c0:c0 + FFN_SUBTILE, n0:n0 + FFN_OUT_SUBTILE])

    @pl.when(j == n_ff - 1)
    def _():
        half_gate = 0.5 * gt_ref[...]

        step_rows = EPILOGUE_ROWS if final_norm else NORM_ROWS

        def body(i, carry):
            rows = [pl.multiple_of(i * step_rows + k, NORM_ROWS) for k in range(0, step_rows, NORM_ROWS)]
            outs = [x_ref[pl.ds(r, NORM_ROWS), :] + half_gate * o_ref[pl.ds(r, NORM_ROWS), :] for r in rows]
            if final_norm:
                outs = [_rms(out, fin_ref[...]) for out in outs]
            for r, out in zip(rows, outs):
                o_ref[pl.ds(r, NORM_ROWS), :] = out
            return carry

        lax.fori_loop(0, o_ref.shape[0] // step_rows, body, 0, unroll=1 if final_norm else 4)


def _ffn_call(x, mod, k0, gain, wg, wu, wd, fin, *, group_rows, fixed_group, final_norm,
              tm=1024, tf=512):
    t, d = x.shape
    dff = wg.shape[1]
    n_ff = dff // tf
    mspec = lambda k: pl.BlockSpec((None, 1, d), _mod_spec(k, tm, group_rows, fixed_group))
    return pl.pallas_call(
        functools.partial(_ffn_kernel, n_ff=n_ff, final_norm=final_norm),
        grid=(t // tm, n_ff),
        in_specs=[pl.BlockSpec((tm, d), lambda i, j: (i, 0)),
                  mspec(k0), mspec(k0 + 1), mspec(k0 + 2),
                  pl.BlockSpec((1, d), lambda i, j: (0, 0)),
                  pl.BlockSpec((d, tf), lambda i, j: (0, j)),
                  pl.BlockSpec((d, tf), lambda i, j: (0, j)),
                  pl.BlockSpec((tf, d), lambda i, j: (j, 0)),
                  pl.BlockSpec((1, d), lambda i, j: (0, 0))],
        out_specs=pl.BlockSpec((tm, d), lambda i, j: (i, 0)),
        out_shape=jax.ShapeDtypeStruct((t, d), F32),
        scratch_shapes=[pltpu.VMEM((tm, d), BF16)],
        compiler_params=_cparams(("parallel", "arbitrary")),
        name="macaron_ffn",
    )(x, mod, mod, mod, gain, wg, wu, wd, fin)


def _rope(x, cos, sin_lo, sin_hi):
    return (x * cos + pltpu.roll(x, HEAD_DIM - HEAD_DIM // 4, 1) * sin_lo
            + pltpu.roll(x, HEAD_DIM // 4, 1) * sin_hi)


def _inproj_kernel(x_ref, sh_ref, sc_ref, gain_ref, wt_ref, wgt_ref, qg_ref, kg_ref, cos_ref, slo_ref, shi_ref,
                   p_ref, gt_ref, h_scr, *, rope):
    j = pl.program_id(1)

    def gqa_head(x, gain, scale):
        y = _rms(x, gain)
        if rope:
            y = _rope(y, cos_ref[...], slo_ref[...], shi_ref[...])
        return y if scale is None else y * scale

    @pl.when(j == 0)
    def _():
        h = _rms(x_ref[...], gain_ref[...]) * (1.0 + sc_ref[...]) + sh_ref[...]
        hb = h.astype(BF16)
        h_scr[...] = hb
        gt_ref[...] = _dot_nt(wgt_ref[...].astype(BF16), hb)
        acc = _dot_nt(hb, wt_ref[...])
        p_ref[:, :COL_GQ_Q] = acc[:, :COL_GQ_Q].astype(p_ref.dtype)
        for c0, n_heads, gain_ref_h, scale in ((COL_GQ_Q, GQA_Q_HEADS, qg_ref, HEAD_DIM ** -0.5),
                                               (COL_GQ_K, GQA_KV_HEADS, kg_ref, None)):
            for hd in range(n_heads):
                sl = slice(c0 + hd * HEAD_DIM, c0 + (hd + 1) * HEAD_DIM)
                p_ref[:, sl] = gqa_head(acc[:, sl], gain_ref_h[...], scale).astype(p_ref.dtype)
        p_ref[:, COL_GQ_V:] = acc[:, COL_GQ_V:].astype(p_ref.dtype)

    @pl.when(j != 0)
    def _():
        p_ref[...] = _dot_nt(h_scr[...], wt_ref[...]).astype(p_ref.dtype)


def _inproj_call(x, mod, gain, w_main_t, w_gates_t, q_gain, k_gain, rope_tables, seq, *, rope, group_rows,
                 fixed_group, tm=512, tn=3328):
    t, d = x.shape
    n = w_main_t.shape[0]
    assert tn >= COL_GQ_V, "mixer B's q / k columns must sit in the first column tile"
    mspec = lambda k: pl.BlockSpec((None, 1, d), _mod_spec(k, tm, group_rows, fixed_group))
    per_seq = seq // tm if rope else 1
    tab_spec = pl.BlockSpec((tm, HEAD_DIM), lambda i, j: (i % per_seq, 0))
    head_row = pl.BlockSpec((1, HEAD_DIM), lambda i, j: (0, 0))
    return pl.pallas_call(
        functools.partial(_inproj_kernel, rope=rope),
        grid=(t // tm, n // tn),
        in_specs=[pl.BlockSpec((tm, d), lambda i, j: (i, 0)),
                  mspec(3), mspec(4),
                  pl.BlockSpec((1, d), lambda i, j: (0, 0)),
                  pl.BlockSpec((tn, d), lambda i, j: (j, 0)),
                  pl.BlockSpec((N_GATES, d), lambda i, j: (0, 0)),
                  head_row, head_row, tab_spec, tab_spec, tab_spec],
        out_specs=[pl.BlockSpec((tm, tn), lambda i, j: (i, j)),
                   pl.BlockSpec((N_GATES, tm), lambda i, j: (0, i))],
        out_shape=[jax.ShapeDtypeStruct((t, n), BF16),
                   jax.ShapeDtypeStruct((N_GATES, t), F32)],
        scratch_shapes=[pltpu.VMEM((tm, d), BF16)],
        compiler_params=_cparams(("parallel", "arbitrary")),
        name="mixer_in_proj",
    )(x, mod, mod, gain, w_main_t, w_gates_t, q_gain, k_gain, *rope_tables)


def _outproj_kernel(x_ref, gt_ref, ya_ref, yb_ref, yc_ref, w_ref, o_ref):
    acc = _dot(ya_ref[...], w_ref[0:NA_W, :])
    acc += _dot(yb_ref[...], w_ref[NA_W:NA_W + GQA_W, :])
    acc += _dot(yc_ref[...], w_ref[NA_W + GQA_W:, :])
    o_ref[...] = x_ref[...] + gt_ref[...] * acc


def _outproj_call(x, mod, ya, yb, yc, w, *, group_rows, fixed_group, tm=512):
    t, d = x.shape
    return pl.pallas_call(
        _outproj_kernel,
        grid=(t // tm,),
        in_specs=[pl.BlockSpec((tm, d), lambda i: (i, 0)),
                  pl.BlockSpec((None, 1, d), _mod_spec(5, tm, group_rows, fixed_group)),
                  pl.BlockSpec((tm, NA_W), lambda i: (i, 0)),
                  pl.BlockSpec((tm, GQA_W), lambda i: (i, 0)),
                  pl.BlockSpec((tm, ML_W), lambda i: (i, 0)),
                  pl.BlockSpec((d, d), lambda i: (0, 0))],
        out_specs=pl.BlockSpec((tm, d), lambda i: (i, 0)),
        out_shape=jax.ShapeDtypeStruct((t, d), F32),
        compiler_params=_cparams(("parallel",)),
        name="mixer_out_proj",
    )(x, mod, ya, yb, yc, w)


def _na_kernel(q_ref, k_ref, v_ref, kc_ref, vc_ref, tl_ref, tr_ref, o_ref, mb_scr, *, rows):
    rb = pl.program_id(2)
    nblk = rows // NA_QROWS
    key_row0 = jnp.clip(rb * NA_QROWS - NA_WIN_R // 2, 0, rows - NA_KROWS)

    @pl.when((rb <= 1) | (rb == nblk - 1))
    def _():
        for a in range(NA_QROWS):
            r = rb * NA_QROWS + a
            row_start = jnp.clip(r - NA_WIN_R // 2, 0, rows - NA_WIN_R)
            for j in range(NA_KROWS // 2):
                sel = []
                for kr in (key_row0 + 2 * j, key_row0 + 2 * j + 1):
                    in_win = (kr >= row_start) & (kr < row_start + NA_WIN_R)
                    sel.append(jnp.where(in_win, kr - r + (NA_WIN_R - 1), NA_MASKED_SLAB))
                mb_scr[a * GRID_W:(a + 1) * GRID_W, 2 * j * GRID_W:(2 * j + 2) * GRID_W] = (
                    tl_ref[sel[0]] + tr_ref[sel[1]])

    start = pl.multiple_of(key_row0 * GRID_W, GRID_W)
    q = (q_ref[...].astype(F32) * (HEAD_DIM ** -0.5)).astype(BF16)
    n_keys = NA_KROWS * GRID_W
    key_tile = lambda t0: k_ref[pl.ds(start + t0, NA_KEY_TILE), :]
    s = _dot_nt(q, kc_ref[...])
    s_next = _dot_nt(q, key_tile(0))
    m = jnp.max(s, axis=-1, keepdims=True)
    p = jnp.exp(s - m)
    den = jnp.sum(p, axis=-1, keepdims=True)
    acc = _dot(p.astype(BF16), vc_ref[...])
    for t0 in range(0, n_keys, NA_KEY_TILE):
        s = s_next + mb_scr[:, t0:t0 + NA_KEY_TILE]
        if t0 + NA_KEY_TILE < n_keys:
            s_next = _dot_nt(q, key_tile(t0 + NA_KEY_TILE))
        m_new = jnp.maximum(m, jnp.max(s, axis=-1, keepdims=True))
        p = jnp.exp(s - m_new)
        alpha = jnp.exp(m - m_new)
        den = alpha * den + jnp.sum(p, axis=-1, keepdims=True)
        acc = alpha * acc + _dot(p.astype(BF16), v_ref[pl.ds(start + t0, NA_KEY_TILE), :])
        m = m_new
    o_ref[...] = (acc / den).astype(o_ref.dtype)


def _na_toeplitz_slabs(rpb):
    n_h, n_dr, n_dc = rpb.shape
    edge = GRID_W - NA_WIN_C
    ext = jnp.concatenate(
        [rpb[..., NA_WIN_C - 1:], jnp.broadcast_to(rpb[..., n_dc - 1:], (n_h, n_dr, edge)),
         jnp.zeros((n_h, n_dr, 1), F32),
         jnp.broadcast_to(rpb[..., :1], (n_h, n_dr, edge)), rpb[..., :NA_WIN_C - 1]], axis=-1)
    width = 2 * GRID_W
    tiled = jnp.broadcast_to(ext[:, :, None, :], (n_h, n_dr, GRID_W, width)).reshape(n_h, n_dr, GRID_W * width)
    toep = tiled[..., :GRID_W * (width - 1)].reshape(n_h, n_dr, GRID_W, width - 1)[..., :GRID_W]
    cq = np.arange(GRID_W)
    col_start = np.clip(cq - NA_WIN_C // 2, 0, GRID_W - NA_WIN_C)
    col_ok = (cq[None, :] >= col_start[:, None]) & (cq[None, :] < col_start[:, None] + NA_WIN_C)
    slabs = jnp.where(col_ok, toep, NEG_BIG)
    slabs = jnp.concatenate([slabs, jnp.full((n_h, 1, GRID_W, GRID_W), NEG_BIG, F32)], axis=1)
    zeros = jnp.zeros_like(slabs)
    return jnp.concatenate([slabs, zeros], axis=-1), jnp.concatenate([zeros, slabs], axis=-1)


def _na_call(p_lat, p_ctx, slabs_left, slabs_right, batch, seq, ctx_len):
    rows = seq // GRID_W
    nblk = rows // NA_QROWS
    tq = NA_QROWS * GRID_W
    nk = NA_KROWS * GRID_W
    cq, ck, cv = COL_NA_Q // HEAD_DIM, COL_NA_K // HEAD_DIM, COL_NA_V // HEAD_DIM
    slab_spec = pl.BlockSpec((None, NA_MASKED_SLAB + 1, GRID_W, 2 * GRID_W), lambda b, h, r: (h, 0, 0, 0))

    return pl.pallas_call(
        functools.partial(_na_kernel, rows=rows),
        grid=(batch, NA_HEADS, nblk),
        in_specs=[pl.BlockSpec((tq, HEAD_DIM), lambda b, h, r: (b * nblk + r, cq + h)),
                  pl.BlockSpec((seq, HEAD_DIM), lambda b, h, r: (b, ck + h)),
                  pl.BlockSpec((seq, HEAD_DIM), lambda b, h, r: (b, cv + h)),
                  pl.BlockSpec((ctx_len, HEAD_DIM), lambda b, h, r: (b, ck + h)),
                  pl.BlockSpec((ctx_len, HEAD_DIM), lambda b, h, r: (b, cv + h)),
                  slab_spec, slab_spec],
        out_specs=pl.BlockSpec((tq, HEAD_DIM), lambda b, h, r: (b * nblk + r, h)),
        out_shape=jax.ShapeDtypeStruct((batch * seq, NA_W), BF16),
        scratch_shapes=[pltpu.VMEM((tq, nk), F32)],
        compiler_params=_cparams(("parallel", "parallel", "arbitrary")),
        name="neighbourhood_attn",
    )(p_lat, p_lat, p_lat, p_ctx, p_ctx, slabs_left, slabs_right)


def _ctx_attn_kernel(q_ref, k_ref, v_ref, o_ref, *, scale):
    s = _dot_nt(q_ref[...], k_ref[...])
    if scale is not None:
        s = s * scale
    m = jnp.max(s, axis=-1, keepdims=True)
    p = jnp.exp(s - m)
    den = jnp.sum(p, axis=-1, keepdims=True)
    o_ref[...] = (_dot(p.astype(BF16), v_ref[...]) / den).astype(o_ref.dtype)


def _ctx_attn_call(q_arr, k_arr, v_arr, batch, ctx_len, n_heads, q_col, k_col, v_col, kv_group, scale):
    return pl.pallas_call(
        functools.partial(_ctx_attn_kernel, scale=scale),
        grid=(batch, n_heads),
        in_specs=[pl.BlockSpec((ctx_len, HEAD_DIM), lambda b, h: (b, q_col + h)),
                  pl.BlockSpec((ctx_len, HEAD_DIM), lambda b, h: (b, k_col + h // kv_group)),
                  pl.BlockSpec((ctx_len, HEAD_DIM), lambda b, h: (b, v_col + h // kv_group))],
        out_specs=pl.BlockSpec((ctx_len, HEAD_DIM), lambda b, h: (b, h)),
        out_shape=jax.ShapeDtypeStruct((batch * ctx_len, n_heads * HEAD_DIM), BF16),
        compiler_params=_cparams(("parallel", "parallel")),
        name="ctx_attn",
    )(q_arr, k_arr, v_arr)


def _rope_tables(n_tokens):
    t = jnp.arange(n_tokens)
    row = (t // GRID_W).astype(F32)
    col = (t % GRID_W).astype(F32)
    half = HEAD_DIM // 2
    inv_freq = 1.0 / (ROPE_THETA ** (jnp.arange(0, half, 2, dtype=F32) / half))
    ang_r = row[:, None] * inv_freq[None, :]
    ang_c = col[:, None] * inv_freq[None, :]
    ang = jnp.concatenate([ang_r, ang_r, ang_c, ang_c], axis=-1)
    cos, sin = jnp.cos(ang), jnp.sin(ang)
    lo = (jnp.arange(HEAD_DIM) % half) < (half // 2)
    return cos, jnp.where(lo, -sin, 0.0), jnp.where(lo, 0.0, sin)


def _gqa_kernel(q_ref, kl_ref, vl_ref, kc_ref, vc_ref, o_ref, *, tk):
    tq = q_ref.shape[0]
    grp = q_ref.shape[1] // HEAD_DIM
    q = jnp.concatenate([q_ref[:, g * HEAD_DIM:(g + 1) * HEAD_DIM] for g in range(grp)], axis=0)

    n_tiles = kl_ref.shape[0] // tk
    s = _dot_nt(q, kc_ref[...])
    s_next = _dot_nt(q, kl_ref[0:tk, :])
    m = jnp.max(s, axis=-1, keepdims=True)
    p = jnp.exp(s - m)
    den = jnp.sum(p, axis=-1, keepdims=True)
    acc = _dot(p.astype(BF16), vc_ref[...])
    for t in range(n_tiles):
        s = s_next
        if t + 1 < n_tiles:
            s_next = _dot_nt(q, kl_ref[(t + 1) * tk:(t + 2) * tk, :])
        m_new = jnp.maximum(m, jnp.max(s, axis=-1, keepdims=True))
        p = jnp.exp(s - m_new)
        alpha = jnp.exp(m - m_new)
        den = alpha * den + jnp.sum(p, axis=-1, keepdims=True)
        acc = alpha * acc + _dot(p.astype(BF16), vl_ref[t * tk:(t + 1) * tk, :])
        m = m_new
    out = (acc / den).astype(o_ref.dtype)
    for g in range(grp):
        o_ref[:, g * HEAD_DIM:(g + 1) * HEAD_DIM] = out[g * tq:(g + 1) * tq]


def _gqa_call(p_lat, p_ctx, batch, seq, ctx_len, tq=512, tk=2048):
    grp = GQA_Q_HEADS // GQA_KV_HEADS
    nq = seq // tq
    cq = COL_GQ_Q // (grp * HEAD_DIM)
    ck, cv = COL_GQ_K // HEAD_DIM, COL_GQ_V // HEAD_DIM
    return pl.pallas_call(
        functools.partial(_gqa_kernel, tk=tk),
        grid=(batch, GQA_KV_HEADS, nq),
        in_specs=[pl.BlockSpec((tq, grp * HEAD_DIM), lambda b, h, i: (b * nq + i, cq + h)),
                  pl.BlockSpec((seq, HEAD_DIM), lambda b, h, i: (b, ck + h)),
                  pl.BlockSpec((seq, HEAD_DIM), lambda b, h, i: (b, cv + h)),
                  pl.BlockSpec((ctx_len, HEAD_DIM), lambda b, h, i: (b, ck + h)),
                  pl.BlockSpec((ctx_len, HEAD_DIM), lambda b, h, i: (b, cv + h))],
        out_specs=pl.BlockSpec((tq, grp * HEAD_DIM), lambda b, h, i: (b * nq + i, h)),
        out_shape=jax.ShapeDtypeStruct((batch * seq, GQA_W), BF16),
        compiler_params=_cparams(("parallel", "parallel", "arbitrary")),
        name="gqa_attn",
    )(p_lat, p_lat, p_lat, p_ctx, p_ctx)


ST_R, ST_W, ST_BTOT, ST_AMAX, ST_MPREV, ST_DEC, ST_INP, ST_BIGR, ST_M = range(9)
N_STATS = 9
REP_R, REP_M = range(2)
ML_AUG = 2 * HEAD_DIM
ML_GROUP = 8


def _mlstm_kernel(ql_ref, kl_ref, vl_ref, ol_ref, qc_ref, kc_ref, vc_ref, oc_ref, g_ref, gb_ref,
                  cwq_ref, cwk_ref, cbq_ref, cbk_ref, on_ref, yl_ref, *rest,
                  n_lat, n_ctx, need_ctx):
    if need_ctx:
        yc_ref = rest[0]
        rest = rest[1:]
    qa_l, kt_l, va_l, qa_c, kt_c, va_c, hf_l, hb_l, hf_c, hb_c, st, rep = rest
    L = ML_CHUNK
    n_all = n_lat + n_ctx
    h = pl.program_id(1)
    row = lax.broadcasted_iota(jnp.int32, (L, L), 0)
    col = lax.broadcasted_iota(jnp.int32, (L, L), 1)

    tri_fw = (row <= col).astype(F32)
    tri_bw = (row >= col).astype(F32)
    b_cum = []
    for d, tri in enumerate((tri_fw, tri_bw)):
        log_i = g_ref[2 * d * ML_HEADS + h] + gb_ref[2 * d * ML_HEADS + h]
        log_f = jax.nn.log_sigmoid(g_ref[(2 * d + 1) * ML_HEADS + h] + gb_ref[(2 * d + 1) * ML_HEADS + h])
        b = jnp.dot(log_f, tri, preferred_element_type=F32, precision=lax.Precision.HIGHEST)
        b_tot = b[:, L - 1:L] if d == 0 else b[:, 0:1]
        a = b_tot - b + log_i
        a_max = jnp.max(a, axis=-1, keepdims=True)
        r = log_i - b
        lane = lax.broadcasted_iota(jnp.int32, r.shape, 1)
        run = r
        shift = 1
        while shift < L:
            if d == 0:
                run = jnp.maximum(run, jnp.where(lane >= shift, pltpu.roll(run, shift, 1), NEG_BIG))
            else:
                run = jnp.maximum(run, jnp.where(lane < L - shift, pltpu.roll(run, L - shift, 1), NEG_BIG))
            shift *= 2
        base = d * N_STATS
        st[base + ST_R] = r
        st[base + ST_W] = jnp.exp(a - a_max)
        st[base + ST_BTOT] = jnp.broadcast_to(b_tot, b.shape)
        st[base + ST_AMAX] = jnp.broadcast_to(a_max, b.shape)
        st[base + ST_BIGR] = run
        st[base + ST_MPREV] = jnp.zeros_like(b)
        b_cum.append(b)

    def stabiliser_step(i, carry):
        new = []
        for d, m_prev in enumerate(carry):
            in_ctx = i < n_ctx
            j = jnp.where(in_ctx, i, i - n_ctx)
            if d == 0:
                n = jnp.where(in_ctx, n_lat + j, j)
            else:
                n = jnp.where(in_ctx, n_lat + n_ctx - 1 - j, n_lat - 1 - j)
            base = d * N_STATS
            b_tot = st[base + ST_BTOT, pl.ds(n, 1), :]
            a_max = st[base + ST_AMAX, pl.ds(n, 1), :]
            m_new = jnp.maximum(b_tot + m_prev, a_max)
            st[base + ST_MPREV, pl.ds(n, 1), :] = m_prev
            st[base + ST_DEC, pl.ds(n, 1), :] = jnp.exp(b_tot + m_prev - m_new)
            st[base + ST_INP, pl.ds(n, 1), :] = jnp.exp(a_max - m_new)
            new.append(m_new)
        return tuple(new)

    lax.fori_loop(0, n_all, stabiliser_step, (jnp.zeros((1, L), F32), jnp.zeros((1, L), F32)), unroll=True)

    for d in range(2):
        base = d * N_STATS
        big_r = jnp.maximum(st[base + ST_BIGR], st[base + ST_MPREV])
        st[base + ST_BIGR] = big_r
        st[base + ST_M] = b_cum[d] + big_r

    def column_tables(n):
        for d in range(2):
            for k_rep, k_st in ((REP_R, ST_BIGR), (REP_M, ST_M)):
                rows = jnp.broadcast_to(st[d * N_STATS + k_st, pl.ds(n, 1), :], (L, L))
                rep[2 * d + k_rep, n] = rows.T

    def prepare_stream(q_src, k_src, v_src, q_dst, kt_dst, va_dst, n_chunks, stat0):
        n_tok = n_chunks * L
        sub = lax.broadcasted_iota(jnp.int32, (L, HEAD_DIM), 0)

        def conv_silu(src_ref, w_ref, b_ref, n, t0):
            x = src_ref[pl.ds(t0, L), :].astype(F32)
            prev_t0 = pl.multiple_of(jnp.maximum(t0 - BF16_ROWS, 0), BF16_ROWS)
            next_t0 = pl.multiple_of(jnp.minimum(t0 + L, n_tok - BF16_ROWS), BF16_ROWS)
            prev_row = (src_ref[pl.ds(prev_t0, BF16_ROWS), :].astype(F32)[BF16_ROWS - 1:, :]
                        * jnp.where(n > 0, 1.0, 0.0))
            next_row = (src_ref[pl.ds(next_t0, BF16_ROWS), :].astype(F32)[:1, :]
                        * jnp.where(n < n_chunks - 1, 1.0, 0.0))
            x_prev = jnp.where(sub == 0, prev_row, pltpu.roll(x, 1, 0))
            x_next = jnp.where(sub == L - 1, next_row, pltpu.roll(x, L - 1, 0))
            y = w_ref[0:1, :] * x_prev + w_ref[1:2, :] * x + w_ref[2:3, :] * x_next + b_ref[...]
            return y * jax.nn.sigmoid(y)

        def body(n, carry):
            t0 = pl.multiple_of(n * L, L)
            q_dst[pl.ds(t0, L), :] = conv_silu(q_src, cwq_ref, cbq_ref, n, t0).astype(BF16)
            k_act = conv_silu(k_src, cwk_ref, cbk_ref, n, t0) * (HEAD_DIM ** -0.5)
            kt_dst[:, pl.ds(t0, L)] = k_act.T
            va_dst[pl.ds(t0, L), 0:HEAD_DIM] = v_src[pl.ds(t0, L), :]
            va_dst[pl.ds(t0, L), HEAD_DIM:ML_AUG] = jnp.ones((L, HEAD_DIM), BF16)
            column_tables(stat0 + n)
            return carry

        lax.fori_loop(0, n_chunks, body, 0, unroll=min(8, n_chunks))

    prepare_stream(ql_ref, kl_ref, vl_ref, qa_l, kt_l, va_l, n_lat, 0)
    prepare_stream(qc_ref, kc_ref, vc_ref, qa_c, kt_c, va_c, n_ctx, n_lat)

    def stat(d, n, k):
        return st[d * N_STATS + k, pl.ds(n, 1), :]

    both = lambda x: jnp.concatenate([x, x], axis=-1)

    def make_body(group, n_chunks, stat0, q_src, kt_src, va_src, h_dst, want_out):
        def body(i, carry):
            items = []
            for u in range(group):
                nf = i * group + u
                for d, n in ((0, nf), (1, n_chunks - 1 - nf)):
                    items.append((d, stat0 + n, pl.multiple_of(n * L, L)))
            va = [va_src[pl.ds(t0, L), :] for _, _, t0 in items]
            c_loc, qk, q = [], [], []
            for (d, ns, t0), v in zip(items, va):
                kt = kt_src[:, pl.ds(t0, L)]
                c_loc.append(_dot((kt * stat(d, ns, ST_W)).astype(BF16), v))
                if want_out:
                    q.append(q_src[pl.ds(t0, L), :])
                    qk.append(_dot(q[-1], kt.astype(BF16)))
            lhs = []
            if want_out:
                for (d, ns, t0), qi, qki in zip(items, q, qk):
                    seen = (col <= row) if d == 0 else (col >= row)
                    big_r = rep[2 * d + REP_R, ns]
                    s = qki * jnp.exp(jnp.where(seen, stat(d, ns, ST_R) - big_r, NEG_BIG))
                    g = jnp.exp(stat(d, ns, ST_MPREV) - big_r)
                    lhs.append(jnp.concatenate([(g * qi.astype(F32)).astype(BF16), s.astype(BF16)], axis=1))
            state = list(carry)
            for idx, (d, ns, t0) in enumerate(items):
                if want_out:
                    acc = _dot(lhs[idx], jnp.concatenate([state[d].astype(BF16), va[idx]], axis=0))
                    floor = jnp.exp(-rep[2 * d + REP_M, ns])
                    h_dst[d][pl.ds(t0, L), :] = (acc[:, :HEAD_DIM]
                                                 / jnp.maximum(jnp.abs(acc[:, HEAD_DIM:]), floor))
                state[d] = both(stat(d, ns, ST_DEC)) * state[d] + both(stat(d, ns, ST_INP)) * c_loc[idx]
            return tuple(state)
        return body

    def finish(hsum, o_src, t0):
        y = _rms(hsum, on_ref[...]) * jax.nn.sigmoid(o_src[pl.ds(t0, L), :].astype(F32))
        return y.astype(BF16)

    zero_state = jnp.zeros((HEAD_DIM, ML_AUG), F32)
    carry = make_body(n_ctx, n_ctx, n_lat, qa_c, kt_c, va_c, (hf_c, hb_c), need_ctx)(0, (zero_state, zero_state))
    lax.fori_loop(0, n_lat // ML_GROUP, make_body(ML_GROUP, n_lat, 0, qa_l, kt_l, va_l, (hf_l, hb_l), True),
                  carry)

    def finish_lat(n, carry):
        t0 = pl.multiple_of(n * L, L)
        yl_ref[pl.ds(t0, L), :] = finish(hf_l[pl.ds(t0, L), :] + hb_l[pl.ds(t0, L), :], ol_ref, t0)
        return carry

    lax.fori_loop(0, n_lat, finish_lat, 0, unroll=4)
    if need_ctx:
        def finish_ctx(n, carry):
            t0 = pl.multiple_of(n * L, L)
            yc_ref[pl.ds(t0, L), :] = finish(hf_c[pl.ds(t0, L), :] + hb_c[pl.ds(t0, L), :], oc_ref, t0)
            return carry

        lax.fori_loop(0, n_ctx, finish_ctx, 0, unroll=True)


def _mlstm_call(p_lat, p_ctx, gates, gate_b, conv_w, conv_b, out_norm, batch, seq, ctx_len, need_ctx):
    n_lat, n_ctx = seq // ML_CHUNK, ctx_len // ML_CHUNK
    n_pad = gates.shape[2]
    cq, ck = COL_ML_Q // HEAD_DIM, COL_ML_K // HEAD_DIM
    cv, co = COL_ML_V // HEAD_DIM, COL_ML_O // HEAD_DIM
    col = lambda rows, c0: pl.BlockSpec((rows, HEAD_DIM), lambda b, h: (b, c0 + h))
    out_specs = [pl.BlockSpec((seq, HEAD_DIM), lambda b, h: (b, h))]
    out_shape = [jax.ShapeDtypeStruct((batch * seq, ML_W), BF16)]
    if need_ctx:
        out_specs.append(pl.BlockSpec((ctx_len, HEAD_DIM), lambda b, h: (b, h)))
        out_shape.append(jax.ShapeDtypeStruct((batch * ctx_len, ML_W), BF16))

    def stream_scratch(n_tok):
        return [pltpu.VMEM((n_tok, HEAD_DIM), BF16), pltpu.VMEM((HEAD_DIM, n_tok), F32),
                pltpu.VMEM((n_tok, ML_AUG), BF16)]

    res = pl.pallas_call(
        functools.partial(_mlstm_kernel, n_lat=n_lat, n_ctx=n_ctx, need_ctx=need_ctx),
        grid=(batch, ML_HEADS),
        in_specs=[col(seq, cq), col(seq, ck), col(seq, cv), col(seq, co),
                  col(ctx_len, cq), col(ctx_len, ck), col(ctx_len, cv), col(ctx_len, co),
                  pl.BlockSpec((N_GATES, None, n_pad, ML_CHUNK), lambda b, h: (0, b, 0, 0)),
                  pl.BlockSpec((N_GATES, 1, ML_CHUNK), lambda b, h: (0, 0, 0)),
                  pl.BlockSpec((3, HEAD_DIM), lambda b, h: (0, h)),
                  pl.BlockSpec((3, HEAD_DIM), lambda b, h: (0, ML_HEADS + h)),
                  pl.BlockSpec((1, HEAD_DIM), lambda b, h: (0, h)),
                  pl.BlockSpec((1, HEAD_DIM), lambda b, h: (0, ML_HEADS + h)),
                  pl.BlockSpec((1, HEAD_DIM), lambda b, h: (0, h))],
        out_specs=out_specs,
        out_shape=out_shape,
        scratch_shapes=(stream_scratch(seq) + stream_scratch(ctx_len)
                        + [pltpu.VMEM((seq, HEAD_DIM), F32), pltpu.VMEM((seq, HEAD_DIM), F32),
                           pltpu.VMEM((ctx_len, HEAD_DIM), F32), pltpu.VMEM((ctx_len, HEAD_DIM), F32),
                           pltpu.VMEM((2 * N_STATS, n_pad, ML_CHUNK), F32),
                           pltpu.VMEM((4, n_lat + n_ctx, ML_CHUNK, ML_CHUNK), F32)]),
        compiler_params=_cparams(("parallel", "parallel")),
        name="mlstm",
    )(p_lat, p_lat, p_lat, p_lat, p_ctx, p_ctx, p_ctx, p_ctx, gates, gate_b,
      conv_w, conv_w, conv_b, conv_b, out_norm)
    return (res[0], res[1]) if need_ctx else (res[0], None)


def kernel(x, c, ctx, c_ctx, w_ada, b_ada, norm_ff1, ff1_gate, ff1_up, ff1_down, norm_mix, w_in, na_rpb,
           gqa_q_norm, gqa_k_norm, ml_conv_w, ml_conv_b, ml_gate_b, ml_out_norm, w_out, norm_ff2,
           ff2_gate, ff2_up, ff2_down, final_norm):
    batch, seq, d = x.shape
    ctx_len = ctx.shape[1]
    depth = w_ada.shape[0]
    ctx_group = batch
    n_lat_chunks, n_ctx_chunks = seq // ML_CHUNK, ctx_len // ML_CHUNK
    n_chunk_pad = -(-(n_lat_chunks + n_ctx_chunks) // 8) * 8

    xl = x.reshape(batch * seq, d)
    xc = ctx.reshape(batch * ctx_len, d)
    cond = jnp.zeros((8, d), F32).at[:batch].set(c).at[batch].set(c_ctx)
    rope_tabs = _rope_tables(seq)
    w_in_t = jnp.swapaxes(w_in, 1, 2)
    row2 = lambda v: v.reshape(1, -1)
    fin = row2(final_norm)

    for l in range(depth):
        last = l == depth - 1
        mod = _ada_call(cond, w_ada, l, row2(b_ada[l])).reshape(8 * N_MOD, 1, d)
        lat = dict(group_rows=seq, fixed_group=None)
        cx = dict(group_rows=None, fixed_group=ctx_group)

        cast = lambda w: _cast_call(w, l)
        wg, wu, wd = cast(ff1_gate), cast(ff1_up), cast(ff1_down)
        g1 = row2(norm_ff1[l])
        xl = _ffn_call(xl, mod, 0, g1, wg, wu, wd, fin, final_norm=False, **lat)
        xc = _ffn_call(xc, mod, 0, g1, wg, wu, wd, fin, final_norm=False, **cx)

        w_main_t = _cast_call(w_in_t, l, COL_GATES)
        w_gates_t = w_in_t[l, COL_GATES:]
        gm = row2(norm_mix[l])
        qg, kg = row2(gqa_q_norm[l]), row2(gqa_k_norm[l])
        p_lat, gt_lat = _inproj_call(xl, mod, gm, w_main_t, w_gates_t, qg, kg, rope_tabs, seq, rope=True, **lat)
        p_ctx, gt_ctx = _inproj_call(xc, mod, gm, w_main_t, w_gates_t, qg, kg, rope_tabs, seq, rope=False, **cx)

        ya_lat = _na_call(p_lat, p_ctx, *_na_toeplitz_slabs(na_rpb[l]), batch, seq, ctx_len)
        yb_lat = _gqa_call(p_lat, p_ctx, batch, seq, ctx_len)
        gates = jnp.concatenate(
            [gt_lat.reshape(N_GATES, batch, n_lat_chunks, ML_CHUNK),
             gt_ctx.reshape(N_GATES, batch, n_ctx_chunks, ML_CHUNK),
             jnp.zeros((N_GATES, batch, n_chunk_pad - n_lat_chunks - n_ctx_chunks, ML_CHUNK), F32)], axis=2)
        gate_b = jnp.broadcast_to(ml_gate_b[l][:, None, None], (N_GATES, 1, ML_CHUNK))
        yc_lat, yc_ctx = _mlstm_call(p_lat, p_ctx, gates, gate_b, ml_conv_w[l], row2(ml_conv_b[l]),
                                     row2(ml_out_norm[l]), batch, seq, ctx_len, not last)

        wo = cast(w_out)
        xl = _outproj_call(xl, mod, ya_lat, yb_lat, yc_lat, wo, **lat)
        wg, wu, wd = cast(ff2_gate), cast(ff2_up), cast(ff2_down)
        g2 = row2(norm_ff2[l])
        xl = _ffn_call(xl, mod, 6, g2, wg, wu, wd, fin, final_norm=last, **lat)
        if not last:
            ya_ctx = _ctx_attn_call(p_ctx, p_ctx, p_ctx, batch, ctx_len, NA_HEADS,
                                    COL_NA_Q // HEAD_DIM, COL_NA_K // HEAD_DIM, COL_NA_V // HEAD_DIM, 1,
                                    HEAD_DIM ** -0.5)
            yb_ctx = _ctx_attn_call(p_ctx, p_ctx, p_ctx, batch, ctx_len, GQA_Q_HEADS,
                                    COL_GQ_Q // HEAD_DIM, COL_GQ_K // HEAD_DIM, COL_GQ_V // HEAD_DIM,
                                    GQA_Q_HEADS // GQA_KV_HEADS, None)
            xc = _outproj_call(xc, mod, ya_ctx, yb_ctx, yc_ctx, wo, **cx)
            xc = _ffn_call(xc, mod, 6, g2, wg, wu, wd, fin, final_norm=False, **cx)
    return xl.reshape(batch, seq, d)
```
